```python
import jax
import jax.numpy as jnp
from jax import lax
import numpy as np

D_MODEL = 1024
BATCH = 2
SEQ = 16384
DEPTH = 2

CHUNK = 64
EPS = 1e-6
NEG_INF = -1e30

RET_HEADS = 4
RET_QK_DIM = 128
RET_V_DIM = 256
ATT_HEADS = 8
ATT_HEAD_DIM = 64
ATT_PAST_CHUNKS = 8
MAX_REL = 128
SGU_BLOCK = 128
SGU_GROUPS = 8
SGU_WIDTH = 2048
FFN_HIDDEN = 2816
CONV_WIDTH = 3

RET_QK_W = RET_HEADS * RET_QK_DIM
RET_V_W = RET_HEADS * RET_V_DIM
ATT_W = ATT_HEADS * ATT_HEAD_DIM
AB_IN_W = 2 * RET_QK_W + 2 * RET_V_W + 3 * ATT_W
AB_OUT_W = RET_V_W + ATT_W
N_EVEN = (DEPTH + 1) // 2
N_ODD = DEPTH // 2

kernel_name = "hybrid_retention_chunkattn_gmlp_convffn"


def rms_norm(x, g):
    xf = x.astype(jnp.float32)
    y = xf * lax.rsqrt(jnp.mean(xf * xf, axis=-1, keepdims=True) + EPS)
    return (y * g.astype(jnp.float32)).astype(x.dtype)


def layer_norm(x, g, b):
    xf = x.astype(jnp.float32)
    mu = jnp.mean(xf, axis=-1, keepdims=True)
    var = jnp.mean(jnp.square(xf - mu), axis=-1, keepdims=True)
    y = (xf - mu) * lax.rsqrt(var + EPS)
    return (y * g.astype(jnp.float32) + b.astype(jnp.float32)).astype(x.dtype)


def rotary(x, pos):
    half = x.shape[-1] // 2
    inv = 1.0 / (10000.0 ** jnp.linspace(0.0, 1.0, half, dtype=jnp.float32))
    ang = pos.astype(jnp.float32)[:, None] * inv[None, :]
    cos = jnp.cos(ang)[None, :, None, :]
    sin = jnp.sin(ang)[None, :, None, :]
    xf = x.astype(jnp.float32)
    x1, x2 = xf[..., :half], xf[..., half:]
    return jnp.concatenate([x1 * cos - x2 * sin, x1 * sin + x2 * cos], axis=-1)


def retention(q, k, v):
    B, T, H, dk = q.shape
    dv = v.shape[-1]
    nc = T // CHUNK
    f32 = jnp.float32
    log_g = jnp.log1p(-jnp.exp2(-5.0 - jnp.arange(H, dtype=f32)))
    qc = q.astype(f32).reshape(B, nc, CHUNK, H, dk)
    kc = (k.astype(f32) * dk ** -0.5).reshape(B, nc, CHUNK, H, dk)
    vc = v.astype(f32).reshape(B, nc, CHUNK, H, dv)
    idx = jnp.arange(CHUNK, dtype=f32)
    d_intra = jnp.exp(log_g[:, None, None] * jnp.abs(idx[:, None] - idx[None, :]))
    s = jnp.einsum('bnihd,bnjhd->bnhij', qc, kc) * d_intra
    o_intra = jnp.einsum('bnhij,bnjhe->bnihe', s, vc)
    k_dec = jnp.exp(log_g[None, :] * (CHUNK - 1 - idx)[:, None])
    q_dec = jnp.exp(log_g[None, :] * (idx + 1.0)[:, None])
    chunk_dec = jnp.exp(log_g * CHUNK)

    def step(state, inp):
        qn, kn, vn = inp
        o = jnp.einsum('bihd,bhde->bihe', qn * q_dec[None, :, :, None], state)
        state = state * chunk_dec[None, :, None, None] + jnp.einsum(
            'bjhd,bjhe->bhde', kn * k_dec[None, :, :, None], vn)
        return state, o

    init = jnp.zeros((B, H, dk, dv), f32)
    _, o_inter = lax.scan(step, init, (qc.swapaxes(0, 1), kc.swapaxes(0, 1), vc.swapaxes(0, 1)))
    o = o_intra + o_inter.swapaxes(0, 1)
    return o.reshape(B, T, H, dv)


def _chunk_attn_one_head(args):
    q, k, v, rel_bias = args
    B, T, d = q.shape
    nc = T // CHUNK
    nb = ATT_PAST_CHUNKS + 1
    qc = q.reshape(B, nc, CHUNK, d)
    pad = ((0, 0), (ATT_PAST_CHUNKS * CHUNK, 0), (0, 0))
    kp = jnp.pad(k, pad).reshape(B, nc + ATT_PAST_CHUNKS, CHUNK, d)
    vp = jnp.pad(v, pad).reshape(B, nc + ATT_PAST_CHUNKS, CHUNK, d)
    band_idx = jnp.arange(nc)[:, None] + jnp.arange(nb)[None, :]
    kb = kp[:, band_idx].reshape(B, nc, nb * CHUNK, d)
    vb = vp[:, band_idx].reshape(B, nc, nb * CHUNK, d)
    s = jnp.einsum('bnid,bnjd->bnij', qc, kb).astype(jnp.float32) * (d ** -0.5)
    qpos = jnp.arange(CHUNK) + ATT_PAST_CHUNKS * CHUNK
    kpos = jnp.arange(nb * CHUNK)
    rel = jnp.clip(qpos[:, None] - kpos[None, :], -MAX_REL, MAX_REL) + MAX_REL
    s = s + rel_bias.astype(jnp.float32)[rel][None, None]
    valid = jnp.repeat(band_idx >= ATT_PAST_CHUNKS, CHUNK, axis=1)
    s = jnp.where(valid[None, :, None, :], s, NEG_INF)
    p = jax.nn.softmax(s, axis=-1).astype(v.dtype)
    o = jnp.einsum('bnij,bnjd->bnid', p, vb)
    return o.reshape(B, T, d)


def chunk_rel_attention(q, k, v, rel_bias):
    o = lax.map(_chunk_attn_one_head,
                (q.transpose(2, 0, 1, 3), k.transpose(2, 0, 1, 3), v.transpose(2, 0, 1, 3), rel_bias))
    return o.transpose(1, 2, 0, 3)


def ab_mixer(h, w_in, w_out, rel_bias, pos):
    B, T, _ = h.shape
    z = h @ w_in
    splits = [RET_QK_W, 2 * RET_QK_W, 2 * RET_QK_W + RET_V_W, 2 * RET_QK_W + 2 * RET_V_W,
              2 * RET_QK_W + 2 * RET_V_W + ATT_W, 2 * RET_QK_W + 2 * RET_V_W + 2 * ATT_W]
    q_a, k_a, v_a, g_a, q_b, k_b, v_b = jnp.split(z, splits, axis=-1)
    q_a = rotary(q_a.reshape(B, T, RET_HEADS, RET_QK_DIM), pos)
    k_a = rotary(k_a.reshape(B, T, RET_HEADS, RET_QK_DIM), pos)
    r = retention(q_a, k_a, v_a.reshape(B, T, RET_HEADS, RET_V_DIM))
    mu = jnp.mean(r, axis=-1, keepdims=True)
    var = jnp.mean(jnp.square(r - mu), axis=-1, keepdims=True)
    r = ((r - mu) * lax.rsqrt(var + EPS)).reshape(B, T, RET_V_W)
    y_a = (jax.nn.silu(g_a.astype(jnp.float32)) * r).astype(h.dtype)
    y_b = chunk_rel_attention(q_b.reshape(B, T, ATT_HEADS, ATT_HEAD_DIM),
                              k_b.reshape(B, T, ATT_HEADS, ATT_HEAD_DIM),
                              v_b.reshape(B, T, ATT_HEADS, ATT_HEAD_DIM),
                              rel_bias).reshape(B, T, ATT_W)
    return jnp.concatenate([y_a, y_b.astype(h.dtype)], axis=-1) @ w_out


def sgu_mixer(h, w_in, ln_g, ln_b, w_s, b_s, w_out):
    B, T, _ = h.shape
    z = jax.nn.gelu(h @ w_in)
    u, v = jnp.split(z, 2, axis=-1)
    v = layer_norm(v, ln_g, ln_b)
    nb = T // SGU_BLOCK
    vg = v.reshape(B, nb, SGU_BLOCK, SGU_GROUPS, SGU_WIDTH // SGU_GROUPS)
    i = jnp.arange(SGU_BLOCK)
    mask = (i[None, :] // CHUNK) <= (i[:, None] // CHUNK)
    w = jnp.where(mask[None], w_s, jnp.zeros_like(w_s))
    mixed = jnp.einsum('gij,bnjgc->bnigc', w, vg) + b_s.T[None, None, :, :, None]
    y = u * mixed.reshape(B, T, SGU_WIDTH)
    return y @ w_out


def conv_ffn(h, w_up, conv_w, conv_b, w_down):
    z = h @ w_up
    c = z.shape[-1]
    z = lax.conv_general_dilated(z, conv_w[:, None, :].astype(z.dtype), window_strides=(1,),
                                 padding=[(CONV_WIDTH - 1, 0)],
                                 dimension_numbers=('NWC', 'WIO', 'NWC'),
                                 feature_group_count=c) + conv_b
    gate, up = jnp.split(z, 2, axis=-1)
    return (jax.nn.gelu(gate) * up) @ w_down


def setup_inputs(seed: int = 0) -> dict:
    key = jax.random.key(seed)
    ks = jax.random.split(key, 20)
    n = jax.random.normal
    f32 = jnp.float32
    nr = 2 * MAX_REL + 1
    return {
        "x": n(ks[0], (BATCH, SEQ, D_MODEL), f32),
        "attn_norm_g": 1.0 + 0.02 * n(ks[1], (DEPTH, D_MODEL), f32),
        "ffn_norm_g": 1.0 + 0.02 * n(ks[2], (DEPTH, D_MODEL), f32),
        "ab_w_in": n(ks[3], (N_EVEN, D_MODEL, AB_IN_W), f32) * D_MODEL ** -0.5,
        "ab_w_out": n(ks[4], (N_EVEN, AB_OUT_W, D_MODEL), f32) * AB_OUT_W ** -0.5,
        "ab_rel_bias": 0.1 * n(ks[5], (N_EVEN, ATT_HEADS, nr), f32),
        "c_w_in": n(ks[6], (N_ODD, D_MODEL, 2 * SGU_WIDTH), f32) * D_MODEL ** -0.5,
        "c_ln_g": 1.0 + 0.02 * n(ks[7], (N_ODD, SGU_WIDTH), f32),
        "c_ln_b": 0.02 * n(ks[8], (N_ODD, SGU_WIDTH), f32),
        "c_w_s": n(ks[9], (N_ODD, SGU_GROUPS, SGU_BLOCK, SGU_BLOCK), f32) * SGU_BLOCK ** -0.5,
        "c_b_s": 1.0 + 0.02 * n(ks[10], (N_ODD, SGU_GROUPS, SGU_BLOCK), f32),
        "c_w_out": n(ks[11], (N_ODD, SGU_WIDTH, D_MODEL), f32) * SGU_WIDTH ** -0.5,
        "ffn_w_up": n(ks[12], (DEPTH, D_MODEL, 2 * FFN_HIDDEN), f32) * D_MODEL ** -0.5,
        "ffn_conv_w": n(ks[13], (DEPTH, CONV_WIDTH, 2 * FFN_HIDDEN), f32) * CONV_WIDTH ** -0.5,
        "ffn_conv_b": 0.02 * n(ks[14], (DEPTH, 2 * FFN_HIDDEN), f32),
        "ffn_w_down": n(ks[15], (DEPTH, FFN_HIDDEN, D_MODEL), f32) * FFN_HIDDEN ** -0.5,
        "final_norm_g": 1.0 + 0.02 * n(ks[16], (D_MODEL,), f32),
    }


def reference(x, attn_norm_g, ffn_norm_g, ab_w_in, ab_w_out, ab_rel_bias, c_w_in, c_ln_g,
              c_ln_b, c_w_s, c_b_s, c_w_out, ffn_w_up, ffn_conv_w, ffn_conv_b, ffn_w_down,
              final_norm_g):
    h = x
    pos = jnp.arange(x.shape[1])
    for layer in range(DEPTH):
        hn = rms_norm(h, attn_norm_g[layer])
        i = layer // 2
        if layer % 2 == 0:
            h = h + ab_mixer(hn, ab_w_in[i], ab_w_out[i], ab_rel_bias[i], pos).astype(h.dtype)
        else:
            h = h + sgu_mixer(hn, c_w_in[i], c_ln_g[i], c_ln_b[i], c_w_s[i], c_b_s[i],
                              c_w_out[i]).astype(h.dtype)
        h = h + conv_ffn(rms_norm(h, ffn_norm_g[layer]), ffn_w_up[layer], ffn_conv_w[layer],
                         ffn_conv_b[layer], ffn_w_down[layer]).astype(h.dtype)
    return rms_norm(h, final_norm_g)
```

```python
import functools
import math

import numpy as np
import jax
import jax.numpy as jnp
from jax import lax
from jax.experimental import pallas as pl
from jax.experimental.pallas import tpu as pltpu

F32 = jnp.float32
BF16 = jnp.bfloat16

D_MODEL = 1024
CHUNK = 64
EPS = 1e-6
NEG_INF = -1e30

RET_HEADS = 4
RET_QK_DIM = 128
RET_V_DIM = 256
ATT_HEADS = 8
ATT_HEAD_DIM = 64
ATT_PAST_CHUNKS = 8
MAX_REL = 128
SGU_BLOCK = 128
SGU_GROUPS = 8
SGU_WIDTH = 2048
FFN_HIDDEN = 2816
CONV_WIDTH = 3

RET_QK_W = RET_HEADS * RET_QK_DIM
RET_V_W = RET_HEADS * RET_V_DIM
ATT_W = ATT_HEADS * ATT_HEAD_DIM
AB_IN_W = 2 * RET_QK_W + 2 * RET_V_W + 3 * ATT_W
AB_OUT_W = RET_V_W + ATT_W

VMEM_LIMIT_BYTES = 56 * 1024 * 1024
BF16_SUBLANES = 16

TM_PROJ = 512
TR_RET = 256
TA_ATT = 512
QB_ATT = 256
FFN_CB = 256
IN_CB = 512


def _const_spec(shape):
    zeros = (0,) * len(shape)
    return pl.BlockSpec(shape, lambda *_: zeros, pipeline_mode=pl.Buffered(1))


def _params(n_axes):
    return pltpu.CompilerParams(
        dimension_semantics=("arbitrary",) * n_axes,
        vmem_limit_bytes=VMEM_LIMIT_BYTES)


def _rms_norm(x, g):
    return x * lax.rsqrt(jnp.mean(x * x, axis=-1, keepdims=True) + EPS) * g


def _gelu_tanh(x):
    c = math.sqrt(2.0 / math.pi)
    return 0.5 * x * (1.0 + jnp.tanh(c * (x + 0.044715 * (x * x * x))))


def _dot(a, b):
    return jnp.dot(a, b, preferred_element_type=F32)


def _dot_nt(a, b):
    return lax.dot_general(a, b, (((1,), (1,)), ((), ())), preferred_element_type=F32)


def _dot_tn(a, b):
    return lax.dot_general(a, b, (((0,), (0,)), ((), ())), preferred_element_type=F32)


def _inproj_kernel(x_ref, g_ref, w_ref, cos_ref, sin_ref, o_ref):
    hn = _rms_norm(x_ref[0], g_ref[...]).astype(BF16)
    cos = cos_ref[...]
    sin = sin_ref[...]
    n_blocks = AB_IN_W // IN_CB
    att_q_block = (2 * RET_QK_W + 2 * RET_V_W) // IN_CB
    for j in range(n_blocks):
        z = _dot(hn, w_ref[:, j * IN_CB:(j + 1) * IN_CB])
        if j < 2 * RET_QK_W // IN_CB:
            parts = []
            for h in range(IN_CB // RET_QK_DIM):
                xh = z[:, h * RET_QK_DIM:(h + 1) * RET_QK_DIM]
                parts.append(xh * cos + pltpu.roll(xh, RET_QK_DIM // 2, axis=1) * sin)
            z = jnp.concatenate(parts, axis=1)
        elif j == att_q_block:
            z = z * (ATT_HEAD_DIM ** -0.5)
        o_ref[0, :, j * IN_CB:(j + 1) * IN_CB] = z.astype(BF16)


def _inproj(x, g, w_in, cos_t, sin_t):
    B, T, D = x.shape
    tm = TM_PROJ
    return pl.pallas_call(
        _inproj_kernel,
        grid=(B, T // tm),
        in_specs=[
            pl.BlockSpec((1, tm, D), lambda b, t: (b, t, 0)),
            _const_spec((1, D)),
            _const_spec((D, AB_IN_W)),
            pl.BlockSpec((tm, RET_QK_DIM), lambda b, t: (t, 0)),
            pl.BlockSpec((tm, RET_QK_DIM), lambda b, t: (t, 0)),
        ],
        out_specs=pl.BlockSpec((1, tm, AB_IN_W), lambda b, t: (b, t, 0)),
        out_shape=jax.ShapeDtypeStruct((B, T, AB_IN_W), BF16),
        compiler_params=_params(2),
        name="ab_inproj",
    )(x, g, w_in, cos_t, sin_t)


def _ret_tables(tr):
    h = np.arange(RET_HEADS, dtype=np.float64)
    log_g = np.log1p(-np.exp2(-5.0 - h))
    n = np.arange(tr)
    cn, cm = n[:, None] // CHUNK, n[None, :] // CHUNK
    diff = (n[:, None] - n[None, :]).astype(np.float64)
    expo = np.where(cn == cm, np.abs(diff), diff)
    dmat = np.where(cm <= cn, np.exp(log_g[:, None, None] * expo), 0.0)
    scale = RET_QK_DIM ** -0.5
    qdec = np.exp(log_g[:, None] * (n + 1.0)[None, :])
    kdec = np.exp(log_g[:, None] * (tr - 1.0 - n)[None, :]) * scale
    qdec = np.broadcast_to(qdec[:, :, None], (RET_HEADS, tr, RET_QK_DIM))
    kdec = np.broadcast_to(kdec[:, :, None], (RET_HEADS, tr, RET_QK_DIM))
    tile_dec = [float(v) for v in np.exp(log_g * tr)]
    return (jnp.asarray(dmat * scale, F32), jnp.asarray(qdec, F32), jnp.asarray(kdec, F32),
            tile_dec)


def _ret_kernel(tile_dec, q_ref, k_ref, v_ref, g_ref, d_ref, qd_ref, kd_ref, o_ref, state_ref):
    @pl.when(pl.program_id(1) == 0)
    def _():
        state_ref[...] = jnp.zeros_like(state_ref)

    for h in range(RET_HEADS):
        qk = slice(h * RET_QK_DIM, (h + 1) * RET_QK_DIM)
        vv = slice(h * RET_V_DIM, (h + 1) * RET_V_DIM)
        q = q_ref[0, :, qk]
        k = k_ref[0, :, qk]
        v = v_ref[0, :, vv]
        state = state_ref[h]
        s = _dot_nt(q, k) * d_ref[h]
        qd = (q.astype(F32) * qd_ref[h]).astype(BF16)
        kd = (k.astype(F32) * kd_ref[h]).astype(BF16)
        r = _dot(s.astype(BF16), v) + _dot(qd, state.astype(BF16))
        state_ref[h] = state * tile_dec[h] + _dot_tn(kd, v)
        mu = jnp.mean(r, axis=-1, keepdims=True)
        rc = r - mu
        var = jnp.mean(rc * rc, axis=-1, keepdims=True)
        rn = rc * lax.rsqrt(var + EPS)
        gate = g_ref[0, :, vv].astype(F32)
        o_ref[0, :, vv] = (0.5 * gate * (1.0 + jnp.tanh(0.5 * gate)) * rn).astype(BF16)


def _retention(z):
    B, T, _ = z.shape
    tr = TR_RET
    dmat, qdec, kdec, tile_dec = _ret_tables(tr)
    return pl.pallas_call(
        functools.partial(_ret_kernel, tile_dec),
        grid=(B, T // tr),
        in_specs=[
            pl.BlockSpec((1, tr, RET_QK_W), lambda b, t: (b, t, 0)),
            pl.BlockSpec((1, tr, RET_QK_W), lambda b, t: (b, t, 1)),
            pl.BlockSpec((1, tr, RET_V_W), lambda b, t: (b, t, 1)),
            pl.BlockSpec((1, tr, RET_V_W), lambda b, t: (b, t, 2)),
            _const_spec((RET_HEADS, tr, tr)),
            _const_spec((RET_HEADS, tr, RET_QK_DIM)),
            _const_spec((RET_HEADS, tr, RET_QK_DIM)),
        ],
        out_specs=pl.BlockSpec((1, tr, RET_V_W), lambda b, t: (b, t, 0)),
        out_shape=jax.ShapeDtypeStruct((B, T, RET_V_W), BF16),
        scratch_shapes=[pltpu.VMEM((RET_HEADS, RET_QK_DIM, RET_V_DIM), F32)],
        compiler_params=_params(2),
        name="retention",
    )(z, z, z, z, dmat, qdec, kdec)


def _att_bias_table(rel_bias):
    hist = ATT_PAST_CHUNKS * CHUNK
    i = np.arange(QB_ATT)[:, None]
    j = np.arange(QB_ATT + hist)[None, :]
    rel = np.clip(i + hist - j, -MAX_REL, MAX_REL) + MAX_REL
    ci, cj = i // CHUNK, j // CHUNK
    in_band = (cj >= ci) & (cj <= ci + ATT_PAST_CHUNKS)
    bias = rel_bias.astype(F32)[:, rel]
    return jnp.where(jnp.asarray(in_band)[None], bias, NEG_INF)


def _att_kernel(q_ref, k_ref, v_ref, b_ref, o_ref, kbuf, vbuf):
    t = pl.program_id(1)
    ta = TA_ATT
    hist = ATT_PAST_CHUNKS * CHUNK
    kw = QB_ATT + hist

    @pl.when(t == 0)
    def _():
        kbuf[0:hist, :] = jnp.zeros((hist, ATT_W), BF16)
        vbuf[0:hist, :] = jnp.zeros((hist, ATT_W), BF16)

    kbuf[hist:hist + ta, :] = k_ref[0]
    vbuf[hist:hist + ta, :] = v_ref[0]

    lane = lax.broadcasted_iota(jnp.int32, (1, 2 * ATT_HEAD_DIM), 1)
    low = lane < ATT_HEAD_DIM
    col = lax.broadcasted_iota(jnp.int32, (1, kw), 1)
    for qb in range(ta // QB_ATT):
        rows = slice(qb * QB_ATT, (qb + 1) * QB_ATT)
        krows = slice(qb * QB_ATT, qb * QB_ATT + kw)
        first_valid = hist - qb * QB_ATT - t * ta
        start_mask = jnp.where(col >= first_valid, 0.0, NEG_INF).astype(F32)
        for hp in range(ATT_HEADS // 2):
            lanes = slice(hp * 2 * ATT_HEAD_DIM, (hp + 1) * 2 * ATT_HEAD_DIM)
            qp = q_ref[0, rows, lanes]
            kp = kbuf[krows, lanes]
            vp = vbuf[krows, lanes]
            outs = []
            for e in range(2):
                keep = low if e == 0 else jnp.logical_not(low)
                qm = jnp.where(keep, qp, jnp.zeros_like(qp))
                s = _dot_nt(qm, kp) + b_ref[2 * hp + e] + start_mask
                m = jnp.max(s, axis=-1, keepdims=True)
                p = jnp.exp(s - m)
                l = jnp.sum(p, axis=-1, keepdims=True)
                outs.append(_dot(p.astype(BF16), vp) / l)
            o_ref[0, rows, lanes] = jnp.where(low, outs[0], outs[1]).astype(BF16)

    kbuf[0:hist, :] = kbuf[ta:ta + hist, :]
    vbuf[0:hist, :] = vbuf[ta:ta + hist, :]


def _attention(z, bias):
    B, T, _ = z.shape
    ta = TA_ATT
    hist = ATT_PAST_CHUNKS * CHUNK
    assert ta == hist
    q_blk = (2 * RET_QK_W + 2 * RET_V_W) // ATT_W
    return pl.pallas_call(
        _att_kernel,
        grid=(B, T // ta),
        in_specs=[
            pl.BlockSpec((1, ta, ATT_W), lambda b, t: (b, t, q_blk)),
            pl.BlockSpec((1, ta, ATT_W), lambda b, t: (b, t, q_blk + 1)),
            pl.BlockSpec((1, ta, ATT_W), lambda b, t: (b, t, q_blk + 2)),
            _const_spec((ATT_HEADS, QB_ATT, QB_ATT + hist)),
        ],
        out_specs=pl.BlockSpec((1, ta, ATT_W), lambda b, t: (b, t, 0)),
        out_shape=jax.ShapeDtypeStruct((B, T, ATT_W), BF16),
        scratch_shapes=[pltpu.VMEM((hist + ta, ATT_W), BF16),
                        pltpu.VMEM((hist + ta, ATT_W), BF16)],
        compiler_params=_params(2),
        name="chunk_attention",
    )(z, z, z, bias)


def _outproj_kernel(x_ref, ya_ref, yb_ref, w_ref, o_ref):
    acc = _dot(ya_ref[0], w_ref[0:RET_V_W, :]) + _dot(yb_ref[0], w_ref[RET_V_W:AB_OUT_W, :])
    o_ref[0] = x_ref[0] + acc


def _outproj(x, ya, yb, w_out):
    B, T, D = x.shape
    tm = TM_PROJ
    return pl.pallas_call(
        _outproj_kernel,
        grid=(B, T // tm),
        in_specs=[
            pl.BlockSpec((1, tm, D), lambda b, t: (b, t, 0)),
            pl.BlockSpec((1, tm, RET_V_W), lambda b, t: (b, t, 0)),
            pl.BlockSpec((1, tm, ATT_W), lambda b, t: (b, t, 0)),
            _const_spec((AB_OUT_W, D)),
        ],
        out_specs=pl.BlockSpec((1, tm, D), lambda b, t: (b, t, 0)),
        out_shape=jax.ShapeDtypeStruct((B, T, D), F32),
        compiler_params=_params(2),
        name="ab_outproj",
    )(x, ya, yb, w_out)


def _ffn_kernel(final_norm, x_ref, g_ref, wup_ref, cw_ref, cb_ref, wdn_ref, fg_ref, o_ref,
                hn_ext, zs_ref, act_ref):
    tm = x_ref.shape[1]
    pad = BF16_SUBLANES
    x = x_ref[0]
    hn = _rms_norm(x, g_ref[...]).astype(BF16)

    @pl.when(pl.program_id(1) == 0)
    def _():
        hn_ext[0:pad, :] = jnp.zeros((pad, D_MODEL), BF16)

    hn_ext[pad:pad + tm, :] = hn
    lhs = hn_ext[...]
    cbw = 2 * FFN_CB
    for j in range(FFN_HIDDEN // FFN_CB):
        cols = slice(j * cbw, (j + 1) * cbw)
        zs_ref[...] = _dot(lhs, wup_ref[:, cols])
        c = cb_ref[:, cols]
        for tap in range(CONV_WIDTH):
            off = pad - (CONV_WIDTH - 1) + tap
            c = c + zs_ref[off:off + tm, :] * cw_ref[tap:tap + 1, cols]
        act_ref[:, j * FFN_CB:(j + 1) * FFN_CB] = (
            _gelu_tanh(c[:, :FFN_CB]) * c[:, FFN_CB:]).astype(BF16)
    hn_ext[0:pad, :] = hn_ext[tm:tm + pad, :]
    y = x + _dot(act_ref[...], wdn_ref[...])
    if final_norm:
        y = _rms_norm(y, fg_ref[...])
    o_ref[0] = y


def _ffn_prepare(w_up, conv_w, conv_b):
    nb = FFN_HIDDEN // FFN_CB

    def perm(a):
        lead = a.shape[:-1]
        a = a.reshape(lead + (2, nb, FFN_CB))
        return jnp.swapaxes(a, -3, -2).reshape(lead + (2 * FFN_HIDDEN,))

    return perm(w_up).astype(BF16), perm(conv_w), perm(conv_b[None, :])


def _ffn(x, g, w_up, conv_w, conv_b, w_down, final_g=None):
    B, T, D = x.shape
    tm = TM_PROJ
    final_norm = final_g is not None
    if final_g is None:
        final_g = jnp.ones((D,), F32)
    w_up_p, cw_p, cb_p = _ffn_prepare(w_up, conv_w, conv_b)
    return pl.pallas_call(
        functools.partial(_ffn_kernel, final_norm),
        grid=(B, T // tm),
        in_specs=[
            pl.BlockSpec((1, tm, D), lambda b, t: (b, t, 0)),
            _const_spec((1, D)),
            _const_spec((D, 2 * FFN_HIDDEN)),
            _const_spec((CONV_WIDTH, 2 * FFN_HIDDEN)),
            _const_spec((1, 2 * FFN_HIDDEN)),
            _const_spec((FFN_HIDDEN, D)),
            _const_spec((1, D)),
        ],
        out_specs=pl.BlockSpec((1, tm, D), lambda b, t: (b, t, 0)),
        out_shape=jax.ShapeDtypeStruct((B, T, D), F32),
        scratch_shapes=[
            pltpu.VMEM((tm + BF16_SUBLANES, D), BF16),
            pltpu.VMEM((tm + BF16_SUBLANES, 2 * FFN_CB), F32),
            pltpu.VMEM((tm, FFN_HIDDEN), BF16),
        ],
        compiler_params=_params(2),
        name="conv_ffn_final" if final_norm else "conv_ffn",
    )(x, g[None, :], w_up_p, cw_p, cb_p, w_down.astype(BF16), final_g[None, :])


def _sgu_kernel(x_ref, g_ref, win_ref, lng_ref, lnb_ref, ws_ref, bs_ref, wout_ref, o_ref,
                v_ref, y_ref):
    tm = x_ref.shape[1]
    gw = SGU_WIDTH // SGU_GROUPS
    pair = 2 * SGU_BLOCK
    x = x_ref[0]
    hn = _rms_norm(x, g_ref[...]).astype(BF16)

    vsum = jnp.zeros((tm, 1), F32)
    for j in range(SGU_WIDTH // IN_CB):
        cols = slice(j * IN_CB, (j + 1) * IN_CB)
        zv = _gelu_tanh(_dot(hn, win_ref[:, SGU_WIDTH + j * IN_CB:SGU_WIDTH + (j + 1) * IN_CB]))
        v_ref[:, cols] = zv
        vsum = vsum + jnp.sum(zv, axis=-1, keepdims=True)
    mu = vsum * (1.0 / SGU_WIDTH)
    vsq = jnp.zeros((tm, 1), F32)
    for j in range(SGU_WIDTH // IN_CB):
        cols = slice(j * IN_CB, (j + 1) * IN_CB)
        vc = v_ref[:, cols] - mu
        vsq = vsq + jnp.sum(vc * vc, axis=-1, keepdims=True)
    rstd = lax.rsqrt(vsq * (1.0 / SGU_WIDTH) + EPS)

    ri = lax.broadcasted_iota(jnp.int32, (pair, pair), 0)
    ci = lax.broadcasted_iota(jnp.int32, (pair, pair), 1)
    allowed = (ci // CHUNK) <= (ri // CHUNK)
    same = (ci // SGU_BLOCK) == (ri // SGU_BLOCK)
    keep = jnp.logical_and(allowed, same)
    for gidx in range(SGU_GROUPS):
        cols = slice(gidx * gw, (gidx + 1) * gw)
        w = jnp.where(keep, ws_ref[gidx], jnp.zeros((pair, pair), BF16))
        vn = ((v_ref[:, cols] - mu) * rstd * lng_ref[:, cols] + lnb_ref[:, cols]).astype(BF16)
        u = _gelu_tanh(_dot(hn, win_ref[:, cols]))
        for rp in range(tm // pair):
            rows = slice(rp * pair, (rp + 1) * pair)
            mixed = _dot(w, vn[rows]) + bs_ref[:, cols]
            y_ref[rows, cols] = (u[rows] * mixed).astype(BF16)
    o_ref[0] = x + _dot(y_ref[...], wout_ref[...])


def _sgu(x, g, w_in, ln_g, ln_b, w_s, b_s, w_out):
    B, T, D = x.shape
    tm = TM_PROJ
    gw = SGU_WIDTH // SGU_GROUPS
    pair = 2 * SGU_BLOCK
    zero = jnp.zeros_like(w_s)
    w_bd = jnp.concatenate([jnp.concatenate([w_s, zero], axis=2),
                            jnp.concatenate([zero, w_s], axis=2)], axis=1).astype(BF16)
    b_tab = jnp.repeat(b_s.T, gw, axis=1)
    b_tab = jnp.concatenate([b_tab, b_tab], axis=0)
    return pl.pallas_call(
        _sgu_kernel,
        grid=(B, T // tm),
        in_specs=[
            pl.BlockSpec((1, tm, D), lambda b, t: (b, t, 0)),
            _const_spec((1, D)),
            _const_spec((D, 2 * SGU_WIDTH)),
            _const_spec((1, SGU_WIDTH)),
            _const_spec((1, SGU_WIDTH)),
            _const_spec((SGU_GROUPS, pair, pair)),
            _const_spec((pair, SGU_WIDTH)),
            _const_spec((SGU_WIDTH, D)),
        ],
        out_specs=pl.BlockSpec((1, tm, D), lambda b, t: (b, t, 0)),
        out_shape=jax.ShapeDtypeStruct((B, T, D), F32),
        scratch_shapes=[
            pltpu.VMEM((tm, SGU_WIDTH), F32),
            pltpu.VMEM((tm, SGU_WIDTH), BF16),
        ],
        compiler_params=_params(2),
        name="sgu_mixer",
    )(x, g[None, :], w_in.astype(BF16), ln_g[None, :], ln_b[None, :], w_bd, b_tab,
      w_out.astype(BF16))


def _rotary_tables(T):
    half = RET_QK_DIM // 2
    inv = 1.0 / (10000.0 ** jnp.linspace(0.0, 1.0, half, dtype=F32))
    ang = jnp.arange(T).astype(F32)[:, None] * inv[None, :]
    cos, sin = jnp.cos(ang), jnp.sin(ang)
    return jnp.concatenate([cos, cos], axis=1), jnp.concatenate([-sin, sin], axis=1)


def kernel(x, attn_norm_g, ffn_norm_g, ab_w_in, ab_w_out, ab_rel_bias, c_w_in, c_ln_g, c_ln_b,
           c_w_s, c_b_s, c_w_out, ffn_w_up, ffn_conv_w, ffn_conv_b, ffn_w_down, final_norm_g):
    T = x.shape[1]
    cos_t, sin_t = _rotary_tables(T)
    z = _inproj(x, attn_norm_g[0][None, :], ab_w_in[0].astype(BF16), cos_t, sin_t)
    ya = _retention(z)
    yb = _attention(z, _att_bias_table(ab_rel_bias[0]))
    h = _outproj(x, ya, yb, ab_w_out[0].astype(BF16))
    h = _ffn(h, ffn_norm_g[0], ffn_w_up[0], ffn_conv_w[0], ffn_conv_b[0], ffn_w_down[0])
    h = _sgu(h, attn_norm_g[1], c_w_in[0], c_ln_g[0], c_ln_b[0], c_w_s[0], c_b_s[0], c_w_out[0])
    h = _ffn(h, ffn_norm_g[1], ffn_w_up[1], ffn_conv_w[1], ffn_conv_b[1], ffn_w_down[1],
             final_g=final_norm_g)
    return h
```

```python
import functools
import math

import numpy as np
import jax
import jax.numpy as jnp
from jax import lax
from jax.experimental import pallas as pl
from jax.experimental.pallas import tpu as pltpu

F32 = jnp.float32
BF16 = jnp.bfloat16

D_MODEL = 1024
CHUNK = 64
EPS = 1e-6
NEG_INF = -1e30

RET_HEADS = 4
RET_QK_DIM = 128
RET_V_DIM = 256
ATT_HEADS = 8
ATT_HEAD_DIM = 64
ATT_PAST_CHUNKS = 8
MAX_REL = 128
SGU_BLOCK = 128
SGU_GROUPS = 8
SGU_WIDTH = 2048
FFN_HIDDEN = 2816
CONV_WIDTH = 3

RET_QK_W = RET_HEADS * RET_QK_DIM
RET_V_W = RET_HEADS * RET_V_DIM
ATT_W = ATT_HEADS * ATT_HEAD_DIM
AB_IN_W = 2 * RET_QK_W + 2 * RET_V_W + 3 * ATT_W
AB_OUT_W = RET_V_W + ATT_W

VMEM_LIMIT_BYTES = 56 * 1024 * 1024
BF16_SUBLANES = 16

TM_PROJ = 512
TR_RET = 256
TA_ATT = 512
QB_ATT = 256
FFN_CB = 256
IN_CB = 512


def _const_spec(shape):
    zeros = (0,) * len(shape)
    return pl.BlockSpec(shape, lambda *_: zeros, pipeline_mode=pl.Buffered(1))


def _params(n_axes):
    return pltpu.CompilerParams(
        dimension_semantics=("arbitrary",) * n_axes,
        vmem_limit_bytes=VMEM_LIMIT_BYTES)


def _rms_norm(x, g):
    return x * lax.rsqrt(jnp.mean(x * x, axis=-1, keepdims=True) + EPS) * g


def _gelu_tanh(x):
    c = math.sqrt(2.0 / math.pi)
    return 0.5 * x * (1.0 + jnp.tanh(c * (x + 0.044715 * (x * x * x))))


def _dot(a, b):
    return jnp.dot(a, b, preferred_element_type=F32)


def _dot_nt(a, b):
    return lax.dot_general(a, b, (((1,), (1,)), ((), ())), preferred_element_type=F32)


def _dot_tn(a, b):
    return lax.dot_general(a, b, (((0,), (0,)), ((), ())), preferred_element_type=F32)


def _inproj_kernel(x_ref, g_ref, w_ref, cos_ref, sin_ref, o_ref):
    hn = _rms_norm(x_ref[0], g_ref[...]).astype(BF16)
    cos = cos_ref[...]
    sin = sin_ref[...]
    n_blocks = AB_IN_W // IN_CB
    att_q_block = (2 * RET_QK_W + 2 * RET_V_W) // IN_CB
    for j in range(n_blocks):
        z = _dot(hn, w_ref[:, j * IN_CB:(j + 1) * IN_CB])
        if j < 2 * RET_QK_W // IN_CB:
            parts = []
            for h in range(IN_CB // RET_QK_DIM):
                xh = z[:, h * RET_QK_DIM:(h + 1) * RET_QK_DIM]
                parts.append(xh * cos + pltpu.roll(xh, RET_QK_DIM // 2, axis=1) * sin)
            z = jnp.concatenate(parts, axis=1)
        elif j == att_q_block:
            z = z * (ATT_HEAD_DIM ** -0.5)
        o_ref[0, :, j * IN_CB:(j + 1) * IN_CB] = z.astype(BF16)


def _inproj(x, g, w_in, cos_t, sin_t):
    B, T, D = x.shape
    tm = TM_PROJ
    return pl.pallas_call(
        _inproj_kernel,
        grid=(B, T // tm),
        in_specs=[
            pl.BlockSpec((1, tm, D), lambda b, t: (b, t, 0)),
            _const_spec((1, D)),
            _const_spec((D, AB_IN_W)),
            pl.BlockSpec((tm, RET_QK_DIM), lambda b, t: (t, 0)),
            pl.BlockSpec((tm, RET_QK_DIM), lambda b, t: (t, 0)),
        ],
        out_specs=pl.BlockSpec((1, tm, AB_IN_W), lambda b, t: (b, t, 0)),
        out_shape=jax.ShapeDtypeStruct((B, T, AB_IN_W), BF16),
        compiler_params=_params(2),
        name="ab_inproj",
    )(x, g, w_in, cos_t, sin_t)


def _ret_tables(tr):
    h = np.arange(RET_HEADS, dtype=np.float64)
    log_g = np.log1p(-np.exp2(-5.0 - h))
    n = np.arange(tr)
    cn, cm = n[:, None] // CHUNK, n[None, :] // CHUNK
    diff = (n[:, None] - n[None, :]).astype(np.float64)
    expo = np.where(cn == cm, np.abs(diff), diff)
    dmat = np.where(cm <= cn, np.exp(log_g[:, None, None] * expo), 0.0)
    scale = RET_QK_DIM ** -0.5
    qdec = np.exp(log_g[:, None] * (n + 1.0)[None, :])
    kdec = np.exp(log_g[:, None] * (tr - 1.0 - n)[None, :]) * scale
    qdec = np.broadcast_to(qdec[:, :, None], (RET_HEADS, tr, RET_QK_DIM))
    kdec = np.broadcast_to(kdec[:, :, None], (RET_HEADS, tr, RET_QK_DIM))
    tile_dec = [float(v) for v in np.exp(log_g * tr)]
    return (jnp.asarray(dmat * scale, F32), jnp.asarray(qdec, F32), jnp.asarray(kdec, F32),
            tile_dec)


def _ret_kernel(tile_dec, q_ref, k_ref, v_ref, g_ref, d_ref, qd_ref, kd_ref, o_ref, state_ref):
    @pl.when(pl.program_id(1) == 0)
    def _():
        state_ref[...] = jnp.zeros_like(state_ref)

    for h in range(RET_HEADS):
        qk = slice(h * RET_QK_DIM, (h + 1) * RET_QK_DIM)
        vv = slice(h * RET_V_DIM, (h + 1) * RET_V_DIM)
        q = q_ref[0, :, qk]
        k = k_ref[0, :, qk]
        v = v_ref[0, :, vv]
        state = state_ref[h]
        s = _dot_nt(q, k) * d_ref[h]
        qd = (q.astype(F32) * qd_ref[h]).astype(BF16)
        kd = (k.astype(F32) * kd_ref[h]).astype(BF16)
        r = _dot(s.astype(BF16), v) + _dot(qd, state.astype(BF16))
        state_ref[h] = state * tile_dec[h] + _dot_tn(kd, v)
        mu = jnp.mean(r, axis=-1, keepdims=True)
        rc = r - mu
        var = jnp.mean(rc * rc, axis=-1, keepdims=True)
        rn = rc * lax.rsqrt(var + EPS)
        gate = g_ref[0, :, vv].astype(F32)
        o_ref[0, :, vv] = (0.5 * gate * (1.0 + jnp.tanh(0.5 * gate)) * rn).astype(BF16)


def _retention(z):
    B, T, _ = z.shape
    tr = TR_RET
    dmat, qdec, kdec, tile_dec = _ret_tables(tr)
    return pl.pallas_call(
        functools.partial(_ret_kernel, tile_dec),
        grid=(B, T // tr),
        in_specs=[
            pl.BlockSpec((1, tr, RET_QK_W), lambda b, t: (b, t, 0)),
            pl.BlockSpec((1, tr, RET_QK_W), lambda b, t: (b, t, 1)),
            pl.BlockSpec((1, tr, RET_V_W), lambda b, t: (b, t, 1)),
            pl.BlockSpec((1, tr, RET_V_W), lambda b, t: (b, t, 2)),
            _const_spec((RET_HEADS, tr, tr)),
            _const_spec((RET_HEADS, tr, RET_QK_DIM)),
            _const_spec((RET_HEADS, tr, RET_QK_DIM)),
        ],
        out_specs=pl.BlockSpec((1, tr, RET_V_W), lambda b, t: (b, t, 0)),
        out_shape=jax.ShapeDtypeStruct((B, T, RET_V_W), BF16),
        scratch_shapes=[pltpu.VMEM((RET_HEADS, RET_QK_DIM, RET_V_DIM), F32)],
        compiler_params=_params(2),
        name="retention",
    )(z, z, z, z, dmat, qdec, kdec)


def _att_bias_table(rel_bias):
    hist = ATT_PAST_CHUNKS * CHUNK
    kw = QB_ATT + hist
    nr = 2 * MAX_REL + 1
    period = 2 * hist
    assert period >= kw + QB_ATT - 1
    n_hi = kw - MAX_REL
    n_lo = period - n_hi - (nr - 2)
    g = jnp.concatenate([
        jnp.broadcast_to(rel_bias[:, nr - 1:nr], (ATT_HEADS, n_hi)),
        rel_bias[:, nr - 2:0:-1],
        jnp.broadcast_to(rel_bias[:, 0:1], (ATT_HEADS, n_lo))], axis=1).astype(F32)
    a = jnp.tile(g, (1, QB_ATT + 1))[:, :QB_ATT * (period + 1)]
    bias = a.reshape(ATT_HEADS, QB_ATT, period + 1)[:, ::-1, :kw]
    ci = np.arange(QB_ATT)[:, None] // CHUNK
    cj = np.arange(kw)[None, :] // CHUNK
    in_band = (cj >= ci) & (cj <= ci + ATT_PAST_CHUNKS)
    return jnp.where(jnp.asarray(in_band)[None], bias, NEG_INF)


def _att_kernel(q_ref, k_ref, v_ref, b_ref, o_ref, kbuf, vbuf):
    t = pl.program_id(1)
    ta = TA_ATT
    hist = ATT_PAST_CHUNKS * CHUNK
    kw = QB_ATT + hist

    @pl.when(t == 0)
    def _():
        kbuf[0:hist, :] = jnp.zeros((hist, ATT_W), BF16)
        vbuf[0:hist, :] = jnp.zeros((hist, ATT_W), BF16)

    kbuf[hist:hist + ta, :] = k_ref[0]
    vbuf[hist:hist + ta, :] = v_ref[0]

    lane = lax.broadcasted_iota(jnp.int32, (1, 2 * ATT_HEAD_DIM), 1)
    low = lane < ATT_HEAD_DIM
    col = lax.broadcasted_iota(jnp.int32, (1, kw), 1)
    for qb in range(ta // QB_ATT):
        rows = slice(qb * QB_ATT, (qb + 1) * QB_ATT)
        krows = slice(qb * QB_ATT, qb * QB_ATT + kw)
        first_valid = hist - qb * QB_ATT - t * ta
        start_mask = jnp.where(col >= first_valid, 0.0, NEG_INF).astype(F32)
        for hp in range(ATT_HEADS // 2):
            lanes = slice(hp * 2 * ATT_HEAD_DIM, (hp + 1) * 2 * ATT_HEAD_DIM)
            qp = q_ref[0, rows, lanes]
            kp = kbuf[krows, lanes]
            vp = vbuf[krows, lanes]
            outs = []
            for e in range(2):
                keep = low if e == 0 else jnp.logical_not(low)
                qm = jnp.where(keep, qp, jnp.zeros_like(qp))
                s = _dot_nt(qm, kp) + b_ref[2 * hp + e] + start_mask
                m = jnp.max(s, axis=-1, keepdims=True)
                p = jnp.exp(s - m)
                l = jnp.sum(p, axis=-1, keepdims=True)
                outs.append(_dot(p.astype(BF16), vp) / l)
            o_ref[0, rows, lanes] = jnp.where(low, outs[0], outs[1]).astype(BF16)

    kbuf[0:hist, :] = kbuf[ta:ta + hist, :]
    vbuf[0:hist, :] = vbuf[ta:ta + hist, :]


def _attention(z, bias):
    B, T, _ = z.shape
    ta = TA_ATT
    hist = ATT_PAST_CHUNKS * CHUNK
    assert ta == hist
    q_blk = (2 * RET_QK_W + 2 * RET_V_W) // ATT_W
    return pl.pallas_call(
        _att_kernel,
        grid=(B, T // ta),
        in_specs=[
            pl.BlockSpec((1, ta, ATT_W), lambda b, t: (b, t, q_blk)),
            pl.BlockSpec((1, ta, ATT_W), lambda b, t: (b, t, q_blk + 1)),
            pl.BlockSpec((1, ta, ATT_W), lambda b, t: (b, t, q_blk + 2)),
            _const_spec((ATT_HEADS, QB_ATT, QB_ATT + hist)),
        ],
        out_specs=pl.BlockSpec((1, ta, ATT_W), lambda b, t: (b, t, 0)),
        out_shape=jax.ShapeDtypeStruct((B, T, ATT_W), BF16),
        scratch_shapes=[pltpu.VMEM((hist + ta, ATT_W), BF16),
                        pltpu.VMEM((hist + ta, ATT_W), BF16)],
        compiler_params=_params(2),
        name="chunk_attention",
    )(z, z, z, bias)


def _outproj_kernel(x_ref, ya_ref, yb_ref, w_ref, o_ref):
    acc = _dot(ya_ref[0], w_ref[0:RET_V_W, :]) + _dot(yb_ref[0], w_ref[RET_V_W:AB_OUT_W, :])
    o_ref[0] = x_ref[0] + acc


def _outproj(x, ya, yb, w_out):
    B, T, D = x.shape
    tm = TM_PROJ
    return pl.pallas_call(
        _outproj_kernel,
        grid=(B, T // tm),
        in_specs=[
            pl.BlockSpec((1, tm, D), lambda b, t: (b, t, 0)),
            pl.BlockSpec((1, tm, RET_V_W), lambda b, t: (b, t, 0)),
            pl.BlockSpec((1, tm, ATT_W), lambda b, t: (b, t, 0)),
            _const_spec((AB_OUT_W, D)),
        ],
        out_specs=pl.BlockSpec((1, tm, D), lambda b, t: (b, t, 0)),
        out_shape=jax.ShapeDtypeStruct((B, T, D), F32),
        compiler_params=_params(2),
        name="ab_outproj",
    )(x, ya, yb, w_out)


def _ffn_kernel(final_norm, x_ref, g_ref, wup_ref, cw_ref, cb_ref, wdn_ref, fg_ref,
                o_ref, hn_ext, zs_ref, act_ref):
    tm = x_ref.shape[1]
    pad = BF16_SUBLANES
    x = x_ref[0]
    hn = _rms_norm(x, g_ref[...]).astype(BF16)

    @pl.when(pl.program_id(1) == 0)
    def _():
        hn_ext[0:pad, :] = jnp.zeros((pad, D_MODEL), BF16)

    hn_ext[pad:pad + tm, :] = hn
    lhs = hn_ext[...]
    for j in range(FFN_HIDDEN // FFN_CB):
        halves = []
        for half in range(2):
            cols = slice(half * FFN_HIDDEN + j * FFN_CB, half * FFN_HIDDEN + (j + 1) * FFN_CB)
            zs_ref[half] = _dot(lhs, wup_ref[:, cols])
            c = cb_ref[:, cols]
            for tap in range(CONV_WIDTH):
                off = pad - (CONV_WIDTH - 1) + tap
                c = c + zs_ref[half, off:off + tm, :] * cw_ref[tap:tap + 1, cols]
            halves.append(c)
        act_ref[:, j * FFN_CB:(j + 1) * FFN_CB] = (
            _gelu_tanh(halves[0]) * halves[1]).astype(BF16)
    hn_ext[0:pad, :] = hn_ext[tm:tm + pad, :]
    y = x + _dot(act_ref[...], wdn_ref[...])
    if final_norm:
        y = _rms_norm(y, fg_ref[...])
    o_ref[0] = y


def _ffn(x, g, w_up, conv_w, conv_b, w_down, final_g=None):
    B, T, D = x.shape
    tm = TM_PROJ
    final_norm = final_g is not None
    if final_g is None:
        final_g = jnp.ones((D,), F32)
    return pl.pallas_call(
        functools.partial(_ffn_kernel, final_norm),
        grid=(B, T // tm),
        in_specs=[
            pl.BlockSpec((1, tm, D), lambda b, t: (b, t, 0)),
            _const_spec((1, D)),
            _const_spec((D, 2 * FFN_HIDDEN)),
            _const_spec((CONV_WIDTH, 2 * FFN_HIDDEN)),
            _const_spec((1, 2 * FFN_HIDDEN)),
            _const_spec((FFN_HIDDEN, D)),
            _const_spec((1, D)),
        ],
        out_specs=pl.BlockSpec((1, tm, D), lambda b, t: (b, t, 0)),
        out_shape=jax.ShapeDtypeStruct((B, T, D), F32),
        scratch_shapes=[
            pltpu.VMEM((tm + BF16_SUBLANES, D), BF16),
            pltpu.VMEM((2, tm + BF16_SUBLANES, FFN_CB), F32),
            pltpu.VMEM((tm, FFN_HIDDEN), BF16),
        ],
        compiler_params=_params(2),
        name="conv_ffn_final" if final_norm else "conv_ffn",
    )(x, g[None, :], w_up.astype(BF16), conv_w, conv_b[None, :],
      w_down.astype(BF16), final_g[None, :])


def _sgu_kernel(x_ref, g_ref, win_ref, lng_ref, lnb_ref, ws_ref, bs_ref, wout_ref, o_ref,
                v_ref, y_ref):
    tm = x_ref.shape[1]
    gw = SGU_WIDTH // SGU_GROUPS
    pair = 2 * SGU_BLOCK
    x = x_ref[0]
    hn = _rms_norm(x, g_ref[...]).astype(BF16)

    vsum = jnp.zeros((tm, 1), F32)
    for j in range(SGU_WIDTH // IN_CB):
        cols = slice(j * IN_CB, (j + 1) * IN_CB)
        zv = _gelu_tanh(_dot(hn, win_ref[:, SGU_WIDTH + j * IN_CB:SGU_WIDTH + (j + 1) * IN_CB]))
        v_ref[:, cols] = zv
        vsum = vsum + jnp.sum(zv, axis=-1, keepdims=True)
    mu = vsum * (1.0 / SGU_WIDTH)
    vsq = jnp.zeros((tm, 1), F32)
    for j in range(SGU_WIDTH // IN_CB):
        cols = slice(j * IN_CB, (j + 1) * IN_CB)
        vc = v_ref[:, cols] - mu
        vsq = vsq + jnp.sum(vc * vc, axis=-1, keepdims=True)
    rstd = lax.rsqrt(vsq * (1.0 / SGU_WIDTH) + EPS)

    ri = lax.broadcasted_iota(jnp.int32, (pair, pair), 0)
    ci = lax.broadcasted_iota(jnp.int32, (pair, pair), 1)
    allowed = (ci // CHUNK) <= (ri // CHUNK)
    same = (ci // SGU_BLOCK) == (ri // SGU_BLOCK)
    keep = jnp.logical_and(allowed, same)
    for gidx in range(SGU_GROUPS):
        cols = slice(gidx * gw, (gidx + 1) * gw)
        w = jnp.where(keep, ws_ref[gidx], jnp.zeros((pair, pair), BF16))
        vn = ((v_ref[:, cols] - mu) * rstd * lng_ref[:, cols] + lnb_ref[:, cols]).astype(BF16)
        u = _gelu_tanh(_dot(hn, win_ref[:, cols]))
        for rp in range(tm // pair):
            rows = slice(rp * pair, (rp + 1) * pair)
            mixed = _dot(w, vn[rows]) + bs_ref[:, cols]
            y_ref[rows, cols] = (u[rows] * mixed).astype(BF16)
    o_ref[0] = x + _dot(y_ref[...], wout_ref[...])


def _sgu(x, g, w_in, ln_g, ln_b, w_s, b_s, w_out):
    B, T, D = x.shape
    tm = TM_PROJ
    gw = SGU_WIDTH // SGU_GROUPS
    pair = 2 * SGU_BLOCK
    zero = jnp.zeros_like(w_s)
    w_bd = jnp.concatenate([jnp.concatenate([w_s, zero], axis=2),
                            jnp.concatenate([zero, w_s], axis=2)], axis=1).astype(BF16)
    b_tab = jnp.repeat(b_s.T, gw, axis=1)
    b_tab = jnp.concatenate([b_tab, b_tab], axis=0)
    return pl.pallas_call(
        _sgu_kernel,
        grid=(B, T // tm),
        in_specs=[
            pl.BlockSpec((1, tm, D), lambda b, t: (b, t, 0)),
            _const_spec((1, D)),
            _const_spec((D, 2 * SGU_WIDTH)),
            _const_spec((1, SGU_WIDTH)),
            _const_spec((1, SGU_WIDTH)),
            _const_spec((SGU_GROUPS, pair, pair)),
            _const_spec((pair, SGU_WIDTH)),
            _const_spec((SGU_WIDTH, D)),
        ],
        out_specs=pl.BlockSpec((1, tm, D), lambda b, t: (b, t, 0)),
        out_shape=jax.ShapeDtypeStruct((B, T, D), F32),
        scratch_shapes=[
            pltpu.VMEM((tm, SGU_WIDTH), F32),
            pltpu.VMEM((tm, SGU_WIDTH), BF16),
        ],
        compiler_params=_params(2),
        name="sgu_mixer",
    )(x, g[None, :], w_in.astype(BF16), ln_g[None, :], ln_b[None, :], w_bd, b_tab,
      w_out.astype(BF16))


def _rotary_tables(T):
    half = RET_QK_DIM // 2
    inv = 1.0 / (10000.0 ** jnp.linspace(0.0, 1.0, half, dtype=F32))
    ang = jnp.arange(T).astype(F32)[:, None] * inv[None, :]
    cos, sin = jnp.cos(ang), jnp.sin(ang)
    return jnp.concatenate([cos, cos], axis=1), jnp.concatenate([-sin, sin], axis=1)


def kernel(x, attn_norm_g, ffn_norm_g, ab_w_in, ab_w_out, ab_rel_bias, c_w_in, c_ln_g, c_ln_b,
           c_w_s, c_b_s, c_w_out, ffn_w_up, ffn_conv_w, ffn_conv_b, ffn_w_down, final_norm_g):
    T = x.shape[1]
    cos_t, sin_t = _rotary_tables(T)
    z = _inproj(x, attn_norm_g[0][None, :], ab_w_in[0].astype(BF16), cos_t, sin_t)
    ya = _retention(z)
    yb = _attention(z, _att_bias_table(ab_rel_bias[0]))
    h = _outproj(x, ya, yb, ab_w_out[0].astype(BF16))
    h = _ffn(h, ffn_norm_g[0], ffn_w_up[0], ffn_conv_w[0], ffn_conv_b[0], ffn_w_down[0])
    h = _sgu(h, attn_norm_g[1], c_w_in[0], c_ln_g[0], c_ln_b[0], c_w_s[0], c_b_s[0], c_w_out[0])
    h = _ffn(h, ffn_norm_g[1], ffn_w_up[1], ffn_conv_w[1], ffn_conv_b[1], ffn_w_down[1],
             final_g=final_norm_g)
    return h
```

```python
import functools
import math

import numpy as np
import jax
import jax.numpy as jnp
from jax import lax
from jax.experimental import pallas as pl
from jax.experimental.pallas import tpu as pltpu

F32 = jnp.float32
BF16 = jnp.bfloat16

D_MODEL = 1024
CHUNK = 64
EPS = 1e-6
NEG_INF = -1e30
LOG2E = math.log2(math.e)

RET_HEADS = 4
RET_QK_DIM = 128
RET_V_DIM = 256
ATT_HEADS = 8
ATT_HEAD_DIM = 64
ATT_PAST_CHUNKS = 8
MAX_REL = 128
SGU_BLOCK = 128
SGU_GROUPS = 8
SGU_WIDTH = 2048
FFN_HIDDEN = 2816
CONV_WIDTH = 3

RET_QK_W = RET_HEADS * RET_QK_DIM
RET_V_W = RET_HEADS * RET_V_DIM
ATT_W = ATT_HEADS * ATT_HEAD_DIM
AB_IN_W = 2 * RET_QK_W + 2 * RET_V_W + 3 * ATT_W
AB_OUT_W = RET_V_W + ATT_W

VMEM_LIMIT_BYTES = 56 * 1024 * 1024
BF16_SUBLANES = 16
LANES = 128

TM_PROJ = 512
TM_FFN = 512
TR_RET = 256
TA_ATT = 512
QB_ATT = 256
FFN_CB = 256
IN_CB = 512
NORM_ROWS = 32
CONV_ROWS = 64


def _const_spec(shape):
    zeros = (0,) * len(shape)
    return pl.BlockSpec(shape, lambda *_: zeros, pipeline_mode=pl.Buffered(1))


def _params(n_axes, flags=None):
    return pltpu.CompilerParams(
        dimension_semantics=("arbitrary",) * n_axes,
        vmem_limit_bytes=VMEM_LIMIT_BYTES,
        flags=flags)


def _rms_norm(x, g):
    return x * lax.rsqrt(jnp.mean(x * x, axis=-1, keepdims=True) + EPS) * g


def _gelu_tanh(x):
    c = math.sqrt(2.0 / math.pi)
    return 0.5 * x * (1.0 + jnp.tanh(c * (x + 0.044715 * (x * x * x))))


def _dot(a, b):
    return jnp.dot(a, b, preferred_element_type=F32)


def _dot_nt(a, b):
    return lax.dot_general(a, b, (((1,), (1,)), ((), ())), preferred_element_type=F32)


def _dot_tn(a, b):
    return lax.dot_general(a, b, (((0,), (0,)), ((), ())), preferred_element_type=F32)


def _inproj_kernel(x_ref, g_ref, w_ref, cos_ref, sin_ref, o_ref):
    hn = _rms_norm(x_ref[0], g_ref[...]).astype(BF16)
    cos = cos_ref[...]
    sin = sin_ref[...]
    n_blocks = AB_IN_W // IN_CB
    att_q_block = (2 * RET_QK_W + 2 * RET_V_W) // IN_CB
    for j in range(n_blocks):
        z = _dot(hn, w_ref[:, j * IN_CB:(j + 1) * IN_CB])
        if j < 2 * RET_QK_W // IN_CB:
            parts = []
            for h in range(IN_CB // RET_QK_DIM):
                xh = z[:, h * RET_QK_DIM:(h + 1) * RET_QK_DIM]
                parts.append(xh * cos + pltpu.roll(xh, RET_QK_DIM // 2, axis=1) * sin)
            z = jnp.concatenate(parts, axis=1)
        elif j == att_q_block:
            z = z * (ATT_HEAD_DIM ** -0.5 * LOG2E)
        o_ref[0, :, j * IN_CB:(j + 1) * IN_CB] = z.astype(BF16)


def _inproj(x, g, w_in, cos_t, sin_t):
    B, T, D = x.shape
    tm = TM_PROJ
    return pl.pallas_call(
        _inproj_kernel,
        grid=(B, T // tm),
        in_specs=[
            pl.BlockSpec((1, tm, D), lambda b, t: (b, t, 0)),
            _const_spec((1, D)),
            _const_spec((D, AB_IN_W)),
            pl.BlockSpec((tm, RET_QK_DIM), lambda b, t: (t, 0)),
            pl.BlockSpec((tm, RET_QK_DIM), lambda b, t: (t, 0)),
        ],
        out_specs=pl.BlockSpec((1, tm, AB_IN_W), lambda b, t: (b, t, 0)),
        out_shape=jax.ShapeDtypeStruct((B, T, AB_IN_W), BF16),
        compiler_params=_params(2),
        name="ab_inproj",
    )(x, g, w_in, cos_t, sin_t)


def _ret_tables(tr):
    h = np.arange(RET_HEADS, dtype=np.float64)
    log_g = np.log1p(-np.exp2(-5.0 - h))
    n = np.arange(tr)
    cn, cm = n[:, None] // CHUNK, n[None, :] // CHUNK
    diff = (n[:, None] - n[None, :]).astype(np.float64)
    expo = np.where(cn == cm, np.abs(diff), diff)
    dmat = np.where(cm <= cn, np.exp(log_g[:, None, None] * expo), 0.0)
    scale = RET_QK_DIM ** -0.5
    qdec = np.exp(log_g[:, None] * (n + 1.0)[None, :])
    kdec = np.exp(log_g[:, None] * (tr - 1.0 - n)[None, :]) * scale
    qdec = np.broadcast_to(qdec[:, :, None], (RET_HEADS, tr, RET_QK_DIM))
    kdec = np.broadcast_to(kdec[:, :, None], (RET_HEADS, tr, RET_QK_DIM))
    tile_dec = [float(v) for v in np.exp(log_g * tr)]
    return (jnp.asarray(dmat * scale, F32), jnp.asarray(qdec, F32), jnp.asarray(kdec, F32),
            tile_dec)


def _ret_kernel(tile_dec, q_ref, k_ref, v_ref, g_ref, d_ref, qd_ref, kd_ref, o_ref, state_ref):
    @pl.when(pl.program_id(1) == 0)
    def _():
        state_ref[...] = jnp.zeros_like(state_ref)

    for h in range(RET_HEADS):
        qk = slice(h * RET_QK_DIM, (h + 1) * RET_QK_DIM)
        vv = slice(h * RET_V_DIM, (h + 1) * RET_V_DIM)
        q = q_ref[0, :, qk]
        k = k_ref[0, :, qk]
        v = v_ref[0, :, vv]
        state = state_ref[h]
        s = _dot_nt(q, k) * d_ref[h]
        qd = (q.astype(F32) * qd_ref[h]).astype(BF16)
        kd = (k.astype(F32) * kd_ref[h]).astype(BF16)
        r = _dot(s.astype(BF16), v) + _dot(qd, state.astype(BF16))
        state_ref[h] = state * tile_dec[h] + _dot_tn(kd, v)
        mu = jnp.mean(r, axis=-1, keepdims=True)
        rc = r - mu
        var = jnp.mean(rc * rc, axis=-1, keepdims=True)
        rn = rc * lax.rsqrt(var + EPS)
        gate = g_ref[0, :, vv].astype(F32)
        o_ref[0, :, vv] = (0.5 * gate * (1.0 + jnp.tanh(0.5 * gate)) * rn).astype(BF16)


def _retention(z):
    B, T, _ = z.shape
    tr = TR_RET
    dmat, qdec, kdec, tile_dec = _ret_tables(tr)
    return pl.pallas_call(
        functools.partial(_ret_kernel, tile_dec),
        grid=(B, T // tr),
        in_specs=[
            pl.BlockSpec((1, tr, RET_QK_W), lambda b, t: (b, t, 0)),
            pl.BlockSpec((1, tr, RET_QK_W), lambda b, t: (b, t, 1)),
            pl.BlockSpec((1, tr, RET_V_W), lambda b, t: (b, t, 1)),
            pl.BlockSpec((1, tr, RET_V_W), lambda b, t: (b, t, 2)),
            _const_spec((RET_HEADS, tr, tr)),
            _const_spec((RET_HEADS, tr, RET_QK_DIM)),
            _const_spec((RET_HEADS, tr, RET_QK_DIM)),
        ],
        out_specs=pl.BlockSpec((1, tr, RET_V_W), lambda b, t: (b, t, 0)),
        out_shape=jax.ShapeDtypeStruct((B, T, RET_V_W), BF16),
        scratch_shapes=[pltpu.VMEM((RET_HEADS, RET_QK_DIM, RET_V_DIM), F32)],
        compiler_params=_params(2),
        name="retention",
    )(z, z, z, z, dmat, qdec, kdec)


def _att_bias_vectors(rel_bias):
    hist = ATT_PAST_CHUNKS * CHUNK
    kw = QB_ATT + hist
    nr = 2 * MAX_REL + 1
    period = 2 * hist
    assert period >= kw + QB_ATT - 1 and hist > MAX_REL
    n_far = hist - MAX_REL + 1
    n_near = kw - (hist + MAX_REL)
    n_wrap = period - kw
    first, last = rel_bias[:, 0:1], rel_bias[:, nr - 1:nr]
    return jnp.concatenate([
        jnp.broadcast_to(last, (ATT_HEADS, n_far)),
        rel_bias[:, nr - 2:0:-1],
        jnp.broadcast_to(first, (ATT_HEADS, n_near)),
        jnp.broadcast_to(last, (ATT_HEADS, n_wrap))], axis=1).astype(F32)


def _att_kernel(q_ref, k_ref, v_ref, g_ref, o_ref, kbuf, vbuf, bias_ref):
    t = pl.program_id(1)
    ta = TA_ATT
    hist = ATT_PAST_CHUNKS * CHUNK
    kw = QB_ATT + hist
    period = g_ref.shape[1]
    n_qb = ta // QB_ATT

    @pl.when(t == 0)
    def _():
        kbuf[0:hist, :] = jnp.zeros((hist, ATT_W), BF16)
        vbuf[0:hist, :] = jnp.zeros((hist, ATT_W), BF16)

    @pl.when(t <= 1)
    def _():
        ci = lax.broadcasted_iota(jnp.int32, (QB_ATT, kw), 0) // CHUNK
        col = lax.broadcasted_iota(jnp.int32, (QB_ATT, kw), 1)
        cj = col // CHUNK
        in_band = jnp.logical_and(cj >= ci, cj <= ci + ATT_PAST_CHUNKS)
        for qb in range(n_qb):
            valid = jnp.logical_and(in_band, col >= hist - qb * QB_ATT - t * ta)
            for h in range(ATT_HEADS):
                rows = jnp.broadcast_to(g_ref[h:h + 1, :], (QB_ATT, period))
                skew = pltpu.roll(rows, 0, axis=1, stride=1, stride_axis=0)
                bias_ref[qb, h] = jnp.where(valid, skew[:, :kw] * LOG2E, NEG_INF)

    kbuf[hist:hist + ta, :] = k_ref[0]
    vbuf[hist:hist + ta, :] = v_ref[0]

    lane = lax.broadcasted_iota(jnp.int32, (1, 2 * ATT_HEAD_DIM), 1)
    low = lane < ATT_HEAD_DIM
    for qb in range(n_qb):
        rows = slice(qb * QB_ATT, (qb + 1) * QB_ATT)
        krows = slice(qb * QB_ATT, qb * QB_ATT + kw)
        for hp in range(ATT_HEADS // 2):
            lanes = slice(hp * 2 * ATT_HEAD_DIM, (hp + 1) * 2 * ATT_HEAD_DIM)
            qp = q_ref[0, rows, lanes]
            kp = kbuf[krows, lanes]
            vp = vbuf[krows, lanes]
            outs = []
            for e in range(2):
                keep = low if e == 0 else jnp.logical_not(low)
                qm = jnp.where(keep, qp, jnp.zeros_like(qp))
                s = _dot_nt(qm, kp) + bias_ref[qb, 2 * hp + e]
                m = jnp.max(s, axis=-1, keepdims=True)
                p = jnp.exp2(s - m).astype(BF16)
                ov = _dot(p, jnp.where(keep, vp, jnp.ones_like(vp)))
                outs.append(ov / pltpu.roll(ov, ATT_HEAD_DIM, axis=1))
            o_ref[0, rows, lanes] = jnp.where(low, outs[0], outs[1]).astype(BF16)

    kbuf[0:hist, :] = kbuf[ta:ta + hist, :]
    vbuf[0:hist, :] = vbuf[ta:ta + hist, :]


def _attention(z, rel_bias):
    B, T, _ = z.shape
    ta = TA_ATT
    hist = ATT_PAST_CHUNKS * CHUNK
    assert ta == hist and T // ta >= 2
    q_blk = (2 * RET_QK_W + 2 * RET_V_W) // ATT_W
    g = _att_bias_vectors(rel_bias)
    return pl.pallas_call(
        _att_kernel,
        grid=(B, T // ta),
        in_specs=[
            pl.BlockSpec((1, ta, ATT_W), lambda b, t: (b, t, q_blk)),
            pl.BlockSpec((1, ta, ATT_W), lambda b, t: (b, t, q_blk + 1)),
            pl.BlockSpec((1, ta, ATT_W), lambda b, t: (b, t, q_blk + 2)),
            _const_spec(g.shape),
        ],
        out_specs=pl.BlockSpec((1, ta, ATT_W), lambda b, t: (b, t, 0)),
        out_shape=jax.ShapeDtypeStruct((B, T, ATT_W), BF16),
        scratch_shapes=[pltpu.VMEM((hist + ta, ATT_W), BF16),
                        pltpu.VMEM((hist + ta, ATT_W), BF16),
                        pltpu.VMEM((ta // QB_ATT, ATT_HEADS, QB_ATT, QB_ATT + hist), F32)],
        compiler_params=_params(2),
        name="chunk_attention",
    )(z, z, z, g)


def _outproj_kernel(x_ref, ya_ref, yb_ref, w_ref, o_ref):
    acc = _dot(ya_ref[0], w_ref[0:RET_V_W, :]) + _dot(yb_ref[0], w_ref[RET_V_W:AB_OUT_W, :])
    o_ref[0] = x_ref[0] + acc


def _outproj(x, ya, yb, w_out):
    B, T, D = x.shape
    tm = TM_PROJ
    return pl.pallas_call(
        _outproj_kernel,
        grid=(B, T // tm),
        in_specs=[
            pl.BlockSpec((1, tm, D), lambda b, t: (b, t, 0)),
            pl.BlockSpec((1, tm, RET_V_W), lambda b, t: (b, t, 0)),
            pl.BlockSpec((1, tm, ATT_W), lambda b, t: (b, t, 0)),
            _const_spec((AB_OUT_W, D)),
        ],
        out_specs=pl.BlockSpec((1, tm, D), lambda b, t: (b, t, 0)),
        out_shape=jax.ShapeDtypeStruct((B, T, D), F32),
        compiler_params=_params(2),
        name="ab_outproj",
    )(x, ya, yb, w_out)


def _ffn_kernel(final_norm, x_ref, g_ref, wup_ref, cw_ref, cb_ref, wdn_ref, fg_ref,
                o_ref, hn_ext, zs_ref, act_ref, ys_ref):
    tm = x_ref.shape[1]
    pad = BF16_SUBLANES
    n_slab = FFN_CB // LANES
    half_rows = tm // 2

    @pl.when(pl.program_id(1) == 0)
    def _():
        hn_ext[0:pad, :] = jnp.zeros((pad, D_MODEL), BF16)

    for r0 in range(0, tm, NORM_ROWS):
        xs = x_ref[0, r0:r0 + NORM_ROWS, :]
        hn_ext[pad + r0:pad + r0 + NORM_ROWS, :] = _rms_norm(xs, g_ref[...]).astype(BF16)
    lhs = hn_ext[...]
    for j in range(FFN_HIDDEN // FFN_CB):
        buf = j % 2
        for half in range(2):
            c0 = half * FFN_HIDDEN + j * FFN_CB
            z = _dot(lhs, wup_ref[:, c0:c0 + FFN_CB])
            for l in range(n_slab):
                zs_ref[buf, half * n_slab + l] = z[:, l * LANES:(l + 1) * LANES]
        for par in range(2):
            for l in range(n_slab):
                for u0 in range(0, half_rows, CONV_ROWS):
                    conv = []
                    for half in range(2):
                        c0 = half * FFN_HIDDEN + j * FFN_CB + l * LANES
                        c = cb_ref[:, c0:c0 + LANES]
                        for tap in range(CONV_WIDTH):
                            off = pad - (CONV_WIDTH - 1) + tap + par + 2 * u0
                            c = c + (zs_ref[buf, half * n_slab + l,
                                            pl.ds(off, CONV_ROWS, stride=2), :]
                                     * cw_ref[tap:tap + 1, c0:c0 + LANES])
                        conv.append(c)
                    a0 = j * FFN_CB + l * LANES
                    act_ref[par * half_rows + u0:par * half_rows + u0 + CONV_ROWS,
                            a0:a0 + LANES] = (_gelu_tanh(conv[0]) * conv[1]).astype(BF16)
    hn_ext[0:pad, :] = hn_ext[tm:tm + pad, :]
    yp = _dot(act_ref[...], wdn_ref[...])
    for l in range(D_MODEL // LANES):
        for par in range(2):
            ys_ref[l, pl.ds(par, half_rows, stride=2), :] = (
                yp[par * half_rows:(par + 1) * half_rows, l * LANES:(l + 1) * LANES])
    for r0 in range(0, tm, NORM_ROWS):
        rows = slice(r0, r0 + NORM_ROWS)
        y = x_ref[0, rows, :] + jnp.concatenate(
            [ys_ref[l, rows, :] for l in range(D_MODEL // LANES)], axis=1)
        if final_norm:
            y = _rms_norm(y, fg_ref[...])
        o_ref[0, rows, :] = y


def _ffn(x, g, w_up, conv_w, conv_b, w_down, final_g=None):
    B, T, D = x.shape
    tm = TM_FFN
    final_norm = final_g is not None
    if final_g is None:
        final_g = jnp.ones((D,), F32)
    return pl.pallas_call(
        functools.partial(_ffn_kernel, final_norm),
        grid=(B, T // tm),
        in_specs=[
            pl.BlockSpec((1, tm, D), lambda b, t: (b, t, 0)),
            _const_spec((1, D)),
            _const_spec((D, 2 * FFN_HIDDEN)),
            _const_spec((CONV_WIDTH, 2 * FFN_HIDDEN)),
            _const_spec((1, 2 * FFN_HIDDEN)),
            _const_spec((FFN_HIDDEN, D)),
            _const_spec((1, D)),
        ],
        out_specs=pl.BlockSpec((1, tm, D), lambda b, t: (b, t, 0)),
        out_shape=jax.ShapeDtypeStruct((B, T, D), F32),
        scratch_shapes=[
            pltpu.VMEM((tm + BF16_SUBLANES, D), BF16),
            pltpu.VMEM((2, 2 * FFN_CB // LANES, tm + BF16_SUBLANES, LANES), F32),
            pltpu.VMEM((tm, FFN_HIDDEN), BF16),
            pltpu.VMEM((D // LANES, tm, LANES), F32),
        ],
        compiler_params=_params(2),
        name="conv_ffn_final" if final_norm else "conv_ffn",
    )(x, g[None, :], w_up.astype(BF16), conv_w, conv_b[None, :],
      w_down.astype(BF16), final_g[None, :])


def _sgu_kernel(x_ref, g_ref, win_ref, lng_ref, lnb_ref, ws_ref, bs_ref, wout_ref, o_ref,
                v_ref, y_ref):
    tm = x_ref.shape[1]
    gw = SGU_WIDTH // SGU_GROUPS
    pair = 2 * SGU_BLOCK
    x = x_ref[0]
    hn = _rms_norm(x, g_ref[...]).astype(BF16)

    vsum = jnp.zeros((tm, 1), F32)
    for j in range(SGU_WIDTH // IN_CB):
        cols = slice(j * IN_CB, (j + 1) * IN_CB)
        zv = _gelu_tanh(_dot(hn, win_ref[:, SGU_WIDTH + j * IN_CB:SGU_WIDTH + (j + 1) * IN_CB]))
        v_ref[:, cols] = zv
        vsum = vsum + jnp.sum(zv, axis=-1, keepdims=True)
    mu = vsum * (1.0 / SGU_WIDTH)
    vsq = jnp.zeros((tm, 1), F32)
    for j in range(SGU_WIDTH // IN_CB):
        cols = slice(j * IN_CB, (j + 1) * IN_CB)
        vc = v_ref[:, cols] - mu
        vsq = vsq + jnp.sum(vc * vc, axis=-1, keepdims=True)
    rstd = lax.rsqrt(vsq * (1.0 / SGU_WIDTH) + EPS)

    ri = lax.broadcasted_iota(jnp.int32, (pair, pair), 0)
    ci = lax.broadcasted_iota(jnp.int32, (pair, pair), 1)
    allowed = (ci // CHUNK) <= (ri // CHUNK)
    same = (ci // SGU_BLOCK) == (ri // SGU_BLOCK)
    keep = jnp.logical_and(allowed, same)
    for gidx in range(SGU_GROUPS):
        cols = slice(gidx * gw, (gidx + 1) * gw)
        w = jnp.where(keep, ws_ref[gidx], jnp.zeros((pair, pair), BF16))
        vn = ((v_ref[:, cols] - mu) * rstd * lng_ref[:, cols] + lnb_ref[:, cols]).astype(BF16)
        u = _gelu_tanh(_dot(hn, win_ref[:, cols]))
        for rp in range(tm // pair):
            rows = slice(rp * pair, (rp + 1) * pair)
            mixed = _dot(w, vn[rows]) + bs_ref[:, cols]
            y_ref[rows, cols] = (u[rows] * mixed).astype(BF16)
    o_ref[0] = x + _dot(y_ref[...], wout_ref[...])


def _sgu(x, g, w_in, ln_g, ln_b, w_s, b_s, w_out):
    B, T, D = x.shape
    tm = TM_PROJ
    gw = SGU_WIDTH // SGU_GROUPS
    pair = 2 * SGU_BLOCK
    zero = jnp.zeros_like(w_s)
    w_bd = jnp.concatenate([jnp.concatenate([w_s, zero], axis=2),
                            jnp.concatenate([zero, w_s], axis=2)], axis=1).astype(BF16)
    b_tab = jnp.repeat(b_s.T, gw, axis=1)
    b_tab = jnp.concatenate([b_tab, b_tab], axis=0)
    return pl.pallas_call(
        _sgu_kernel,
        grid=(B, T // tm),
        in_specs=[
            pl.BlockSpec((1, tm, D), lambda b, t: (b, t, 0)),
            _const_spec((1, D)),
            _const_spec((D, 2 * SGU_WIDTH)),
            _const_spec((1, SGU_WIDTH)),
            _const_spec((1, SGU_WIDTH)),
            _const_spec((SGU_GROUPS, pair, pair)),
            _const_spec((pair, SGU_WIDTH)),
            _const_spec((SGU_WIDTH, D)),
        ],
        out_specs=pl.BlockSpec((1, tm, D), lambda b, t: (b, t, 0)),
        out_shape=jax.ShapeDtypeStruct((B, T, D), F32),
        scratch_shapes=[
            pltpu.VMEM((tm, SGU_WIDTH), F32),
            pltpu.VMEM((tm, SGU_WIDTH), BF16),
        ],
        compiler_params=_params(2),
        name="sgu_mixer",
    )(x, g[None, :], w_in.astype(BF16), ln_g[None, :], ln_b[None, :], w_bd, b_tab,
      w_out.astype(BF16))


def _rotary_tables(T):
    half = RET_QK_DIM // 2
    inv = 1.0 / (10000.0 ** jnp.linspace(0.0, 1.0, half, dtype=F32))
    ang = jnp.arange(T).astype(F32)[:, None] * inv[None, :]
    cos, sin = jnp.cos(ang), jnp.sin(ang)
    return jnp.concatenate([cos, cos], axis=1), jnp.concatenate([-sin, sin], axis=1)


def kernel(x, attn_norm_g, ffn_norm_g, ab_w_in, ab_w_out, ab_rel_bias, c_w_in, c_ln_g, c_ln_b,
           c_w_s, c_b_s, c_w_out, ffn_w_up, ffn_conv_w, ffn_conv_b, ffn_w_down, final_norm_g):
    T = x.shape[1]
    cos_t, sin_t = _rotary_tables(T)
    z = _inproj(x, attn_norm_g[0][None, :], ab_w_in[0].astype(BF16), cos_t, sin_t)
    ya = _retention(z)
    yb = _attention(z, ab_rel_bias[0])
    h = _outproj(x, ya, yb, ab_w_out[0].astype(BF16))
    h = _ffn(h, ffn_norm_g[0], ffn_w_up[0], ffn_conv_w[0], ffn_conv_b[0], ffn_w_down[0])
    h = _sgu(h, attn_norm_g[1], c_w_in[0], c_ln_g[0], c_ln_b[0], c_w_s[0], c_b_s[0], c_w_out[0])
    h = _ffn(h, ffn_norm_g[1], ffn_w_up[1], ffn_conv_w[1], ffn_conv_b[1], ffn_w_down[1],
             final_g=final_norm_g)
    return h
```

```python
import functools
import math

import numpy as np
import jax
import jax.numpy as jnp
from jax import lax
from jax.experimental import pallas as pl
from jax.experimental.pallas import tpu as pltpu

F32 = jnp.float32
BF16 = jnp.bfloat16

D_MODEL = 1024
CHUNK = 64
EPS = 1e-6
NEG_INF = -1e30
LOG2E = math.log2(math.e)

RET_HEADS = 4
RET_QK_DIM = 128
RET_V_DIM = 256
ATT_HEADS = 8
ATT_HEAD_DIM = 64
ATT_PAST_CHUNKS = 8
MAX_REL = 128
SGU_BLOCK = 128
SGU_GROUPS = 8
SGU_WIDTH = 2048
FFN_HIDDEN = 2816
CONV_WIDTH = 3

RET_QK_W = RET_HEADS * RET_QK_DIM
RET_V_W = RET_HEADS * RET_V_DIM
ATT_W = ATT_HEADS * ATT_HEAD_DIM
AB_IN_W = 2 * RET_QK_W + 2 * RET_V_W + 3 * ATT_W
AB_OUT_W = RET_V_W + ATT_W

VMEM_LIMIT_BYTES = 56 * 1024 * 1024
BF16_SUBLANES = 16
LANES = 128

TM_PROJ = 512
TM_FFN = 512
TR_RET = 256
TA_ATT = 512
QB_ATT = 256
FFN_CB = 256
IN_CB = 512
NORM_ROWS = 32
CONV_ROWS = 64


def _const_spec(shape):
    zeros = (0,) * len(shape)
    return pl.BlockSpec(shape, lambda *_: zeros, pipeline_mode=pl.Buffered(1))


def _params(n_axes, flags=None):
    return pltpu.CompilerParams(
        dimension_semantics=("arbitrary",) * n_axes,
        vmem_limit_bytes=VMEM_LIMIT_BYTES,
        flags=flags)


def _rms_norm(x, g):
    return x * lax.rsqrt(jnp.mean(x * x, axis=-1, keepdims=True) + EPS) * g


def _gelu_tanh(x):
    c = math.sqrt(2.0 / math.pi)
    return 0.5 * x * (1.0 + jnp.tanh(c * (x + 0.044715 * (x * x * x))))


def _dot(a, b):
    return jnp.dot(a, b, preferred_element_type=F32)


def _dot_nt(a, b):
    return lax.dot_general(a, b, (((1,), (1,)), ((), ())), preferred_element_type=F32)


def _dot_tn(a, b):
    return lax.dot_general(a, b, (((0,), (0,)), ((), ())), preferred_element_type=F32)


def _inproj_kernel(x_ref, g_ref, w_ref, base_ref, off_ref, o_ref):
    hn = _rms_norm(x_ref[0], g_ref[...]).astype(BF16)
    ca, sa = base_ref[0, 0:1, :], base_ref[0, 1:2, :]
    cos = ca * off_ref[0] - sa * off_ref[1]
    sin = sa * off_ref[2] + ca * off_ref[3]
    n_blocks = AB_IN_W // IN_CB
    att_q_block = (2 * RET_QK_W + 2 * RET_V_W) // IN_CB
    for j in range(n_blocks):
        z = _dot(hn, w_ref[:, j * IN_CB:(j + 1) * IN_CB])
        if j < 2 * RET_QK_W // IN_CB:
            parts = []
            for h in range(IN_CB // RET_QK_DIM):
                xh = z[:, h * RET_QK_DIM:(h + 1) * RET_QK_DIM]
                parts.append(xh * cos + pltpu.roll(xh, RET_QK_DIM // 2, axis=1) * sin)
            z = jnp.concatenate(parts, axis=1)
        elif j == att_q_block:
            z = z * (ATT_HEAD_DIM ** -0.5 * LOG2E)
        o_ref[0, :, j * IN_CB:(j + 1) * IN_CB] = z.astype(BF16)


def _rotary_tables(T, tm):
    half = RET_QK_DIM // 2
    inv = 1.0 / (np.float32(10000.0) ** np.linspace(0.0, 1.0, half, dtype=np.float32))
    inv = np.concatenate([inv, inv]).astype(np.float64)
    sign = np.concatenate([-np.ones(half), np.ones(half)])
    base = (np.arange(T // tm) * tm)[:, None] * inv[None, :]
    off = np.arange(tm)[:, None] * inv[None, :]
    bases = np.stack([np.cos(base), np.sin(base)], axis=1)
    offs = np.stack([np.cos(off), np.sin(off), sign * np.cos(off), sign * np.sin(off)])
    return jnp.asarray(bases, F32), jnp.asarray(offs, F32)


def _inproj(x, g, w_in):
    B, T, D = x.shape
    tm = TM_PROJ
    bases, offs = _rotary_tables(T, tm)
    return pl.pallas_call(
        _inproj_kernel,
        grid=(B, T // tm),
        in_specs=[
            pl.BlockSpec((1, tm, D), lambda b, t: (b, t, 0)),
            _const_spec((1, D)),
            _const_spec((D, AB_IN_W)),
            pl.BlockSpec((1, 2, RET_QK_DIM), lambda b, t: (t, 0, 0)),
            _const_spec(offs.shape),
        ],
        out_specs=pl.BlockSpec((1, tm, AB_IN_W), lambda b, t: (b, t, 0)),
        out_shape=jax.ShapeDtypeStruct((B, T, AB_IN_W), BF16),
        compiler_params=_params(2),
        name="ab_inproj",
    )(x, g, w_in, bases, offs)


def _ret_tables(tr):
    h = np.arange(RET_HEADS, dtype=np.float64)
    log_g = np.log1p(-np.exp2(-5.0 - h))
    n = np.arange(tr)
    cn, cm = n[:, None] // CHUNK, n[None, :] // CHUNK
    diff = (n[:, None] - n[None, :]).astype(np.float64)
    expo = np.where(cn == cm, np.abs(diff), diff)
    dmat = np.where(cm <= cn, np.exp(log_g[:, None, None] * expo), 0.0)
    scale = RET_QK_DIM ** -0.5
    qdec = np.exp(log_g[:, None] * (n + 1.0)[None, :])
    kdec = np.exp(log_g[:, None] * (tr - 1.0 - n)[None, :]) * scale
    qdec = np.broadcast_to(qdec[:, :, None], (RET_HEADS, tr, RET_QK_DIM))
    kdec = np.broadcast_to(kdec[:, :, None], (RET_HEADS, tr, RET_QK_DIM))
    tile_dec = [float(v) for v in np.exp(log_g * tr)]
    return (jnp.asarray(dmat * scale, F32), jnp.asarray(qdec, F32), jnp.asarray(kdec, F32),
            tile_dec)


def _ret_kernel(tile_dec, q_ref, k_ref, v_ref, g_ref, d_ref, qd_ref, kd_ref, o_ref, state_ref):
    @pl.when(pl.program_id(1) == 0)
    def _():
        state_ref[...] = jnp.zeros_like(state_ref)

    for h in range(RET_HEADS):
        qk = slice(h * RET_QK_DIM, (h + 1) * RET_QK_DIM)
        vv = slice(h * RET_V_DIM, (h + 1) * RET_V_DIM)
        q = q_ref[0, :, qk]
        k = k_ref[0, :, qk]
        v = v_ref[0, :, vv]
        state = state_ref[h]
        s = _dot_nt(q, k) * d_ref[h]
        qd = (q.astype(F32) * qd_ref[h]).astype(BF16)
        kd = (k.astype(F32) * kd_ref[h]).astype(BF16)
        r = _dot(s.astype(BF16), v) + _dot(qd, state.astype(BF16))
        state_ref[h] = state * tile_dec[h] + _dot_tn(kd, v)
        mu = jnp.mean(r, axis=-1, keepdims=True)
        rc = r - mu
        var = jnp.mean(rc * rc, axis=-1, keepdims=True)
        rn = rc * lax.rsqrt(var + EPS)
        gate = g_ref[0, :, vv].astype(F32)
        o_ref[0, :, vv] = (0.5 * gate * (1.0 + jnp.tanh(0.5 * gate)) * rn).astype(BF16)


def _retention(z):
    B, T, _ = z.shape
    tr = TR_RET
    dmat, qdec, kdec, tile_dec = _ret_tables(tr)
    return pl.pallas_call(
        functools.partial(_ret_kernel, tile_dec),
        grid=(B, T // tr),
        in_specs=[
            pl.BlockSpec((1, tr, RET_QK_W), lambda b, t: (b, t, 0)),
            pl.BlockSpec((1, tr, RET_QK_W), lambda b, t: (b, t, 1)),
            pl.BlockSpec((1, tr, RET_V_W), lambda b, t: (b, t, 1)),
            pl.BlockSpec((1, tr, RET_V_W), lambda b, t: (b, t, 2)),
            _const_spec((RET_HEADS, tr, tr)),
            _const_spec((RET_HEADS, tr, RET_QK_DIM)),
            _const_spec((RET_HEADS, tr, RET_QK_DIM)),
        ],
        out_specs=pl.BlockSpec((1, tr, RET_V_W), lambda b, t: (b, t, 0)),
        out_shape=jax.ShapeDtypeStruct((B, T, RET_V_W), BF16),
        scratch_shapes=[pltpu.VMEM((RET_HEADS, RET_QK_DIM, RET_V_DIM), F32)],
        compiler_params=_params(2),
        name="retention",
    )(z, z, z, z, dmat, qdec, kdec)


def _att_bias_vectors(rel_bias):
    hist = ATT_PAST_CHUNKS * CHUNK
    kw = QB_ATT + hist
    nr = 2 * MAX_REL + 1
    period = 2 * hist
    assert period >= kw + QB_ATT - 1 and hist > MAX_REL
    n_far = hist - MAX_REL + 1
    n_near = kw - (hist + MAX_REL)
    n_wrap = period - kw
    first, last = rel_bias[:, 0:1], rel_bias[:, nr - 1:nr]
    return jnp.concatenate([
        jnp.broadcast_to(last, (ATT_HEADS, n_far)),
        rel_bias[:, nr - 2:0:-1],
        jnp.broadcast_to(first, (ATT_HEADS, n_near)),
        jnp.broadcast_to(last, (ATT_HEADS, n_wrap))], axis=1).astype(F32)


def _att_kernel(q_ref, k_ref, v_ref, g_ref, o_ref, kbuf, vbuf, bias_ref):
    t = pl.program_id(1)
    ta = TA_ATT
    hist = ATT_PAST_CHUNKS * CHUNK
    kw = QB_ATT + hist
    period = g_ref.shape[1]
    n_qb = ta // QB_ATT

    @pl.when(t == 0)
    def _():
        kbuf[0:hist, :] = jnp.zeros((hist, ATT_W), BF16)
        vbuf[0:hist, :] = jnp.zeros((hist, ATT_W), BF16)

    @pl.when(t <= 1)
    def _():
        ci = lax.broadcasted_iota(jnp.int32, (QB_ATT, kw), 0) // CHUNK
        col = lax.broadcasted_iota(jnp.int32, (QB_ATT, kw), 1)
        cj = col // CHUNK
        in_band = jnp.logical_and(cj >= ci, cj <= ci + ATT_PAST_CHUNKS)
        for qb in range(n_qb):
            valid = jnp.logical_and(in_band, col >= hist - qb * QB_ATT - t * ta)
            for h in range(ATT_HEADS):
                rows = jnp.broadcast_to(g_ref[h:h + 1, :], (QB_ATT, period))
                skew = pltpu.roll(rows, 0, axis=1, stride=1, stride_axis=0)
                bias_ref[qb, h] = jnp.where(valid, skew[:, :kw] * LOG2E, NEG_INF)

    kbuf[hist:hist + ta, :] = k_ref[0]
    vbuf[hist:hist + ta, :] = v_ref[0]

    lane = lax.broadcasted_iota(jnp.int32, (1, 2 * ATT_HEAD_DIM), 1)
    low = lane < ATT_HEAD_DIM
    for qb in range(n_qb):
        rows = slice(qb * QB_ATT, (qb + 1) * QB_ATT)
        krows = slice(qb * QB_ATT, qb * QB_ATT + kw)
        for hp in range(ATT_HEADS // 2):
            lanes = slice(hp * 2 * ATT_HEAD_DIM, (hp + 1) * 2 * ATT_HEAD_DIM)
            qp = q_ref[0, rows, lanes]
            kp = kbuf[krows, lanes]
            vp = vbuf[krows, lanes]
            outs = []
            for e in range(2):
                keep = low if e == 0 else jnp.logical_not(low)
                qm = jnp.where(keep, qp, jnp.zeros_like(qp))
                s = _dot_nt(qm, kp) + bias_ref[qb, 2 * hp + e]
                m = jnp.max(s, axis=-1, keepdims=True)
                p = jnp.exp2(s - m).astype(BF16)
                ov = _dot(p, jnp.where(keep, vp, jnp.ones_like(vp)))
                outs.append(ov / pltpu.roll(ov, ATT_HEAD_DIM, axis=1))
            o_ref[0, rows, lanes] = jnp.where(low, outs[0], outs[1]).astype(BF16)

    kbuf[0:hist, :] = kbuf[ta:ta + hist, :]
    vbuf[0:hist, :] = vbuf[ta:ta + hist, :]


def _attention(z, rel_bias):
    B, T, _ = z.shape
    ta = TA_ATT
    hist = ATT_PAST_CHUNKS * CHUNK
    assert ta == hist and T // ta >= 2
    q_blk = (2 * RET_QK_W + 2 * RET_V_W) // ATT_W
    g = _att_bias_vectors(rel_bias)
    return pl.pallas_call(
        _att_kernel,
        grid=(B, T // ta),
        in_specs=[
            pl.BlockSpec((1, ta, ATT_W), lambda b, t: (b, t, q_blk)),
            pl.BlockSpec((1, ta, ATT_W), lambda b, t: (b, t, q_blk + 1)),
            pl.BlockSpec((1, ta, ATT_W), lambda b, t: (b, t, q_blk + 2)),
            _const_spec(g.shape),
        ],
        out_specs=pl.BlockSpec((1, ta, ATT_W), lambda b, t: (b, t, 0)),
        out_shape=jax.ShapeDtypeStruct((B, T, ATT_W), BF16),
        scratch_shapes=[pltpu.VMEM((hist + ta, ATT_W), BF16),
                        pltpu.VMEM((hist + ta, ATT_W), BF16),
                        pltpu.VMEM((ta // QB_ATT, ATT_HEADS, QB_ATT, QB_ATT + hist), F32)],
        compiler_params=_params(2),
        name="chunk_attention",
    )(z, z, z, g)


def _ffn_kernel(final_norm, mixer_proj, *refs):
    if mixer_proj:
        x_ref, ya_ref, yb_ref, wmix_ref = refs[:4]
        refs = refs[4:]
    else:
        x_ref = refs[0]
        refs = refs[1:]
    (g_ref, wup_ref, cw_ref, cb_ref, wdn_ref, fg_ref, o_ref,
     hn_ext, zs_ref, act_ref, ys_ref) = refs
    tm = x_ref.shape[1]
    pad = BF16_SUBLANES
    n_slab = FFN_CB // LANES
    half_rows = tm // 2
    n_lane_slabs = D_MODEL // LANES

    @pl.when(pl.program_id(1) == 0)
    def _():
        hn_ext[0:pad, :] = jnp.zeros((pad, D_MODEL), BF16)

    if mixer_proj:
        na = ya_ref.shape[2]
        mix = _dot(ya_ref[0], wmix_ref[0:na, :]) + _dot(yb_ref[0], wmix_ref[na:, :])
        for l in range(n_lane_slabs):
            ys_ref[l] = mix[:, l * LANES:(l + 1) * LANES]
    for r0 in range(0, tm, NORM_ROWS):
        rows = slice(r0, r0 + NORM_ROWS)
        xs = x_ref[0, rows, :]
        if mixer_proj:
            xs = xs + jnp.concatenate([ys_ref[l, rows, :] for l in range(n_lane_slabs)], axis=1)
            o_ref[0, rows, :] = xs
        hn_ext[pad + r0:pad + r0 + NORM_ROWS, :] = _rms_norm(xs, g_ref[...]).astype(BF16)
    lhs = hn_ext[...]
    n_blocks = FFN_HIDDEN // FFN_CB

    def up_block(j):
        for half in range(2):
            c0 = half * FFN_HIDDEN + j * FFN_CB
            z = _dot(lhs, wup_ref[:, c0:c0 + FFN_CB])
            for l in range(n_slab):
                zs_ref[j % 2, half * n_slab + l] = z[:, l * LANES:(l + 1) * LANES]

    up_block(0)
    for j in range(n_blocks):
        buf = j % 2
        if j + 1 < n_blocks:
            up_block(j + 1)
        for par in range(2):
            for l in range(n_slab):
                for u0 in range(0, half_rows, CONV_ROWS):
                    conv = []
                    for half in range(2):
                        c0 = half * FFN_HIDDEN + j * FFN_CB + l * LANES
                        c = cb_ref[:, c0:c0 + LANES]
                        for tap in range(CONV_WIDTH):
                            off = pad - (CONV_WIDTH - 1) + tap + par + 2 * u0
                            c = c + (zs_ref[buf, half * n_slab + l,
                                            pl.ds(off, CONV_ROWS, stride=2), :]
                                     * cw_ref[tap:tap + 1, c0:c0 + LANES])
                        conv.append(c)
                    a0 = j * FFN_CB + l * LANES
                    act_ref[par * half_rows + u0:par * half_rows + u0 + CONV_ROWS,
                            a0:a0 + LANES] = (_gelu_tanh(conv[0]) * conv[1]).astype(BF16)
    hn_ext[0:pad, :] = hn_ext[tm:tm + pad, :]
    yp = _dot(act_ref[...], wdn_ref[...])
    for l in range(n_lane_slabs):
        for par in range(2):
            ys_ref[l, pl.ds(par, half_rows, stride=2), :] = (
                yp[par * half_rows:(par + 1) * half_rows, l * LANES:(l + 1) * LANES])
    for r0 in range(0, tm, NORM_ROWS):
        rows = slice(r0, r0 + NORM_ROWS)
        y = (o_ref if mixer_proj else x_ref)[0, rows, :] + jnp.concatenate(
            [ys_ref[l, rows, :] for l in range(n_lane_slabs)], axis=1)
        if final_norm:
            y = _rms_norm(y, fg_ref[...])
        o_ref[0, rows, :] = y


def _layer_spec(layer, shape):
    zeros = (0,) * len(shape)
    return pl.BlockSpec((None,) + tuple(shape), lambda *_: (layer,) + zeros,
                        pipeline_mode=pl.Buffered(1))


def _ffn(x, layer, g, w_up, conv_w, conv_b, w_down, final_g=None, mixer=None):
    B, T, D = x.shape
    tm = TM_FFN
    final_norm = final_g is not None
    if final_g is None:
        final_g = jnp.ones((D,), F32)
    tile = lambda width: pl.BlockSpec((1, tm, width), lambda b, t: (b, t, 0))
    mixer_args, mixer_specs = (), []
    if mixer is not None:
        ya, yb, w_mix = mixer
        mixer_args = (ya, yb, w_mix)
        mixer_specs = [tile(ya.shape[2]), tile(yb.shape[2]), _const_spec(w_mix.shape)]
    return pl.pallas_call(
        functools.partial(_ffn_kernel, final_norm, mixer is not None),
        grid=(B, T // tm),
        in_specs=[tile(D)] + mixer_specs + [
            _layer_spec(layer, (1, D)),
            _layer_spec(layer, (D, 2 * FFN_HIDDEN)),
            _layer_spec(layer, (CONV_WIDTH, 2 * FFN_HIDDEN)),
            _layer_spec(layer, (1, 2 * FFN_HIDDEN)),
            _layer_spec(layer, (FFN_HIDDEN, D)),
            _const_spec((1, D)),
        ],
        out_specs=pl.BlockSpec((1, tm, D), lambda b, t: (b, t, 0)),
        out_shape=jax.ShapeDtypeStruct((B, T, D), F32),
        scratch_shapes=[
            pltpu.VMEM((tm + BF16_SUBLANES, D), BF16),
            pltpu.VMEM((2, 2 * FFN_CB // LANES, tm + BF16_SUBLANES, LANES), F32),
            pltpu.VMEM((tm, FFN_HIDDEN), BF16),
            pltpu.VMEM((D // LANES, tm, LANES), F32),
        ],
        compiler_params=_params(2),
        name="conv_ffn_final" if final_norm else "conv_ffn",
    )(x, *mixer_args, g, w_up, conv_w, conv_b, w_down, final_g[None, :])


def _sgu_kernel(x_ref, g_ref, win_ref, lng_ref, lnb_ref, ws_ref, bs_ref, wout_ref, o_ref,
                v_ref, y_ref):
    tm = x_ref.shape[1]
    gw = SGU_WIDTH // SGU_GROUPS
    pair = 2 * SGU_BLOCK
    x = x_ref[0]
    hn = _rms_norm(x, g_ref[...]).astype(BF16)

    vsum = jnp.zeros((tm, 1), F32)
    for j in range(SGU_WIDTH // IN_CB):
        cols = slice(j * IN_CB, (j + 1) * IN_CB)
        zv = _gelu_tanh(_dot(hn, win_ref[:, SGU_WIDTH + j * IN_CB:SGU_WIDTH + (j + 1) * IN_CB]))
        v_ref[:, cols] = zv
        vsum = vsum + jnp.sum(zv, axis=-1, keepdims=True)
    mu = vsum * (1.0 / SGU_WIDTH)
    vsq = jnp.zeros((tm, 1), F32)
    for j in range(SGU_WIDTH // IN_CB):
        cols = slice(j * IN_CB, (j + 1) * IN_CB)
        vc = v_ref[:, cols] - mu
        vsq = vsq + jnp.sum(vc * vc, axis=-1, keepdims=True)
    rstd = lax.rsqrt(vsq * (1.0 / SGU_WIDTH) + EPS)

    ri = lax.broadcasted_iota(jnp.int32, (pair, pair), 0)
    ci = lax.broadcasted_iota(jnp.int32, (pair, pair), 1)
    allowed = (ci // CHUNK) <= (ri // CHUNK)
    same = (ci // SGU_BLOCK) == (ri // SGU_BLOCK)
    keep = jnp.logical_and(allowed, same)
    for gidx in range(SGU_GROUPS):
        cols = slice(gidx * gw, (gidx + 1) * gw)
        w = jnp.where(keep, ws_ref[gidx], jnp.zeros((pair, pair), BF16))
        vn = ((v_ref[:, cols] - mu) * rstd * lng_ref[:, cols] + lnb_ref[:, cols]).astype(BF16)
        u = _gelu_tanh(_dot(hn, win_ref[:, cols]))
        for rp in range(tm // pair):
            rows = slice(rp * pair, (rp + 1) * pair)
            mixed = _dot(w, vn[rows]) + bs_ref[:, cols]
            y_ref[rows, cols] = (u[rows] * mixed).astype(BF16)
    o_ref[0] = x + _dot(y_ref[...], wout_ref[...])


def _sgu(x, g, w_in, ln_g, ln_b, w_s, b_s, w_out):
    B, T, D = x.shape
    tm = TM_PROJ
    gw = SGU_WIDTH // SGU_GROUPS
    pair = 2 * SGU_BLOCK
    zero = jnp.zeros_like(w_s)
    w_bd = jnp.concatenate([jnp.concatenate([w_s, zero], axis=2),
                            jnp.concatenate([zero, w_s], axis=2)], axis=1).astype(BF16)
    b_tab = jnp.repeat(b_s.T, gw, axis=1)
    b_tab = jnp.concatenate([b_tab, b_tab], axis=0)
    return pl.pallas_call(
        _sgu_kernel,
        grid=(B, T // tm),
        in_specs=[
            pl.BlockSpec((1, tm, D), lambda b, t: (b, t, 0)),
            _const_spec((1, D)),
            _const_spec((D, 2 * SGU_WIDTH)),
            _const_spec((1, SGU_WIDTH)),
            _const_spec((1, SGU_WIDTH)),
            _const_spec((SGU_GROUPS, pair, pair)),
            _const_spec((pair, SGU_WIDTH)),
            _const_spec((SGU_WIDTH, D)),
        ],
        out_specs=pl.BlockSpec((1, tm, D), lambda b, t: (b, t, 0)),
        out_shape=jax.ShapeDtypeStruct((B, T, D), F32),
        scratch_shapes=[
            pltpu.VMEM((tm, SGU_WIDTH), F32),
            pltpu.VMEM((tm, SGU_WIDTH), BF16),
        ],
        compiler_params=_params(2),
        name="sgu_mixer",
    )(x, g[None, :], w_in.astype(BF16), ln_g[None, :], ln_b[None, :], w_bd, b_tab,
      w_out.astype(BF16))


def kernel(x, attn_norm_g, ffn_norm_g, ab_w_in, ab_w_out, ab_rel_bias, c_w_in, c_ln_g, c_ln_b,
           c_w_s, c_b_s, c_w_out, ffn_w_up, ffn_conv_w, ffn_conv_b, ffn_w_down, final_norm_g):
    ffn_params = (ffn_norm_g[:, None, :], ffn_w_up.astype(BF16), ffn_conv_w,
                  ffn_conv_b[:, None, :], ffn_w_down.astype(BF16))
    z = _inproj(x, attn_norm_g[0][None, :], ab_w_in[0].astype(BF16))
    ya = _retention(z)
    yb = _attention(z, ab_rel_bias[0])
    h = _ffn(x, 0, *ffn_params, mixer=(ya, yb, ab_w_out[0].astype(BF16)))
    h = _sgu(h, attn_norm_g[1], c_w_in[0], c_ln_g[0], c_ln_b[0], c_w_s[0], c_b_s[0], c_w_out[0])
    h = _ffn(h, 1, *ffn_params, final_g=final_norm_g)
    return h
```

```python
import functools
import math

import numpy as np
import jax
import jax.numpy as jnp
from jax import lax
from jax.experimental import pallas as pl
from jax.experimental.pallas import tpu as pltpu

F32 = jnp.float32
BF16 = jnp.bfloat16

D_MODEL = 1024
CHUNK = 64
EPS = 1e-6
NEG_INF = -1e30
LOG2E = math.log2(math.e)

RET_HEADS = 4
RET_QK_DIM = 128
RET_V_DIM = 256
ATT_HEADS = 8
ATT_HEAD_DIM = 64
ATT_PAST_CHUNKS = 8
MAX_REL = 128
SGU_BLOCK = 128
SGU_GROUPS = 8
SGU_WIDTH = 2048
FFN_HIDDEN = 2816
CONV_WIDTH = 3

RET_QK_W = RET_HEADS * RET_QK_DIM
RET_V_W = RET_HEADS * RET_V_DIM
ATT_W = ATT_HEADS * ATT_HEAD_DIM
AB_IN_W = 2 * RET_QK_W + 2 * RET_V_W + 3 * ATT_W
AB_OUT_W = RET_V_W + ATT_W

VMEM_LIMIT_BYTES = 56 * 1024 * 1024
BF16_SUBLANES = 16
LANES = 128

TM_PROJ = 512
TM_FFN = 512
TR_RET = 256
TA_ATT = 512
QB_ATT = 256
FFN_CB = 256
IN_CB = 512
NORM_ROWS = 32
CONV_ROWS = 64


def _const_spec(shape):
    zeros = (0,) * len(shape)
    return pl.BlockSpec(shape, lambda *_: zeros, pipeline_mode=pl.Buffered(1))


def _params(n_axes, flags=None):
    return pltpu.CompilerParams(
        dimension_semantics=("arbitrary",) * n_axes,
        vmem_limit_bytes=VMEM_LIMIT_BYTES,
        flags=flags)


def _rms_norm(x, g):
    return x * lax.rsqrt(jnp.mean(x * x, axis=-1, keepdims=True) + EPS) * g


def _gelu_tanh(x):
    c = math.sqrt(2.0 / math.pi)
    return 0.5 * x * (1.0 + jnp.tanh(c * (x + 0.044715 * (x * x * x))))


def _dot(a, b):
    return jnp.dot(a, b, preferred_element_type=F32)


def _dot_nt(a, b):
    return lax.dot_general(a, b, (((1,), (1,)), ((), ())), preferred_element_type=F32)


def _dot_tn(a, b):
    return lax.dot_general(a, b, (((0,), (0,)), ((), ())), preferred_element_type=F32)


def _inproj_kernel(x_ref, g_ref, w_ref, base_ref, off_ref, o_ref):
    hn = _rms_norm(x_ref[0], g_ref[...]).astype(BF16)
    ca, sa = base_ref[0, 0:1, :], base_ref[0, 1:2, :]
    cos = ca * off_ref[0] - sa * off_ref[1]
    sin = sa * off_ref[2] + ca * off_ref[3]
    n_blocks = AB_IN_W // IN_CB
    att_q_block = (2 * RET_QK_W + 2 * RET_V_W) // IN_CB
    for j in range(n_blocks):
        z = _dot(hn, w_ref[:, j * IN_CB:(j + 1) * IN_CB])
        if j < 2 * RET_QK_W // IN_CB:
            parts = []
            for h in range(IN_CB // RET_QK_DIM):
                xh = z[:, h * RET_QK_DIM:(h + 1) * RET_QK_DIM]
                parts.append(xh * cos + pltpu.roll(xh, RET_QK_DIM // 2, axis=1) * sin)
            z = jnp.concatenate(parts, axis=1)
        elif j == att_q_block:
            z = z * (ATT_HEAD_DIM ** -0.5 * LOG2E)
        o_ref[0, :, j * IN_CB:(j + 1) * IN_CB] = z.astype(BF16)


def _rotary_tables(T, tm):
    half = RET_QK_DIM // 2
    inv = 1.0 / (np.float32(10000.0) ** np.linspace(0.0, 1.0, half, dtype=np.float32))
    inv = np.concatenate([inv, inv]).astype(np.float64)
    sign = np.concatenate([-np.ones(half), np.ones(half)])
    base = (np.arange(T // tm) * tm)[:, None] * inv[None, :]
    off = np.arange(tm)[:, None] * inv[None, :]
    bases = np.stack([np.cos(base), np.sin(base)], axis=1)
    offs = np.stack([np.cos(off), np.sin(off), sign * np.cos(off), sign * np.sin(off)])
    return jnp.asarray(bases, F32), jnp.asarray(offs, F32)


def _inproj(x, g, w_in):
    B, T, D = x.shape
    tm = TM_PROJ
    bases, offs = _rotary_tables(T, tm)
    return pl.pallas_call(
        _inproj_kernel,
        grid=(B, T // tm),
        in_specs=[
            pl.BlockSpec((1, tm, D), lambda b, t: (b, t, 0)),
            _const_spec((1, D)),
            _const_spec((D, AB_IN_W)),
            pl.BlockSpec((1, 2, RET_QK_DIM), lambda b, t: (t, 0, 0)),
            _const_spec(offs.shape),
        ],
        out_specs=pl.BlockSpec((1, tm, AB_IN_W), lambda b, t: (b, t, 0)),
        out_shape=jax.ShapeDtypeStruct((B, T, AB_IN_W), BF16),
        compiler_params=_params(2),
        name="ab_inproj",
    )(x, g, w_in, bases, offs)


def _ret_tables(tr):
    h = np.arange(RET_HEADS, dtype=np.float64)
    log_g = np.log1p(-np.exp2(-5.0 - h))
    n = np.arange(tr)
    cn, cm = n[:, None] // CHUNK, n[None, :] // CHUNK
    diff = (n[:, None] - n[None, :]).astype(np.float64)
    expo = np.where(cn == cm, np.abs(diff), diff)
    dmat = np.where(cm <= cn, np.exp(log_g[:, None, None] * expo), 0.0)
    scale = RET_QK_DIM ** -0.5
    qdec = np.exp(log_g[:, None] * (n + 1.0)[None, :])
    kdec = np.exp(log_g[:, None] * (tr - 1.0 - n)[None, :]) * scale
    qdec = np.broadcast_to(qdec[:, :, None], (RET_HEADS, tr, RET_QK_DIM))
    kdec = np.broadcast_to(kdec[:, :, None], (RET_HEADS, tr, RET_QK_DIM))
    tile_dec = [float(v) for v in np.exp(log_g * tr)]
    return (jnp.asarray(dmat * scale, F32), jnp.asarray(qdec, F32), jnp.asarray(kdec, F32),
            tile_dec)


def _ret_tile(tile_dec, rows, q_ref, k_ref, v_ref, g_ref, d_ref, qd_ref, kd_ref, o_ref, state_ref):
    for h in range(RET_HEADS):
        qk = slice(h * RET_QK_DIM, (h + 1) * RET_QK_DIM)
        vv = slice(h * RET_V_DIM, (h + 1) * RET_V_DIM)
        q = q_ref[0, rows, qk]
        k = k_ref[0, rows, qk]
        v = v_ref[0, rows, vv]
        state = state_ref[h]
        s = _dot_nt(q, k) * d_ref[h]
        qd = (q.astype(F32) * qd_ref[h]).astype(BF16)
        kd = (k.astype(F32) * kd_ref[h]).astype(BF16)
        r = _dot(s.astype(BF16), v) + _dot(qd, state.astype(BF16))
        state_ref[h] = state * tile_dec[h] + _dot_tn(kd, v)
        mu = jnp.mean(r, axis=-1, keepdims=True)
        rc = r - mu
        var = jnp.mean(rc * rc, axis=-1, keepdims=True)
        rn = rc * lax.rsqrt(var + EPS)
        gate = g_ref[0, rows, vv].astype(F32)
        o_ref[0, rows, vv] = (0.5 * gate * (1.0 + jnp.tanh(0.5 * gate)) * rn).astype(BF16)


def _att_bias_vectors(rel_bias):
    hist = ATT_PAST_CHUNKS * CHUNK
    kw = QB_ATT + hist
    nr = 2 * MAX_REL + 1
    period = 2 * hist
    assert period >= kw + QB_ATT - 1 and hist > MAX_REL
    n_far = hist - MAX_REL + 1
    n_near = kw - (hist + MAX_REL)
    n_wrap = period - kw
    first, last = rel_bias[:, 0:1], rel_bias[:, nr - 1:nr]
    return jnp.concatenate([
        jnp.broadcast_to(last, (ATT_HEADS, n_far)),
        rel_bias[:, nr - 2:0:-1],
        jnp.broadcast_to(first, (ATT_HEADS, n_near)),
        jnp.broadcast_to(last, (ATT_HEADS, n_wrap))], axis=1).astype(F32)


def _mixers_kernel(tile_dec, q_ref, k_ref, v_ref, g_ref, rq_ref, rk_ref, rv_ref, rg_ref,
                   d_ref, qd_ref, kd_ref, o_ref, ro_ref, kbuf, vbuf, bias_ref, state_ref):
    t = pl.program_id(1)
    ta = TA_ATT
    hist = ATT_PAST_CHUNKS * CHUNK
    kw = QB_ATT + hist
    period = g_ref.shape[1]
    n_qb = ta // QB_ATT

    @pl.when(t == 0)
    def _():
        kbuf[0:hist, :] = jnp.zeros((hist, ATT_W), BF16)
        vbuf[0:hist, :] = jnp.zeros((hist, ATT_W), BF16)
        state_ref[...] = jnp.zeros_like(state_ref)

    @pl.when(t <= 1)
    def _():
        ci = lax.broadcasted_iota(jnp.int32, (QB_ATT, kw), 0) // CHUNK
        col = lax.broadcasted_iota(jnp.int32, (QB_ATT, kw), 1)
        cj = col // CHUNK
        in_band = jnp.logical_and(cj >= ci, cj <= ci + ATT_PAST_CHUNKS)
        for qb in range(n_qb):
            valid = jnp.logical_and(in_band, col >= hist - qb * QB_ATT - t * ta)
            for h in range(ATT_HEADS):
                rows = jnp.broadcast_to(g_ref[h:h + 1, :], (QB_ATT, period))
                skew = pltpu.roll(rows, 0, axis=1, stride=1, stride_axis=0)
                bias_ref[qb, h] = jnp.where(valid, skew[:, :kw] * LOG2E, NEG_INF)

    kbuf[hist:hist + ta, :] = k_ref[0]
    vbuf[hist:hist + ta, :] = v_ref[0]

    lane = lax.broadcasted_iota(jnp.int32, (1, 2 * ATT_HEAD_DIM), 1)
    low = lane < ATT_HEAD_DIM
    for qb in range(n_qb):
        rows = slice(qb * QB_ATT, (qb + 1) * QB_ATT)
        krows = slice(qb * QB_ATT, qb * QB_ATT + kw)
        for hp in range(ATT_HEADS // 2):
            lanes = slice(hp * 2 * ATT_HEAD_DIM, (hp + 1) * 2 * ATT_HEAD_DIM)
            qp = q_ref[0, rows, lanes]
            kp = kbuf[krows, lanes]
            vp = vbuf[krows, lanes]
            outs = []
            for e in range(2):
                keep = low if e == 0 else jnp.logical_not(low)
                qm = jnp.where(keep, qp, jnp.zeros_like(qp))
                s = _dot_nt(qm, kp) + bias_ref[qb, 2 * hp + e]
                m = jnp.max(s, axis=-1, keepdims=True)
                p = jnp.exp2(s - m).astype(BF16)
                ov = _dot(p, jnp.where(keep, vp, jnp.ones_like(vp)))
                outs.append(ov / pltpu.roll(ov, ATT_HEAD_DIM, axis=1))
            o_ref[0, rows, lanes] = jnp.where(low, outs[0], outs[1]).astype(BF16)
        for r0 in range(qb * QB_ATT, (qb + 1) * QB_ATT, TR_RET):
            _ret_tile(tile_dec, slice(r0, r0 + TR_RET), rq_ref, rk_ref, rv_ref, rg_ref,
                      d_ref, qd_ref, kd_ref, ro_ref, state_ref)

    kbuf[0:hist, :] = kbuf[ta:ta + hist, :]
    vbuf[0:hist, :] = vbuf[ta:ta + hist, :]


def _mixers(z, rel_bias):
    B, T, _ = z.shape
    ta = TA_ATT
    hist = ATT_PAST_CHUNKS * CHUNK
    assert ta == hist and T // ta >= 2 and QB_ATT % TR_RET == 0
    q_blk = (2 * RET_QK_W + 2 * RET_V_W) // ATT_W
    g = _att_bias_vectors(rel_bias)
    dmat, qdec, kdec, tile_dec = _ret_tables(TR_RET)
    zcols = lambda width, blk: pl.BlockSpec((1, ta, width), lambda b, t: (b, t, blk))
    return pl.pallas_call(
        functools.partial(_mixers_kernel, tile_dec),
        grid=(B, T // ta),
        in_specs=[
            zcols(ATT_W, q_blk), zcols(ATT_W, q_blk + 1), zcols(ATT_W, q_blk + 2),
            _const_spec(g.shape),
            zcols(RET_QK_W, 0), zcols(RET_QK_W, 1), zcols(RET_V_W, 1), zcols(RET_V_W, 2),
            _const_spec(dmat.shape), _const_spec(qdec.shape), _const_spec(kdec.shape),
        ],
        out_specs=[zcols(ATT_W, 0), zcols(RET_V_W, 0)],
        out_shape=[jax.ShapeDtypeStruct((B, T, ATT_W), BF16),
                   jax.ShapeDtypeStruct((B, T, RET_V_W), BF16)],
        scratch_shapes=[pltpu.VMEM((hist + ta, ATT_W), BF16),
                        pltpu.VMEM((hist + ta, ATT_W), BF16),
                        pltpu.VMEM((ta // QB_ATT, ATT_HEADS, QB_ATT, QB_ATT + hist), F32),
                        pltpu.VMEM((RET_HEADS, RET_QK_DIM, RET_V_DIM), F32)],
        compiler_params=_params(2),
        name="mixers",
    )(z, z, z, g, z, z, z, z, dmat, qdec, kdec)


def _ffn_kernel(final_norm, mixer_proj, *refs):
    if mixer_proj:
        x_ref, ya_ref, yb_ref, wmix_ref = refs[:4]
        refs = refs[4:]
    else:
        x_ref = refs[0]
        refs = refs[1:]
    (g_ref, wup_ref, cw_ref, cb_ref, wdn_ref, fg_ref, o_ref,
     hn_ext, zs_ref, act_ref, ys_ref) = refs
    tm = x_ref.shape[1]
    pad = BF16_SUBLANES
    n_slab = FFN_CB // LANES
    half_rows = tm // 2
    n_lane_slabs = D_MODEL // LANES

    @pl.when(pl.program_id(1) == 0)
    def _():
        hn_ext[0:pad, :] = jnp.zeros((pad, D_MODEL), BF16)

    if mixer_proj:
        na = ya_ref.shape[2]
        mix = _dot(ya_ref[0], wmix_ref[0:na, :]) + _dot(yb_ref[0], wmix_ref[na:, :])
        for l in range(n_lane_slabs):
            ys_ref[l] = mix[:, l * LANES:(l + 1) * LANES]
    for r0 in range(0, tm, NORM_ROWS):
        rows = slice(r0, r0 + NORM_ROWS)
        xs = x_ref[0, rows, :]
        if mixer_proj:
            xs = xs + jnp.concatenate([ys_ref[l, rows, :] for l in range(n_lane_slabs)], axis=1)
            o_ref[0, rows, :] = xs
        hn_ext[pad + r0:pad + r0 + NORM_ROWS, :] = _rms_norm(xs, g_ref[...]).astype(BF16)
    lhs = hn_ext[...]
    n_blocks = FFN_HIDDEN // FFN_CB

    def up_block(j):
        for half in range(2):
            c0 = half * FFN_HIDDEN + j * FFN_CB
            z = _dot(lhs, wup_ref[:, c0:c0 + FFN_CB])
            for l in range(n_slab):
                zs_ref[j % 2, half * n_slab + l] = z[:, l * LANES:(l + 1) * LANES]

    up_block(0)
    for j in range(n_blocks):
        buf = j % 2
        if j + 1 < n_blocks:
            up_block(j + 1)
        for par in range(2):
            for l in range(n_slab):
                for u0 in range(0, half_rows, CONV_ROWS):
                    conv = []
                    for half in range(2):
                        c0 = half * FFN_HIDDEN + j * FFN_CB + l * LANES
                        c = cb_ref[:, c0:c0 + LANES]
                        for tap in range(CONV_WIDTH):
                            off = pad - (CONV_WIDTH - 1) + tap + par + 2 * u0
                            c = c + (zs_ref[buf, half * n_slab + l,
                                            pl.ds(off, CONV_ROWS, stride=2), :]
                                     * cw_ref[tap:tap + 1, c0:c0 + LANES])
                        conv.append(c)
                    a0 = j * FFN_CB + l * LANES
                    act_ref[par * half_rows + u0:par * half_rows + u0 + CONV_ROWS,
                            a0:a0 + LANES] = (_gelu_tanh(conv[0]) * conv[1]).astype(BF16)
    hn_ext[0:pad, :] = hn_ext[tm:tm + pad, :]
    yp = _dot(act_ref[...], wdn_ref[...])
    for l in range(n_lane_slabs):
        for par in range(2):
            ys_ref[l, pl.ds(par, half_rows, stride=2), :] = (
                yp[par * half_rows:(par + 1) * half_rows, l * LANES:(l + 1) * LANES])
    for r0 in range(0, tm, NORM_ROWS):
        rows = slice(r0, r0 + NORM_ROWS)
        y = (o_ref if mixer_proj else x_ref)[0, rows, :] + jnp.concatenate(
            [ys_ref[l, rows, :] for l in range(n_lane_slabs)], axis=1)
        if final_norm:
            y = _rms_norm(y, fg_ref[...])
        o_ref[0, rows, :] = y


def _layer_spec(layer, shape):
    zeros = (0,) * len(shape)
    return pl.BlockSpec((None,) + tuple(shape), lambda *_: (layer,) + zeros,
                        pipeline_mode=pl.Buffered(1))


def _ffn(x, layer, g, w_up, conv_w, conv_b, w_down, final_g=None, mixer=None):
    B, T, D = x.shape
    tm = TM_FFN
    final_norm = final_g is not None
    if final_g is None:
        final_g = jnp.ones((D,), F32)
    tile = lambda width: pl.BlockSpec((1, tm, width), lambda b, t: (b, t, 0))
    mixer_args, mixer_specs = (), []
    if mixer is not None:
        ya, yb, w_mix = mixer
        mixer_args = (ya, yb, w_mix)
        mixer_specs = [tile(ya.shape[2]), tile(yb.shape[2]), _const_spec(w_mix.shape)]
    return pl.pallas_call(
        functools.partial(_ffn_kernel, final_norm, mixer is not None),
        grid=(B, T // tm),
        in_specs=[tile(D)] + mixer_specs + [
            _layer_spec(layer, (1, D)),
            _layer_spec(layer, (D, 2 * FFN_HIDDEN)),
            _layer_spec(layer, (CONV_WIDTH, 2 * FFN_HIDDEN)),
            _layer_spec(layer, (1, 2 * FFN_HIDDEN)),
            _layer_spec(layer, (FFN_HIDDEN, D)),
            _const_spec((1, D)),
        ],
        out_specs=pl.BlockSpec((1, tm, D), lambda b, t: (b, t, 0)),
        out_shape=jax.ShapeDtypeStruct((B, T, D), F32),
        scratch_shapes=[
            pltpu.VMEM((tm + BF16_SUBLANES, D), BF16),
            pltpu.VMEM((2, 2 * FFN_CB // LANES, tm + BF16_SUBLANES, LANES), F32),
            pltpu.VMEM((tm, FFN_HIDDEN), BF16),
            pltpu.VMEM((D // LANES, tm, LANES), F32),
        ],
        compiler_params=_params(2),
        name="conv_ffn_final" if final_norm else "conv_ffn",
    )(x, *mixer_args, g, w_up, conv_w, conv_b, w_down, final_g[None, :])


def _sgu_kernel(x_ref, g_ref, win_ref, lng_ref, lnb_ref, ws_ref, bs_ref, wout_ref, o_ref,
                v_ref, y_ref):
    tm = x_ref.shape[1]
    gw = SGU_WIDTH // SGU_GROUPS
    pair = 2 * SGU_BLOCK
    x = x_ref[0]
    hn = _rms_norm(x, g_ref[...]).astype(BF16)

    vsum = jnp.zeros((tm, 1), F32)
    for j in range(SGU_WIDTH // IN_CB):
        cols = slice(j * IN_CB, (j + 1) * IN_CB)
        zv = _gelu_tanh(_dot(hn, win_ref[:, SGU_WIDTH + j * IN_CB:SGU_WIDTH + (j + 1) * IN_CB]))
        v_ref[:, cols] = zv
        vsum = vsum + jnp.sum(zv, axis=-1, keepdims=True)
    mu = vsum * (1.0 / SGU_WIDTH)
    vsq = jnp.zeros((tm, 1), F32)
    for j in range(SGU_WIDTH // IN_CB):
        cols = slice(j * IN_CB, (j + 1) * IN_CB)
        vc = v_ref[:, cols] - mu
        vsq = vsq + jnp.sum(vc * vc, axis=-1, keepdims=True)
    rstd = lax.rsqrt(vsq * (1.0 / SGU_WIDTH) + EPS)

    ri = lax.broadcasted_iota(jnp.int32, (pair, pair), 0)
    ci = lax.broadcasted_iota(jnp.int32, (pair, pair), 1)
    allowed = (ci // CHUNK) <= (ri // CHUNK)
    same = (ci // SGU_BLOCK) == (ri // SGU_BLOCK)
    keep = jnp.logical_and(allowed, same)
    for gidx in range(SGU_GROUPS):
        cols = slice(gidx * gw, (gidx + 1) * gw)
        w = jnp.where(keep, ws_ref[gidx], jnp.zeros((pair, pair), BF16))
        vn = ((v_ref[:, cols] - mu) * rstd * lng_ref[:, cols] + lnb_ref[:, cols]).astype(BF16)
        u = _gelu_tanh(_dot(hn, win_ref[:, cols]))
        for rp in range(tm // pair):
            rows = slice(rp * pair, (rp + 1) * pair)
            mixed = _dot(w, vn[rows]) + bs_ref[:, cols]
            y_ref[rows, cols] = (u[rows] * mixed).astype(BF16)
    o_ref[0] = x + _dot(y_ref[...], wout_ref[...])


def _sgu(x, g, w_in, ln_g, ln_b, w_s, b_s, w_out):
    B, T, D = x.shape
    tm = TM_PROJ
    gw = SGU_WIDTH // SGU_GROUPS
    pair = 2 * SGU_BLOCK
    zero = jnp.zeros_like(w_s)
    w_bd = jnp.concatenate([jnp.concatenate([w_s, zero], axis=2),
                            jnp.concatenate([zero, w_s], axis=2)], axis=1).astype(BF16)
    b_tab = jnp.repeat(b_s.T, gw, axis=1)
    b_tab = jnp.concatenate([b_tab, b_tab], axis=0)
    return pl.pallas_call(
        _sgu_kernel,
        grid=(B, T // tm),
        in_specs=[
            pl.BlockSpec((1, tm, D), lambda b, t: (b, t, 0)),
            _const_spec((1, D)),
            _const_spec((D, 2 * SGU_WIDTH)),
            _const_spec((1, SGU_WIDTH)),
            _const_spec((1, SGU_WIDTH)),
            _const_spec((SGU_GROUPS, pair, pair)),
            _const_spec((pair, SGU_WIDTH)),
            _const_spec((SGU_WIDTH, D)),
        ],
        out_specs=pl.BlockSpec((1, tm, D), lambda b, t: (b, t, 0)),
        out_shape=jax.ShapeDtypeStruct((B, T, D), F32),
        scratch_shapes=[
            pltpu.VMEM((tm, SGU_WIDTH), F32),
            pltpu.VMEM((tm, SGU_WIDTH), BF16),
        ],
        compiler_params=_params(2),
        name="sgu_mixer",
    )(x, g[None, :], w_in.astype(BF16), ln_g[None, :], ln_b[None, :], w_bd, b_tab,
      w_out.astype(BF16))


def kernel(x, attn_norm_g, ffn_norm_g, ab_w_in, ab_w_out, ab_rel_bias, c_w_in, c_ln_g, c_ln_b,
           c_w_s, c_b_s, c_w_out, ffn_w_up, ffn_conv_w, ffn_conv_b, ffn_w_down, final_norm_g):
    ffn_params = (ffn_norm_g[:, None, :], ffn_w_up.astype(BF16), ffn_conv_w,
                  ffn_conv_b[:, None, :], ffn_w_down.astype(BF16))
    z = _inproj(x, attn_norm_g[0][None, :], ab_w_in[0].astype(BF16))
    yb, ya = _mixers(z, ab_rel_bias[0])
    h = _ffn(x, 0, *ffn_params, mixer=(ya, yb, ab_w_out[0].astype(BF16)))
    h = _sgu(h, attn_norm_g[1], c_w_in[0], c_ln_g[0], c_ln_b[0], c_w_s[0], c_b_s[0], c_w_out[0])
    h = _ffn(h, 1, *ffn_params, final_g=final_norm_g)
    return h
```

```python
import functools
import math

import numpy as np
import jax
import jax.numpy as jnp
from jax import lax
from jax.experimental import pallas as pl
from jax.experimental.pallas import tpu as pltpu

F32 = jnp.float32
BF16 = jnp.bfloat16

D_MODEL = 1024
CHUNK = 64
EPS = 1e-6
NEG_INF = -1e30
LOG2E = math.log2(math.e)

RET_HEADS = 4
RET_QK_DIM = 128
RET_V_DIM = 256
ATT_HEADS = 8
ATT_HEAD_DIM = 64
ATT_PAST_CHUNKS = 8
MAX_REL = 128
SGU_BLOCK = 128
SGU_GROUPS = 8
SGU_WIDTH = 2048
FFN_HIDDEN = 2816
CONV_WIDTH = 3

RET_QK_W = RET_HEADS * RET_QK_DIM
RET_V_W = RET_HEADS * RET_V_DIM
ATT_W = ATT_HEADS * ATT_HEAD_DIM
AB_IN_W = 2 * RET_QK_W + 2 * RET_V_W + 3 * ATT_W
AB_OUT_W = RET_V_W + ATT_W
Z_RET_Q = 0
Z_RET_K = Z_RET_Q + RET_QK_W
Z_RET_V = Z_RET_K + RET_QK_W
Z_RET_G = Z_RET_V + RET_V_W
Z_ATT_Q = Z_RET_G + RET_V_W
Z_ATT_K = Z_ATT_Q + ATT_W
Z_ATT_V = Z_ATT_K + ATT_W

VMEM_LIMIT_BYTES = 56 * 1024 * 1024
BF16_SUBLANES = 16
LANES = 128

TM_PROJ = 512
TM_FFN = 512
TR_RET = 256
TA_ATT = 512
QB_ATT = 256
FFN_CB = 256
IN_CB = 256
SGU_CB = 256
NORM_ROWS = 32
CONV_ROWS = 64


def _const_spec(shape):
    zeros = (0,) * len(shape)
    return pl.BlockSpec(shape, lambda *_: zeros, pipeline_mode=pl.Buffered(1))


def _params(n_axes, flags=None):
    return pltpu.CompilerParams(
        dimension_semantics=("arbitrary",) * n_axes,
        vmem_limit_bytes=VMEM_LIMIT_BYTES,
        flags=flags)


def _rms_norm(x, g):
    return x * lax.rsqrt(jnp.mean(x * x, axis=-1, keepdims=True) + EPS) * g


def _gelu_tanh(x):
    c = math.sqrt(2.0 / math.pi)
    return 0.5 * x * (1.0 + jnp.tanh(c * (x + 0.044715 * (x * x * x))))


def _dot(a, b):
    return jnp.dot(a, b, preferred_element_type=F32)


def _dot_nt(a, b):
    return lax.dot_general(a, b, (((1,), (1,)), ((), ())), preferred_element_type=F32)


def _dot_tn(a, b):
    return lax.dot_general(a, b, (((0,), (0,)), ((), ())), preferred_element_type=F32)


def _inproj_blocks(x_ref, g_ref, w_ref, base_ref, off_ref, zput):
    hn = _rms_norm(x_ref[0], g_ref[...]).astype(BF16)
    ca, sa = base_ref[0, 0:1, :], base_ref[0, 1:2, :]
    cos = ca * off_ref[0] - sa * off_ref[1]
    sin = sa * off_ref[2] + ca * off_ref[3]

    def block(j):
        z = _dot(hn, w_ref[:, j * IN_CB:(j + 1) * IN_CB])
        if j * IN_CB < Z_RET_V:
            parts = []
            for h in range(IN_CB // RET_QK_DIM):
                xh = z[:, h * RET_QK_DIM:(h + 1) * RET_QK_DIM]
                parts.append(xh * cos + pltpu.roll(xh, RET_QK_DIM // 2, axis=1) * sin)
            z = jnp.concatenate(parts, axis=1)
        elif Z_ATT_Q <= j * IN_CB < Z_ATT_K:
            z = z * (ATT_HEAD_DIM ** -0.5 * LOG2E)
        zput(j * IN_CB, z.astype(BF16))

    return [functools.partial(block, j) for j in range(AB_IN_W // IN_CB)]


def _rotary_tables(T, tm):
    half = RET_QK_DIM // 2
    inv = 1.0 / (np.float32(10000.0) ** np.linspace(0.0, 1.0, half, dtype=np.float32))
    inv = np.concatenate([inv, inv]).astype(np.float64)
    sign = np.concatenate([-np.ones(half), np.ones(half)])
    base = (np.arange(T // tm) * tm)[:, None] * inv[None, :]
    off = np.arange(tm)[:, None] * inv[None, :]
    bases = np.stack([np.cos(base), np.sin(base)], axis=1)
    offs = np.stack([np.cos(off), np.sin(off), sign * np.cos(off), sign * np.sin(off)])
    return jnp.asarray(bases, F32), jnp.asarray(offs, F32)


def _ret_tables(tr):
    h = np.arange(RET_HEADS, dtype=np.float64)
    log_g = np.log1p(-np.exp2(-5.0 - h))
    n = np.arange(tr)
    cn, cm = n[:, None] // CHUNK, n[None, :] // CHUNK
    diff = (n[:, None] - n[None, :]).astype(np.float64)
    expo = np.where(cn == cm, np.abs(diff), diff)
    dmat = np.where(cm <= cn, np.exp(log_g[:, None, None] * expo), 0.0)
    scale = RET_QK_DIM ** -0.5
    qdec = np.exp(log_g[:, None] * (n + 1.0)[None, :])
    kdec = np.exp(log_g[:, None] * (tr - 1.0 - n)[None, :]) * scale
    qdec = np.broadcast_to(qdec[:, :, None], (RET_HEADS, tr, RET_QK_DIM))
    kdec = np.broadcast_to(kdec[:, :, None], (RET_HEADS, tr, RET_QK_DIM))
    tile_dec = [float(v) for v in np.exp(log_g * tr)]
    return (jnp.asarray(dmat * scale, F32), jnp.asarray(qdec, F32), jnp.asarray(kdec, F32),
            tile_dec)


def _ret_tile(tile_dec, rows, zget, d_ref, qd_ref, kd_ref, o_ref, state_ref):
    for h in range(RET_HEADS):
        vv = slice(h * RET_V_DIM, (h + 1) * RET_V_DIM)
        q = zget(rows, Z_RET_Q + h * RET_QK_DIM, RET_QK_DIM)
        k = zget(rows, Z_RET_K + h * RET_QK_DIM, RET_QK_DIM)
        v = zget(rows, Z_RET_V + h * RET_V_DIM, RET_V_DIM)
        state = state_ref[h]
        s = _dot_nt(q, k) * d_ref[h]
        qd = (q.astype(F32) * qd_ref[h]).astype(BF16)
        kd = (k.astype(F32) * kd_ref[h]).astype(BF16)
        r = _dot(s.astype(BF16), v) + _dot(qd, state.astype(BF16))
        state_ref[h] = state * tile_dec[h] + _dot_tn(kd, v)
        mu = jnp.mean(r, axis=-1, keepdims=True)
        rc = r - mu
        var = jnp.mean(rc * rc, axis=-1, keepdims=True)
        rn = rc * lax.rsqrt(var + EPS)
        gate = zget(rows, Z_RET_G + h * RET_V_DIM, RET_V_DIM).astype(F32)
        o_ref[0, rows, vv] = (0.5 * gate * (1.0 + jnp.tanh(0.5 * gate)) * rn).astype(BF16)


def _att_bias_vectors(rel_bias):
    hist = ATT_PAST_CHUNKS * CHUNK
    kw = QB_ATT + hist
    nr = 2 * MAX_REL + 1
    period = 2 * hist
    assert period >= kw + QB_ATT - 1 and hist > MAX_REL
    n_far = hist - MAX_REL + 1
    n_near = kw - (hist + MAX_REL)
    n_wrap = period - kw
    first, last = rel_bias[:, 0:1], rel_bias[:, nr - 1:nr]
    return jnp.concatenate([
        jnp.broadcast_to(last, (ATT_HEADS, n_far)),
        rel_bias[:, nr - 2:0:-1],
        jnp.broadcast_to(first, (ATT_HEADS, n_near)),
        jnp.broadcast_to(last, (ATT_HEADS, n_wrap))], axis=1).astype(F32)


def _layer0_kernel(tile_dec, x_ref, ng_ref, w_ref, base_ref, off_ref, g_ref, d_ref, qd_ref,
                   kd_ref, o_ref, ro_ref, zbuf0, zbuf1, kbuf, vbuf, bias_ref, state_ref):
    t = pl.program_id(1)
    ta = TA_ATT
    hist = ATT_PAST_CHUNKS * CHUNK
    kw = QB_ATT + hist
    period = g_ref.shape[1]
    n_qb = ta // QB_ATT
    tile = jnp.maximum(t - 1, 0)

    @pl.when(t == 0)
    def _():
        zbuf1[...] = jnp.zeros(zbuf1.shape, BF16)

    @pl.when(t <= 1)
    def _():
        kbuf[0:hist, :] = jnp.zeros((hist, ATT_W), BF16)
        vbuf[0:hist, :] = jnp.zeros((hist, ATT_W), BF16)
        state_ref[...] = jnp.zeros_like(state_ref)

    @pl.when(t <= 2)
    def _():
        ci = lax.broadcasted_iota(jnp.int32, (QB_ATT, kw), 0) // CHUNK
        col = lax.broadcasted_iota(jnp.int32, (QB_ATT, kw), 1)
        cj = col // CHUNK
        in_band = jnp.logical_and(cj >= ci, cj <= ci + ATT_PAST_CHUNKS)
        for qb in range(n_qb):
            valid = jnp.logical_and(in_band, col >= hist - qb * QB_ATT - tile * ta)
            for h in range(ATT_HEADS):
                rows = jnp.broadcast_to(g_ref[h:h + 1, :], (QB_ATT, period))
                skew = pltpu.roll(rows, 0, axis=1, stride=1, stride_axis=0)
                bias_ref[qb, h] = jnp.where(valid, skew[:, :kw] * LOG2E, NEG_INF)

    def step(wbuf, rbuf):
        def zput(c0, block):
            wbuf[:, c0:c0 + block.shape[1]] = block

        def zget(rows, c0, width):
            return rbuf[rows, c0:c0 + width]

        pending = _inproj_blocks(x_ref, ng_ref, w_ref, base_ref, off_ref, zput)

        def issue_projection_block():
            if pending:
                pending.pop(0)()

        kbuf[hist:hist + ta, :] = zget(slice(0, ta), Z_ATT_K, ATT_W)
        vbuf[hist:hist + ta, :] = zget(slice(0, ta), Z_ATT_V, ATT_W)

        lane = lax.broadcasted_iota(jnp.int32, (1, 2 * ATT_HEAD_DIM), 1)
        low = lane < ATT_HEAD_DIM
        for qb in range(n_qb):
            rows = slice(qb * QB_ATT, (qb + 1) * QB_ATT)
            krows = slice(qb * QB_ATT, qb * QB_ATT + kw)
            for hp in range(ATT_HEADS // 2):
                lanes = slice(hp * 2 * ATT_HEAD_DIM, (hp + 1) * 2 * ATT_HEAD_DIM)
                qp = zget(rows, Z_ATT_Q + lanes.start, 2 * ATT_HEAD_DIM)
                kp = kbuf[krows, lanes]
                vp = vbuf[krows, lanes]
                outs = []
                for e in range(2):
                    keep = low if e == 0 else jnp.logical_not(low)
                    qm = jnp.where(keep, qp, jnp.zeros_like(qp))
                    s = _dot_nt(qm, kp)
                    issue_projection_block()
                    s = s + bias_ref[qb, 2 * hp + e]
                    m = jnp.max(s, axis=-1, keepdims=True)
                    p = jnp.exp2(s - m).astype(BF16)
                    ov = _dot(p, jnp.where(keep, vp, jnp.ones_like(vp)))
                    outs.append(ov / pltpu.roll(ov, ATT_HEAD_DIM, axis=1))
                o_ref[0, rows, lanes] = jnp.where(low, outs[0], outs[1]).astype(BF16)
            for r0 in range(qb * QB_ATT, (qb + 1) * QB_ATT, TR_RET):
                _ret_tile(tile_dec, slice(r0, r0 + TR_RET), zget, d_ref, qd_ref, kd_ref,
                          ro_ref, state_ref)
        while pending:
            issue_projection_block()

        kbuf[0:hist, :] = kbuf[ta:ta + hist, :]
        vbuf[0:hist, :] = vbuf[ta:ta + hist, :]

    @pl.when(t % 2 == 0)
    def _():
        step(zbuf0, zbuf1)

    @pl.when(t % 2 == 1)
    def _():
        step(zbuf1, zbuf0)


def _layer0_mixers(x, norm_g, w_in, rel_bias):
    B, T, D = x.shape
    ta = TA_ATT
    n_t = T // ta
    hist = ATT_PAST_CHUNKS * CHUNK
    assert ta == hist and n_t >= 2 and QB_ATT % TR_RET == 0 and AB_IN_W % IN_CB == 0
    g = _att_bias_vectors(rel_bias)
    dmat, qdec, kdec, tile_dec = _ret_tables(TR_RET)
    bases, offs = _rotary_tables(T, ta)
    proj_tile = lambda b, t: (b, jnp.minimum(t, n_t - 1), 0)
    mix_tile = lambda b, t: (b, jnp.maximum(t - 1, 0), 0)
    return pl.pallas_call(
        functools.partial(_layer0_kernel, tile_dec),
        grid=(B, n_t + 1),
        in_specs=[
            pl.BlockSpec((1, ta, D), proj_tile),
            _const_spec((1, D)),
            _const_spec((D, AB_IN_W)),
            pl.BlockSpec((1, 2, RET_QK_DIM), lambda b, t: (jnp.minimum(t, n_t - 1), 0, 0)),
            _const_spec(offs.shape),
            _const_spec(g.shape),
            _const_spec(dmat.shape), _const_spec(qdec.shape), _const_spec(kdec.shape),
        ],
        out_specs=[pl.BlockSpec((1, ta, ATT_W), mix_tile),
                   pl.BlockSpec((1, ta, RET_V_W), mix_tile)],
        out_shape=[jax.ShapeDtypeStruct((B, T, ATT_W), BF16),
                   jax.ShapeDtypeStruct((B, T, RET_V_W), BF16)],
        scratch_shapes=[pltpu.VMEM((ta, AB_IN_W), BF16),
                        pltpu.VMEM((ta, AB_IN_W), BF16),
                        pltpu.VMEM((hist + ta, ATT_W), BF16),
                        pltpu.VMEM((hist + ta, ATT_W), BF16),
                        pltpu.VMEM((ta // QB_ATT, ATT_HEADS, QB_ATT, QB_ATT + hist), F32),
                        pltpu.VMEM((RET_HEADS, RET_QK_DIM, RET_V_DIM), F32)],
        compiler_params=_params(2),
        name="layer0_mixers",
    )(x, norm_g, w_in, bases, offs, g, dmat, qdec, kdec)


def _ffn_kernel(final_norm, mixer_proj, *refs):
    if mixer_proj:
        x_ref, ya_ref, yb_ref, wmix_ref = refs[:4]
        refs = refs[4:]
    else:
        x_ref = refs[0]
        refs = refs[1:]
    (g_ref, wup_ref, cw_ref, cb_ref, wdn_ref, fg_ref, o_ref,
     hn_ext, zs_ref, act_ref, ys_ref) = refs
    tm = x_ref.shape[1]
    pad = BF16_SUBLANES
    n_slab = FFN_CB // LANES
    half_rows = tm // 2
    n_lane_slabs = D_MODEL // LANES

    @pl.when(pl.program_id(1) == 0)
    def _():
        hn_ext[0:pad, :] = jnp.zeros((pad, D_MODEL), BF16)

    if mixer_proj:
        na = ya_ref.shape[2]
        mix = _dot(ya_ref[0], wmix_ref[0:na, :]) + _dot(yb_ref[0], wmix_ref[na:, :])
        for l in range(n_lane_slabs):
            ys_ref[l] = mix[:, l * LANES:(l + 1) * LANES]
    for r0 in range(0, tm, NORM_ROWS):
        rows = slice(r0, r0 + NORM_ROWS)
        xs = x_ref[0, rows, :]
        if mixer_proj:
            xs = xs + jnp.concatenate([ys_ref[l, rows, :] for l in range(n_lane_slabs)], axis=1)
            o_ref[0, rows, :] = xs
        hn_ext[pad + r0:pad + r0 + NORM_ROWS, :] = _rms_norm(xs, g_ref[...]).astype(BF16)
    lhs = hn_ext[...]
    n_blocks = FFN_HIDDEN // FFN_CB

    def up_block(j):
        for half in range(2):
            c0 = half * FFN_HIDDEN + j * FFN_CB
            z = _dot(lhs, wup_ref[:, c0:c0 + FFN_CB])
            for l in range(n_slab):
                zs_ref[j % 2, half * n_slab + l] = z[:, l * LANES:(l + 1) * LANES]

    up_block(0)
    for j in range(n_blocks):
        buf = j % 2
        if j + 1 < n_blocks:
            up_block(j + 1)
        for par in range(2):
            for l in range(n_slab):
                for u0 in range(0, half_rows, CONV_ROWS):
                    conv = []
                    for half in range(2):
                        c0 = half * FFN_HIDDEN + j * FFN_CB + l * LANES
                        c = cb_ref[:, c0:c0 + LANES]
                        for tap in range(CONV_WIDTH):
                            off = pad - (CONV_WIDTH - 1) + tap + par + 2 * u0
                            c = c + (zs_ref[buf, half * n_slab + l,
                                            pl.ds(off, CONV_ROWS, stride=2), :]
                                     * cw_ref[tap:tap + 1, c0:c0 + LANES])
                        conv.append(c)
                    a0 = j * FFN_CB + l * LANES
                    act_ref[par * half_rows + u0:par * half_rows + u0 + CONV_ROWS,
                            a0:a0 + LANES] = (_gelu_tanh(conv[0]) * conv[1]).astype(BF16)
    hn_ext[0:pad, :] = hn_ext[tm:tm + pad, :]
    yp = _dot(act_ref[...], wdn_ref[...])
    for l in range(n_lane_slabs):
        for par in range(2):
            ys_ref[l, pl.ds(par, half_rows, stride=2), :] = (
                yp[par * half_rows:(par + 1) * half_rows, l * LANES:(l + 1) * LANES])
    for r0 in range(0, tm, NORM_ROWS):
        rows = slice(r0, r0 + NORM_ROWS)
        y = (o_ref if mixer_proj else x_ref)[0, rows, :] + jnp.concatenate(
            [ys_ref[l, rows, :] for l in range(n_lane_slabs)], axis=1)
        if final_norm:
            y = _rms_norm(y, fg_ref[...])
        o_ref[0, rows, :] = y


def _layer_spec(layer, shape):
    zeros = (0,) * len(shape)
    return pl.BlockSpec((None,) + tuple(shape), lambda *_: (layer,) + zeros,
                        pipeline_mode=pl.Buffered(1))


def _ffn(x, layer, g, w_up, conv_w, conv_b, w_down, final_g=None, mixer=None):
    B, T, D = x.shape
    tm = TM_FFN
    final_norm = final_g is not None
    if final_g is None:
        final_g = jnp.ones((D,), F32)
    tile = lambda width: pl.BlockSpec((1, tm, width), lambda b, t: (b, t, 0))
    mixer_args, mixer_specs = (), []
    if mixer is not None:
        ya, yb, w_mix = mixer
        mixer_args = (ya, yb, w_mix)
        mixer_specs = [tile(ya.shape[2]), tile(yb.shape[2]), _const_spec(w_mix.shape)]
    return pl.pallas_call(
        functools.partial(_ffn_kernel, final_norm, mixer is not None),
        grid=(B, T // tm),
        in_specs=[tile(D)] + mixer_specs + [
            _layer_spec(layer, (1, D)),
            _layer_spec(layer, (D, 2 * FFN_HIDDEN)),
            _layer_spec(layer, (CONV_WIDTH, 2 * FFN_HIDDEN)),
            _layer_spec(layer, (1, 2 * FFN_HIDDEN)),
            _layer_spec(layer, (FFN_HIDDEN, D)),
            _const_spec((1, D)),
        ],
        out_specs=pl.BlockSpec((1, tm, D), lambda b, t: (b, t, 0)),
        out_shape=jax.ShapeDtypeStruct((B, T, D), F32),
        scratch_shapes=[
            pltpu.VMEM((tm + BF16_SUBLANES, D), BF16),
            pltpu.VMEM((2, 2 * FFN_CB // LANES, tm + BF16_SUBLANES, LANES), F32),
            pltpu.VMEM((tm, FFN_HIDDEN), BF16),
            pltpu.VMEM((D // LANES, tm, LANES), F32),
        ],
        compiler_params=_params(2),
        name="conv_ffn_final" if final_norm else "conv_ffn",
    )(x, *mixer_args, g, w_up, conv_w, conv_b, w_down, final_g[None, :])


def _sgu_kernel(x_ref, g_ref, win_ref, lng_ref, lnb_ref, ws_ref, bs_ref, wout_ref, o_ref,
                hn_ref, v_ref, u_ref, y_ref):
    tm = x_ref.shape[1]
    gw = SGU_WIDTH // SGU_GROUPS
    pair = 2 * SGU_BLOCK
    for r0 in range(0, tm, NORM_ROWS):
        rows = slice(r0, r0 + NORM_ROWS)
        hn_ref[rows, :] = _rms_norm(x_ref[0, rows, :], g_ref[...]).astype(BF16)
    hn = hn_ref[...]

    vsum = jnp.zeros((tm, 1), F32)
    for j in range(SGU_WIDTH // SGU_CB):
        cols = slice(j * SGU_CB, (j + 1) * SGU_CB)
        zv = _gelu_tanh(_dot(hn, win_ref[:, SGU_WIDTH + j * SGU_CB:SGU_WIDTH + (j + 1) * SGU_CB]))
        v_ref[:, cols] = zv
        vsum = vsum + jnp.sum(zv, axis=-1, keepdims=True)
    for j in range(SGU_WIDTH // SGU_CB):
        cols = slice(j * SGU_CB, (j + 1) * SGU_CB)
        u_ref[:, cols] = _gelu_tanh(_dot(hn, win_ref[:, cols]))
    mu = vsum * (1.0 / SGU_WIDTH)
    vsq = jnp.zeros((tm, 1), F32)
    for j in range(SGU_WIDTH // SGU_CB):
        cols = slice(j * SGU_CB, (j + 1) * SGU_CB)
        vc = v_ref[:, cols] - mu
        vsq = vsq + jnp.sum(vc * vc, axis=-1, keepdims=True)
    rstd = lax.rsqrt(vsq * (1.0 / SGU_WIDTH) + EPS)

    ri = lax.broadcasted_iota(jnp.int32, (pair, pair), 0)
    ci = lax.broadcasted_iota(jnp.int32, (pair, pair), 1)
    allowed = (ci // CHUNK) <= (ri // CHUNK)
    same = (ci // SGU_BLOCK) == (ri // SGU_BLOCK)
    keep = jnp.logical_and(allowed, same)
    for gidx in range(SGU_GROUPS):
        cols = slice(gidx * gw, (gidx + 1) * gw)
        w = jnp.where(keep, ws_ref[gidx], jnp.zeros((pair, pair), BF16))
        vn = ((v_ref[:, cols] - mu) * rstd * lng_ref[:, cols] + lnb_ref[:, cols]).astype(BF16)
        for rp in range(tm // pair):
            rows = slice(rp * pair, (rp + 1) * pair)
            mixed = _dot(w, vn[rows]) + bs_ref[:, cols]
            y_ref[rows, cols] = (u_ref[rows, cols] * mixed).astype(BF16)
    o_ref[0] = x_ref[0] + _dot(y_ref[...], wout_ref[...])


def _sgu(x, g, w_in, ln_g, ln_b, w_s, b_s, w_out):
    B, T, D = x.shape
    tm = TM_PROJ
    gw = SGU_WIDTH // SGU_GROUPS
    pair = 2 * SGU_BLOCK
    zero = jnp.zeros_like(w_s)
    w_bd = jnp.concatenate([jnp.concatenate([w_s, zero], axis=2),
                            jnp.concatenate([zero, w_s], axis=2)], axis=1).astype(BF16)
    b_tab = jnp.repeat(b_s.T, gw, axis=1)
    b_tab = jnp.concatenate([b_tab, b_tab], axis=0)
    return pl.pallas_call(
        _sgu_kernel,
        grid=(B, T // tm),
        in_specs=[
            pl.BlockSpec((1, tm, D), lambda b, t: (b, t, 0)),
            _const_spec((1, D)),
            _const_spec((D, 2 * SGU_WIDTH)),
            _const_spec((1, SGU_WIDTH)),
            _const_spec((1, SGU_WIDTH)),
            _const_spec((SGU_GROUPS, pair, pair)),
            _const_spec((pair, SGU_WIDTH)),
            _const_spec((SGU_WIDTH, D)),
        ],
        out_specs=pl.BlockSpec((1, tm, D), lambda b, t: (b, t, 0)),
        out_shape=jax.ShapeDtypeStruct((B, T, D), F32),
        scratch_shapes=[
            pltpu.VMEM((tm, D), BF16),
            pltpu.VMEM((tm, SGU_WIDTH), F32),
            pltpu.VMEM((tm, SGU_WIDTH), F32),
            pltpu.VMEM((tm, SGU_WIDTH), BF16),
        ],
        compiler_params=_params(2),
        name="sgu_mixer",
    )(x, g[None, :], w_in.astype(BF16), ln_g[None, :], ln_b[None, :], w_bd, b_tab,
      w_out.astype(BF16))


def kernel(x, attn_norm_g, ffn_norm_g, ab_w_in, ab_w_out, ab_rel_bias, c_w_in, c_ln_g, c_ln_b,
           c_w_s, c_b_s, c_w_out, ffn_w_up, ffn_conv_w, ffn_conv_b, ffn_w_down, final_norm_g):
    ffn_params = (ffn_norm_g[:, None, :], ffn_w_up.astype(BF16), ffn_conv_w,
                  ffn_conv_b[:, None, :], ffn_w_down.astype(BF16))
    yb, ya = _layer0_mixers(x, attn_norm_g[0][None, :], ab_w_in[0].astype(BF16), ab_rel_bias[0])
    h = _ffn(x, 0, *ffn_params, mixer=(ya, yb, ab_w_out[0].astype(BF16)))
    h = _sgu(h, attn_norm_g[1], c_w_in[0], c_ln_g[0], c_ln_b[0], c_w_s[0], c_b_s[0], c_w_out[0])
    h = _ffn(h, 1, *ffn_params, final_g=final_norm_g)
    return h
```

```python
import functools
import math

import numpy as np
import jax
import jax.numpy as jnp
from jax import lax
from jax.experimental import pallas as pl
from jax.experimental.pallas import tpu as pltpu

F32 = jnp.float32
BF16 = jnp.bfloat16

D_MODEL = 1024
CHUNK = 64
EPS = 1e-6
NEG_INF = -1e30
LOG2E = math.log2(math.e)

RET_HEADS = 4
RET_QK_DIM = 128
RET_V_DIM = 256
ATT_HEADS = 8
ATT_HEAD_DIM = 64
ATT_PAST_CHUNKS = 8
MAX_REL = 128
SGU_BLOCK = 128
SGU_GROUPS = 8
SGU_WIDTH = 2048
FFN_HIDDEN = 2816
CONV_WIDTH = 3

RET_QK_W = RET_HEADS * RET_QK_DIM
RET_V_W = RET_HEADS * RET_V_DIM
ATT_W = ATT_HEADS * ATT_HEAD_DIM
AB_IN_W = 2 * RET_QK_W + 2 * RET_V_W + 3 * ATT_W
AB_OUT_W = RET_V_W + ATT_W
Z_RET_Q = 0
Z_RET_K = Z_RET_Q + RET_QK_W
Z_RET_V = Z_RET_K + RET_QK_W
Z_RET_G = Z_RET_V + RET_V_W
Z_ATT_Q = Z_RET_G + RET_V_W
Z_ATT_K = Z_ATT_Q + ATT_W
Z_ATT_V = Z_ATT_K + ATT_W

VMEM_LIMIT_BYTES = 56 * 1024 * 1024
BF16_SUBLANES = 16
LANES = 128

TM_PROJ = 512
TM_FFN = 512
TR_RET = 256
TA_ATT = 512
QB_ATT = 256
FFN_CB = 256
IN_CB = 256
SGU_CB = 256
NORM_ROWS = 32
CONV_ROWS = 64


def _const_spec(shape):
    zeros = (0,) * len(shape)
    return pl.BlockSpec(shape, lambda *_: zeros, pipeline_mode=pl.Buffered(1))


def _params(n_axes, flags=None):
    return pltpu.CompilerParams(
        dimension_semantics=("arbitrary",) * n_axes,
        vmem_limit_bytes=VMEM_LIMIT_BYTES,
        flags=flags)


def _rms_norm(x, g):
    return x * lax.rsqrt(jnp.mean(x * x, axis=-1, keepdims=True) + EPS) * g


def _gelu_tanh(x):
    c = math.sqrt(2.0 / math.pi)
    return 0.5 * x * (1.0 + jnp.tanh(c * (x + 0.044715 * (x * x * x))))


def _dot(a, b):
    return jnp.dot(a, b, preferred_element_type=F32)


def _dot_nt(a, b):
    return lax.dot_general(a, b, (((1,), (1,)), ((), ())), preferred_element_type=F32)


def _dot_tn(a, b):
    return lax.dot_general(a, b, (((0,), (0,)), ((), ())), preferred_element_type=F32)


def _inproj_blocks(x_ref, g_ref, w_ref, base_ref, off_ref, zput):
    hn = _rms_norm(x_ref[0], g_ref[...]).astype(BF16)
    ca, sa = base_ref[0, 0:1, :], base_ref[0, 1:2, :]
    cos = ca * off_ref[0] - sa * off_ref[1]
    sin = sa * off_ref[2] + ca * off_ref[3]

    def block(j):
        z = _dot(hn, w_ref[:, j * IN_CB:(j + 1) * IN_CB])
        if j * IN_CB < Z_RET_V:
            parts = []
            for h in range(IN_CB // RET_QK_DIM):
                xh = z[:, h * RET_QK_DIM:(h + 1) * RET_QK_DIM]
                parts.append(xh * cos + pltpu.roll(xh, RET_QK_DIM // 2, axis=1) * sin)
            z = jnp.concatenate(parts, axis=1)
        elif Z_ATT_Q <= j * IN_CB < Z_ATT_K:
            z = z * (ATT_HEAD_DIM ** -0.5 * LOG2E)
        zput(j * IN_CB, z.astype(BF16))

    return [functools.partial(block, j) for j in range(AB_IN_W // IN_CB)]


def _rotary_tables(T, tm):
    half = RET_QK_DIM // 2
    inv = 1.0 / (np.float32(10000.0) ** np.linspace(0.0, 1.0, half, dtype=np.float32))
    inv = np.concatenate([inv, inv]).astype(np.float64)
    sign = np.concatenate([-np.ones(half), np.ones(half)])
    base = (np.arange(T // tm) * tm)[:, None] * inv[None, :]
    off = np.arange(tm)[:, None] * inv[None, :]
    bases = np.stack([np.cos(base), np.sin(base)], axis=1)
    offs = np.stack([np.cos(off), np.sin(off), sign * np.cos(off), sign * np.sin(off)])
    return jnp.asarray(bases, F32), jnp.asarray(offs, F32)


def _ret_tables(tr):
    h = np.arange(RET_HEADS, dtype=np.float64)
    log_g = np.log1p(-np.exp2(-5.0 - h))
    n = np.arange(tr)
    cn, cm = n[:, None] // CHUNK, n[None, :] // CHUNK
    diff = (n[:, None] - n[None, :]).astype(np.float64)
    expo = np.where(cn == cm, np.abs(diff), diff)
    dmat = np.where(cm <= cn, np.exp(log_g[:, None, None] * expo), 0.0)
    scale = RET_QK_DIM ** -0.5
    qdec = np.exp(log_g[:, None] * (n + 1.0)[None, :])
    kdec = np.exp(log_g[:, None] * (tr - 1.0 - n)[None, :]) * scale
    qdec = np.broadcast_to(qdec[:, :, None], (RET_HEADS, tr, RET_QK_DIM))
    kdec = np.broadcast_to(kdec[:, :, None], (RET_HEADS, tr, RET_QK_DIM))
    tile_dec = [float(v) for v in np.exp(log_g * tr)]
    return (jnp.asarray(dmat * scale, F32), jnp.asarray(qdec, F32), jnp.asarray(kdec, F32),
            tile_dec)


def _ret_tile(tile_dec, rows, zget, d_ref, qd_ref, kd_ref, o_ref, state_ref):
    for h in range(RET_HEADS):
        vv = slice(h * RET_V_DIM, (h + 1) * RET_V_DIM)
        q = zget(rows, Z_RET_Q + h * RET_QK_DIM, RET_QK_DIM)
        k = zget(rows, Z_RET_K + h * RET_QK_DIM, RET_QK_DIM)
        v = zget(rows, Z_RET_V + h * RET_V_DIM, RET_V_DIM)
        state = state_ref[h]
        s = _dot_nt(q, k) * d_ref[h]
        qd = (q.astype(F32) * qd_ref[h]).astype(BF16)
        kd = (k.astype(F32) * kd_ref[h]).astype(BF16)
        r = _dot(s.astype(BF16), v) + _dot(qd, state.astype(BF16))
        state_ref[h] = state * tile_dec[h] + _dot_tn(kd, v)
        mu = jnp.mean(r, axis=-1, keepdims=True)
        rc = r - mu
        var = jnp.mean(rc * rc, axis=-1, keepdims=True)
        rn = rc * lax.rsqrt(var + EPS)
        gate = zget(rows, Z_RET_G + h * RET_V_DIM, RET_V_DIM).astype(F32)
        o_ref[0, rows, vv] = (0.5 * gate * (1.0 + jnp.tanh(0.5 * gate)) * rn).astype(BF16)


def _att_bias_vectors(rel_bias):
    hist = ATT_PAST_CHUNKS * CHUNK
    kw = QB_ATT + hist
    nr = 2 * MAX_REL + 1
    period = 2 * hist
    assert period >= kw + QB_ATT - 1 and hist > MAX_REL
    n_far = hist - MAX_REL + 1
    n_near = kw - (hist + MAX_REL)
    n_wrap = period - kw
    first, last = rel_bias[:, 0:1], rel_bias[:, nr - 1:nr]
    return jnp.concatenate([
        jnp.broadcast_to(last, (ATT_HEADS, n_far)),
        rel_bias[:, nr - 2:0:-1],
        jnp.broadcast_to(first, (ATT_HEADS, n_near)),
        jnp.broadcast_to(last, (ATT_HEADS, n_wrap))], axis=1).astype(F32)


def _layer0_kernel(tile_dec, x_ref, ng_ref, w_ref, base_ref, off_ref, g_ref, d_ref, qd_ref,
                   kd_ref, o_ref, ro_ref, zbuf0, zbuf1, kbuf, vbuf, bias_ref, state_ref):
    t = pl.program_id(1)
    ta = TA_ATT
    hist = ATT_PAST_CHUNKS * CHUNK
    kw = QB_ATT + hist
    period = g_ref.shape[1]
    n_qb = ta // QB_ATT
    tile = jnp.maximum(t - 1, 0)

    @pl.when(t == 0)
    def _():
        zbuf1[...] = jnp.zeros(zbuf1.shape, BF16)

    @pl.when(t <= 1)
    def _():
        kbuf[0:hist, :] = jnp.zeros((hist, ATT_W), BF16)
        vbuf[0:hist, :] = jnp.zeros((hist, ATT_W), BF16)
        state_ref[...] = jnp.zeros_like(state_ref)

    @pl.when(t <= 2)
    def _():
        ci = lax.broadcasted_iota(jnp.int32, (QB_ATT, kw), 0) // CHUNK
        col = lax.broadcasted_iota(jnp.int32, (QB_ATT, kw), 1)
        cj = col // CHUNK
        in_band = jnp.logical_and(cj >= ci, cj <= ci + ATT_PAST_CHUNKS)
        for qb in range(n_qb):
            valid = jnp.logical_and(in_band, col >= hist - qb * QB_ATT - tile * ta)
            for h in range(ATT_HEADS):
                rows = jnp.broadcast_to(g_ref[h:h + 1, :], (QB_ATT, period))
                skew = pltpu.roll(rows, 0, axis=1, stride=1, stride_axis=0)
                bias_ref[qb, h] = jnp.where(valid, skew[:, :kw] * LOG2E, NEG_INF)

    def step(wbuf, rbuf):
        def zput(c0, block):
            wbuf[:, c0:c0 + block.shape[1]] = block

        def zget(rows, c0, width):
            return rbuf[rows, c0:c0 + width]

        pending = _inproj_blocks(x_ref, ng_ref, w_ref, base_ref, off_ref, zput)

        def issue_projection_block():
            if pending:
                pending.pop(0)()

        kbuf[hist:hist + ta, :] = zget(slice(0, ta), Z_ATT_K, ATT_W)
        vbuf[hist:hist + ta, :] = zget(slice(0, ta), Z_ATT_V, ATT_W)

        lane = lax.broadcasted_iota(jnp.int32, (1, 2 * ATT_HEAD_DIM), 1)
        low = lane < ATT_HEAD_DIM
        for qb in range(n_qb):
            rows = slice(qb * QB_ATT, (qb + 1) * QB_ATT)
            krows = slice(qb * QB_ATT, qb * QB_ATT + kw)
            for hp in range(ATT_HEADS // 2):
                lanes = slice(hp * 2 * ATT_HEAD_DIM, (hp + 1) * 2 * ATT_HEAD_DIM)
                qp = zget(rows, Z_ATT_Q + lanes.start, 2 * ATT_HEAD_DIM)
                kp = kbuf[krows, lanes]
                vp = vbuf[krows, lanes]
                outs = []
                for e in range(2):
                    keep = low if e == 0 else jnp.logical_not(low)
                    qm = jnp.where(keep, qp, jnp.zeros_like(qp))
                    s = _dot_nt(qm, kp)
                    issue_projection_block()
                    s = s + bias_ref[qb, 2 * hp + e]
                    m = jnp.max(s, axis=-1, keepdims=True)
                    p = jnp.exp2(s - m).astype(BF16)
                    ov = _dot(p, jnp.where(keep, vp, jnp.ones_like(vp)))
                    outs.append(ov / pltpu.roll(ov, ATT_HEAD_DIM, axis=1))
                o_ref[0, rows, lanes] = jnp.where(low, outs[0], outs[1]).astype(BF16)
            for r0 in range(qb * QB_ATT, (qb + 1) * QB_ATT, TR_RET):
                _ret_tile(tile_dec, slice(r0, r0 + TR_RET), zget, d_ref, qd_ref, kd_ref,
                          ro_ref, state_ref)
        while pending:
            issue_projection_block()

        kbuf[0:hist, :] = kbuf[ta:ta + hist, :]
        vbuf[0:hist, :] = vbuf[ta:ta + hist, :]

    @pl.when(t % 2 == 0)
    def _():
        step(zbuf0, zbuf1)

    @pl.when(t % 2 == 1)
    def _():
        step(zbuf1, zbuf0)


def _layer0_mixers(x, norm_g, w_in, rel_bias):
    B, T, D = x.shape
    ta = TA_ATT
    n_t = T // ta
    hist = ATT_PAST_CHUNKS * CHUNK
    assert ta == hist and n_t >= 2 and QB_ATT % TR_RET == 0 and AB_IN_W % IN_CB == 0
    g = _att_bias_vectors(rel_bias)
    dmat, qdec, kdec, tile_dec = _ret_tables(TR_RET)
    bases, offs = _rotary_tables(T, ta)
    proj_tile = lambda b, t: (b, jnp.minimum(t, n_t - 1), 0)
    mix_tile = lambda b, t: (b, jnp.maximum(t - 1, 0), 0)
    return pl.pallas_call(
        functools.partial(_layer0_kernel, tile_dec),
        grid=(B, n_t + 1),
        in_specs=[
            pl.BlockSpec((1, ta, D), proj_tile),
            _const_spec((1, D)),
            _const_spec((D, AB_IN_W)),
            pl.BlockSpec((1, 2, RET_QK_DIM), lambda b, t: (jnp.minimum(t, n_t - 1), 0, 0)),
            _const_spec(offs.shape),
            _const_spec(g.shape),
            _const_spec(dmat.shape), _const_spec(qdec.shape), _const_spec(kdec.shape),
        ],
        out_specs=[pl.BlockSpec((1, ta, ATT_W), mix_tile),
                   pl.BlockSpec((1, ta, RET_V_W), mix_tile)],
        out_shape=[jax.ShapeDtypeStruct((B, T, ATT_W), BF16),
                   jax.ShapeDtypeStruct((B, T, RET_V_W), BF16)],
        scratch_shapes=[pltpu.VMEM((ta, AB_IN_W), BF16),
                        pltpu.VMEM((ta, AB_IN_W), BF16),
                        pltpu.VMEM((hist + ta, ATT_W), BF16),
                        pltpu.VMEM((hist + ta, ATT_W), BF16),
                        pltpu.VMEM((ta // QB_ATT, ATT_HEADS, QB_ATT, QB_ATT + hist), F32),
                        pltpu.VMEM((RET_HEADS, RET_QK_DIM, RET_V_DIM), F32)],
        compiler_params=_params(2),
        name="layer0_mixers",
    )(x, norm_g, w_in, bases, offs, g, dmat, qdec, kdec)


def _ffn_kernel(final_norm, mixer_proj, *refs):
    if mixer_proj:
        x_ref, ya_ref, yb_ref, wmix_ref = refs[:4]
        refs = refs[4:]
    else:
        x_ref = refs[0]
        refs = refs[1:]
    (g_ref, wup_ref, cw_ref, cb_ref, wdn_ref, fg_ref, o_ref,
     hn_ext, zs_ref, act_ref, ys_ref) = refs
    tm = x_ref.shape[1]
    pad = BF16_SUBLANES
    n_slab = FFN_CB // LANES
    half_rows = tm // 2
    n_lane_slabs = D_MODEL // LANES

    @pl.when(pl.program_id(1) == 0)
    def _():
        hn_ext[0:pad, :] = jnp.zeros((pad, D_MODEL), BF16)

    if mixer_proj:
        na = ya_ref.shape[2]
        mix = _dot(ya_ref[0], wmix_ref[0:na, :]) + _dot(yb_ref[0], wmix_ref[na:, :])
        for l in range(n_lane_slabs):
            ys_ref[l] = mix[:, l * LANES:(l + 1) * LANES]
    for r0 in range(0, tm, NORM_ROWS):
        rows = slice(r0, r0 + NORM_ROWS)
        xs = x_ref[0, rows, :]
        if mixer_proj:
            xs = xs + jnp.concatenate([ys_ref[l, rows, :] for l in range(n_lane_slabs)], axis=1)
            o_ref[0, rows, :] = xs
        hn_ext[pad + r0:pad + r0 + NORM_ROWS, :] = _rms_norm(xs, g_ref[...]).astype(BF16)
    lhs = hn_ext[...]
    n_blocks = FFN_HIDDEN // FFN_CB

    def up_block(j):
        for half in range(2):
            c0 = half * FFN_HIDDEN + j * FFN_CB
            z = _dot(lhs, wup_ref[:, c0:c0 + FFN_CB])
            for l in range(n_slab):
                zs_ref[j % 2, half * n_slab + l] = z[:, l * LANES:(l + 1) * LANES]

    up_block(0)
    for j in range(n_blocks):
        buf = j % 2
        if j + 1 < n_blocks:
            up_block(j + 1)
        for par in range(2):
            for l in range(n_slab):
                for u0 in range(0, half_rows, CONV_ROWS):
                    conv = []
                    for half in range(2):
                        c0 = half * FFN_HIDDEN + j * FFN_CB + l * LANES
                        c = cb_ref[:, c0:c0 + LANES]
                        for tap in range(CONV_WIDTH):
                            off = pad - (CONV_WIDTH - 1) + tap + par + 2 * u0
                            c = c + (zs_ref[buf, half * n_slab + l,
                                            pl.ds(off, CONV_ROWS, stride=2), :]
                                     * cw_ref[tap:tap + 1, c0:c0 + LANES])
                        conv.append(c)
                    a0 = j * FFN_CB + l * LANES
                    act_ref[par * half_rows + u0:par * half_rows + u0 + CONV_ROWS,
                            a0:a0 + LANES] = (_gelu_tanh(conv[0]) * conv[1]).astype(BF16)
    hn_ext[0:pad, :] = hn_ext[tm:tm + pad, :]
    k_split = (n_blocks // 2 + 1) * FFN_CB
    yp = (_dot(act_ref[:, :k_split], wdn_ref[:k_split, :])
          + _dot(act_ref[:, k_split:], wdn_ref[k_split:, :]))
    for l in range(n_lane_slabs):
        for par in range(2):
            ys_ref[l, pl.ds(par, half_rows, stride=2), :] = (
                yp[par * half_rows:(par + 1) * half_rows, l * LANES:(l + 1) * LANES])
    for r0 in range(0, tm, NORM_ROWS):
        rows = slice(r0, r0 + NORM_ROWS)
        y = (o_ref if mixer_proj else x_ref)[0, rows, :] + jnp.concatenate(
            [ys_ref[l, rows, :] for l in range(n_lane_slabs)], axis=1)
        if final_norm:
            y = _rms_norm(y, fg_ref[...])
        o_ref[0, rows, :] = y


def _layer_spec(layer, shape):
    zeros = (0,) * len(shape)
    return pl.BlockSpec((None,) + tuple(shape), lambda *_: (layer,) + zeros,
                        pipeline_mode=pl.Buffered(1))


def _ffn(x, layer, g, w_up, conv_w, conv_b, w_down, final_g=None, mixer=None):
    B, T, D = x.shape
    tm = TM_FFN
    final_norm = final_g is not None
    if final_g is None:
        final_g = jnp.ones((D,), F32)
    tile = lambda width: pl.BlockSpec((1, tm, width), lambda b, t: (b, t, 0))
    mixer_args, mixer_specs = (), []
    if mixer is not None:
        ya, yb, w_mix = mixer
        mixer_args = (ya, yb, w_mix)
        mixer_specs = [tile(ya.shape[2]), tile(yb.shape[2]), _const_spec(w_mix.shape)]
    return pl.pallas_call(
        functools.partial(_ffn_kernel, final_norm, mixer is not None),
        grid=(B, T // tm),
        in_specs=[tile(D)] + mixer_specs + [
            _layer_spec(layer, (1, D)),
            _layer_spec(layer, (D, 2 * FFN_HIDDEN)),
            _layer_spec(layer, (CONV_WIDTH, 2 * FFN_HIDDEN)),
            _layer_spec(layer, (1, 2 * FFN_HIDDEN)),
            _layer_spec(layer, (FFN_HIDDEN, D)),
            _const_spec((1, D)),
        ],
        out_specs=pl.BlockSpec((1, tm, D), lambda b, t: (b, t, 0)),
        out_shape=jax.ShapeDtypeStruct((B, T, D), F32),
        scratch_shapes=[
            pltpu.VMEM((tm + BF16_SUBLANES, D), BF16),
            pltpu.VMEM((2, 2 * FFN_CB // LANES, tm + BF16_SUBLANES, LANES), F32),
            pltpu.VMEM((tm, FFN_HIDDEN), BF16),
            pltpu.VMEM((D // LANES, tm, LANES), F32),
        ],
        compiler_params=_params(2),
        name="conv_ffn_final" if final_norm else "conv_ffn",
    )(x, *mixer_args, g, w_up, conv_w, conv_b, w_down, final_g[None, :])


def _sgu_kernel(x_ref, g_ref, win_ref, lng_ref, lnb_ref, ws_ref, bs_ref, wout_ref, o_ref,
                hn_ref, v_ref, u_ref, y_ref):
    tm = x_ref.shape[1]
    gw = SGU_WIDTH // SGU_GROUPS
    pair = 2 * SGU_BLOCK
    for r0 in range(0, tm, NORM_ROWS):
        rows = slice(r0, r0 + NORM_ROWS)
        hn_ref[rows, :] = _rms_norm(x_ref[0, rows, :], g_ref[...]).astype(BF16)
    hn = hn_ref[...]

    vsum = jnp.zeros((tm, 1), F32)
    for j in range(SGU_WIDTH // SGU_CB):
        cols = slice(j * SGU_CB, (j + 1) * SGU_CB)
        zv = _gelu_tanh(_dot(hn, win_ref[:, SGU_WIDTH + j * SGU_CB:SGU_WIDTH + (j + 1) * SGU_CB]))
        v_ref[:, cols] = zv
        vsum = vsum + jnp.sum(zv, axis=-1, keepdims=True)
    for j in range(SGU_WIDTH // SGU_CB):
        cols = slice(j * SGU_CB, (j + 1) * SGU_CB)
        u_ref[:, cols] = _gelu_tanh(_dot(hn, win_ref[:, cols]))
    mu = vsum * (1.0 / SGU_WIDTH)
    vsq = jnp.zeros((tm, 1), F32)
    for j in range(SGU_WIDTH // SGU_CB):
        cols = slice(j * SGU_CB, (j + 1) * SGU_CB)
        vc = v_ref[:, cols] - mu
        vsq = vsq + jnp.sum(vc * vc, axis=-1, keepdims=True)
    rstd = lax.rsqrt(vsq * (1.0 / SGU_WIDTH) + EPS)

    ri = lax.broadcasted_iota(jnp.int32, (pair, pair), 0)
    ci = lax.broadcasted_iota(jnp.int32, (pair, pair), 1)
    allowed = (ci // CHUNK) <= (ri // CHUNK)
    same = (ci // SGU_BLOCK) == (ri // SGU_BLOCK)
    keep = jnp.logical_and(allowed, same)
    for gidx in range(SGU_GROUPS):
        cols = slice(gidx * gw, (gidx + 1) * gw)
        w = jnp.where(keep, ws_ref[gidx], jnp.zeros((pair, pair), BF16))
        vn = ((v_ref[:, cols] - mu) * rstd * lng_ref[:, cols] + lnb_ref[:, cols]).astype(BF16)
        for rp in range(tm // pair):
            rows = slice(rp * pair, (rp + 1) * pair)
            mixed = _dot(w, vn[rows]) + bs_ref[:, cols]
            y_ref[rows, cols] = (u_ref[rows, cols] * mixed).astype(BF16)
    o_ref[0] = x_ref[0] + _dot(y_ref[...], wout_ref[...])


def _sgu(x, g, w_in, ln_g, ln_b, w_s, b_s, w_out):
    B, T, D = x.shape
    tm = TM_PROJ
    gw = SGU_WIDTH // SGU_GROUPS
    pair = 2 * SGU_BLOCK
    zero = jnp.zeros_like(w_s)
    w_bd = jnp.concatenate([jnp.concatenate([w_s, zero], axis=2),
                            jnp.concatenate([zero, w_s], axis=2)], axis=1).astype(BF16)
    b_tab = jnp.repeat(b_s.T, gw, axis=1)
    b_tab = jnp.concatenate([b_tab, b_tab], axis=0)
    return pl.pallas_call(
        _sgu_kernel,
        grid=(B, T // tm),
        in_specs=[
            pl.BlockSpec((1, tm, D), lambda b, t: (b, t, 0)),
            _const_spec((1, D)),
            _const_spec((D, 2 * SGU_WIDTH)),
            _const_spec((1, SGU_WIDTH)),
            _const_spec((1, SGU_WIDTH)),
            _const_spec((SGU_GROUPS, pair, pair)),
            _const_spec((pair, SGU_WIDTH)),
            _const_spec((SGU_WIDTH, D)),
        ],
        out_specs=pl.BlockSpec((1, tm, D), lambda b, t: (b, t, 0)),
        out_shape=jax.ShapeDtypeStruct((B, T, D), F32),
        scratch_shapes=[
            pltpu.VMEM((tm, D), BF16),
            pltpu.VMEM((tm, SGU_WIDTH), F32),
            pltpu.VMEM((tm, SGU_WIDTH), F32),
            pltpu.VMEM((tm, SGU_WIDTH), BF16),
        ],
        compiler_params=_params(2),
        name="sgu_mixer",
    )(x, g[None, :], w_in.astype(BF16), ln_g[None, :], ln_b[None, :], w_bd, b_tab,
      w_out.astype(BF16))


def kernel(x, attn_norm_g, ffn_norm_g, ab_w_in, ab_w_out, ab_rel_bias, c_w_in, c_ln_g, c_ln_b,
           c_w_s, c_b_s, c_w_out, ffn_w_up, ffn_conv_w, ffn_conv_b, ffn_w_down, final_norm_g):
    ffn_params = (ffn_norm_g[:, None, :], ffn_w_up.astype(BF16), ffn_conv_w,
                  ffn_conv_b[:, None, :], ffn_w_down.astype(BF16))
    yb, ya = _layer0_mixers(x, attn_norm_g[0][None, :], ab_w_in[0].astype(BF16), ab_rel_bias[0])
    h = _ffn(x, 0, *ffn_params, mixer=(ya, yb, ab_w_out[0].astype(BF16)))
    h = _sgu(h, attn_norm_g[1], c_w_in[0], c_ln_g[0], c_ln_b[0], c_w_s[0], c_b_s[0], c_w_out[0])
    h = _ffn(h, 1, *ffn_params, final_g=final_norm_g)
    return h
```

```python
import functools
import math

import numpy as np
import jax
import jax.numpy as jnp
from jax import lax
from jax.experimental import pallas as pl
from jax.experimental.pallas import tpu as pltpu

F32 = jnp.float32
BF16 = jnp.bfloat16

D_MODEL = 1024
CHUNK = 64
EPS = 1e-6
NEG_INF = -1e30
LOG2E = math.log2(math.e)

RET_HEADS = 4
RET_QK_DIM = 128
RET_V_DIM = 256
ATT_HEADS = 8
ATT_HEAD_DIM = 64
ATT_PAST_CHUNKS = 8
MAX_REL = 128
SGU_BLOCK = 128
SGU_GROUPS = 8
SGU_WIDTH = 2048
FFN_HIDDEN = 2816
CONV_WIDTH = 3

RET_QK_W = RET_HEADS * RET_QK_DIM
RET_V_W = RET_HEADS * RET_V_DIM
ATT_W = ATT_HEADS * ATT_HEAD_DIM
AB_IN_W = 2 * RET_QK_W + 2 * RET_V_W + 3 * ATT_W
AB_OUT_W = RET_V_W + ATT_W
Z_RET_Q = 0
Z_RET_K = Z_RET_Q + RET_QK_W
Z_RET_V = Z_RET_K + RET_QK_W
Z_RET_G = Z_RET_V + RET_V_W
Z_ATT_Q = Z_RET_G + RET_V_W
Z_ATT_K = Z_ATT_Q + ATT_W
Z_ATT_V = Z_ATT_K + ATT_W

VMEM_LIMIT_BYTES = 56 * 1024 * 1024
SUBLANES = 8
BF16_SUBLANES = 16
LANES = 128

TM_PROJ = 512
TM_FFN = 512
TR_RET = 256
TA_ATT = 512
QB_ATT = 256
FFN_CB = 256
IN_CB = 256
SGU_CB = 256
NORM_ROWS = 32
CONV_ROWS = 64


def _const_spec(shape):
    zeros = (0,) * len(shape)
    return pl.BlockSpec(shape, lambda *_: zeros, pipeline_mode=pl.Buffered(1))


def _params(n_axes, flags=None):
    return pltpu.CompilerParams(
        dimension_semantics=("arbitrary",) * n_axes,
        vmem_limit_bytes=VMEM_LIMIT_BYTES,
        flags=flags)


def _rms_norm(x, g):
    return x * lax.rsqrt(jnp.mean(x * x, axis=-1, keepdims=True) + EPS) * g


def _gelu_tanh(x):
    c = math.sqrt(2.0 / math.pi)
    return 0.5 * x * (1.0 + jnp.tanh(c * (x + 0.044715 * (x * x * x))))


def _dot(a, b):
    return jnp.dot(a, b, preferred_element_type=F32)


def _dot_nt(a, b):
    return lax.dot_general(a, b, (((1,), (1,)), ((), ())), preferred_element_type=F32)


def _dot_tn(a, b):
    return lax.dot_general(a, b, (((0,), (0,)), ((), ())), preferred_element_type=F32)


def _inproj_blocks(x_ref, g_ref, w_ref, base_ref, off_ref, zput):
    hn = _rms_norm(x_ref[0], g_ref[...]).astype(BF16)
    ca, sa = base_ref[0, 0:1, :], base_ref[0, 1:2, :]
    cos = ca * off_ref[0] - sa * off_ref[1]
    sin = sa * off_ref[2] + ca * off_ref[3]

    def block(j):
        z = _dot(hn, w_ref[:, j * IN_CB:(j + 1) * IN_CB])
        if j * IN_CB < Z_RET_V:
            parts = []
            for h in range(IN_CB // RET_QK_DIM):
                xh = z[:, h * RET_QK_DIM:(h + 1) * RET_QK_DIM]
                parts.append(xh * cos + pltpu.roll(xh, RET_QK_DIM // 2, axis=1) * sin)
            z = jnp.concatenate(parts, axis=1)
        elif Z_ATT_Q <= j * IN_CB < Z_ATT_K:
            z = z * (ATT_HEAD_DIM ** -0.5 * LOG2E)
        zput(j * IN_CB, z.astype(BF16))

    return [functools.partial(block, j) for j in range(AB_IN_W // IN_CB)]


def _rotary_tables(T, tm):
    half = RET_QK_DIM // 2
    inv = 1.0 / (np.float32(10000.0) ** np.linspace(0.0, 1.0, half, dtype=np.float32))
    inv = np.concatenate([inv, inv]).astype(np.float64)
    sign = np.concatenate([-np.ones(half), np.ones(half)])
    base = (np.arange(T // tm) * tm)[:, None] * inv[None, :]
    off = np.arange(tm)[:, None] * inv[None, :]
    bases = np.stack([np.cos(base), np.sin(base)], axis=1)
    offs = np.stack([np.cos(off), np.sin(off), sign * np.cos(off), sign * np.sin(off)])
    return jnp.asarray(bases, F32), jnp.asarray(offs, F32)


def _ret_tables(tr):
    h = np.arange(RET_HEADS, dtype=np.float64)
    log_g = np.log1p(-np.exp2(-5.0 - h))
    n = np.arange(tr)
    cn, cm = n[:, None] // CHUNK, n[None, :] // CHUNK
    diff = (n[:, None] - n[None, :]).astype(np.float64)
    expo = np.where(cn == cm, np.abs(diff), diff)
    dmat = np.where(cm <= cn, np.exp(log_g[:, None, None] * expo), 0.0)
    scale = RET_QK_DIM ** -0.5
    qdec = np.exp(log_g[:, None] * (n + 1.0)[None, :])
    kdec = np.exp(log_g[:, None] * (tr - 1.0 - n)[None, :]) * scale
    qdec = np.broadcast_to(qdec[:, :, None], (RET_HEADS, tr, RET_QK_DIM))
    kdec = np.broadcast_to(kdec[:, :, None], (RET_HEADS, tr, RET_QK_DIM))
    tile_dec = [float(v) for v in np.exp(log_g * tr)]
    return (jnp.asarray(dmat * scale, F32), jnp.asarray(qdec, F32), jnp.asarray(kdec, F32),
            tile_dec)


def _ret_tile(tile_dec, rows, zget, d_ref, qd_ref, kd_ref, o_ref, state_ref):
    for h in range(RET_HEADS):
        vv = slice(h * RET_V_DIM, (h + 1) * RET_V_DIM)
        q = zget(rows, Z_RET_Q + h * RET_QK_DIM, RET_QK_DIM)
        k = zget(rows, Z_RET_K + h * RET_QK_DIM, RET_QK_DIM)
        v = zget(rows, Z_RET_V + h * RET_V_DIM, RET_V_DIM)
        state = state_ref[h]
        s = _dot_nt(q, k) * d_ref[h]
        qd = (q.astype(F32) * qd_ref[h]).astype(BF16)
        kd = (k.astype(F32) * kd_ref[h]).astype(BF16)
        r = _dot(s.astype(BF16), v) + _dot(qd, state.astype(BF16))
        state_ref[h] = state * tile_dec[h] + _dot_tn(kd, v)
        mu = jnp.mean(r, axis=-1, keepdims=True)
        rc = r - mu
        var = jnp.mean(rc * rc, axis=-1, keepdims=True)
        rn = rc * lax.rsqrt(var + EPS)
        gate = zget(rows, Z_RET_G + h * RET_V_DIM, RET_V_DIM).astype(F32)
        o_ref[0, rows, vv] = (0.5 * gate * (1.0 + jnp.tanh(0.5 * gate)) * rn).astype(BF16)


def _att_bias_vectors(rel_bias):
    hist = ATT_PAST_CHUNKS * CHUNK
    kw = QB_ATT + hist
    nr = 2 * MAX_REL + 1
    period = 2 * hist
    assert period >= kw + QB_ATT - 1 and hist > MAX_REL
    n_far = hist - MAX_REL + 1
    n_near = kw - (hist + MAX_REL)
    n_wrap = period - kw
    first, last = rel_bias[:, 0:1], rel_bias[:, nr - 1:nr]
    return jnp.concatenate([
        jnp.broadcast_to(last, (ATT_HEADS, n_far)),
        rel_bias[:, nr - 2:0:-1],
        jnp.broadcast_to(first, (ATT_HEADS, n_near)),
        jnp.broadcast_to(last, (ATT_HEADS, n_wrap))], axis=1).astype(F32)


def _layer0_kernel(tile_dec, x_ref, ng_ref, w_ref, base_ref, off_ref, g_ref, d_ref, qd_ref,
                   kd_ref, o_ref, ro_ref, zbuf0, zbuf1, kbuf, vbuf, bias_ref, state_ref):
    t = pl.program_id(1)
    ta = TA_ATT
    hist = ATT_PAST_CHUNKS * CHUNK
    kw = QB_ATT + hist
    period = g_ref.shape[1]
    n_qb = ta // QB_ATT
    tile = jnp.maximum(t - 1, 0)

    @pl.when(t == 0)
    def _():
        zbuf1[...] = jnp.zeros(zbuf1.shape, BF16)

    @pl.when(t <= 1)
    def _():
        kbuf[0:hist, :] = jnp.zeros((hist, ATT_W), BF16)
        vbuf[0:hist, :] = jnp.zeros((hist, ATT_W), BF16)
        state_ref[...] = jnp.zeros_like(state_ref)

    @pl.when(t <= 2)
    def _():
        ci = lax.broadcasted_iota(jnp.int32, (QB_ATT, kw), 0) // CHUNK
        col = lax.broadcasted_iota(jnp.int32, (QB_ATT, kw), 1)
        cj = col // CHUNK
        in_band = jnp.logical_and(cj >= ci, cj <= ci + ATT_PAST_CHUNKS)
        for qb in range(n_qb):
            valid = jnp.logical_and(in_band, col >= hist - qb * QB_ATT - tile * ta)
            for h in range(ATT_HEADS):
                rows = jnp.broadcast_to(g_ref[h:h + 1, :], (QB_ATT, period))
                skew = pltpu.roll(rows, 0, axis=1, stride=1, stride_axis=0)
                bias_ref[qb, h] = jnp.where(valid, skew[:, :kw] * LOG2E, NEG_INF)

    def step(wbuf, rbuf):
        def zput(c0, block):
            wbuf[:, c0:c0 + block.shape[1]] = block

        def zget(rows, c0, width):
            return rbuf[rows, c0:c0 + width]

        pending = _inproj_blocks(x_ref, ng_ref, w_ref, base_ref, off_ref, zput)

        def issue_projection_block():
            if pending:
                pending.pop(0)()

        kbuf[hist:hist + ta, :] = zget(slice(0, ta), Z_ATT_K, ATT_W)
        vbuf[hist:hist + ta, :] = zget(slice(0, ta), Z_ATT_V, ATT_W)

        lane = lax.broadcasted_iota(jnp.int32, (1, 2 * ATT_HEAD_DIM), 1)
        low = lane < ATT_HEAD_DIM
        for qb in range(n_qb):
            rows = slice(qb * QB_ATT, (qb + 1) * QB_ATT)
            krows = slice(qb * QB_ATT, qb * QB_ATT + kw)
            for hp in range(ATT_HEADS // 2):
                lanes = slice(hp * 2 * ATT_HEAD_DIM, (hp + 1) * 2 * ATT_HEAD_DIM)
                qp = zget(rows, Z_ATT_Q + lanes.start, 2 * ATT_HEAD_DIM)
                kp = kbuf[krows, lanes]
                vp = vbuf[krows, lanes]
                outs = []
                for e in range(2):
                    keep = low if e == 0 else jnp.logical_not(low)
                    qm = jnp.where(keep, qp, jnp.zeros_like(qp))
                    s = _dot_nt(qm, kp)
                    issue_projection_block()
                    s = s + bias_ref[qb, 2 * hp + e]
                    m = jnp.max(s, axis=-1, keepdims=True)
                    p = jnp.exp2(s - m).astype(BF16)
                    ov = _dot(p, jnp.where(keep, vp, jnp.ones_like(vp)))
                    outs.append(ov / pltpu.roll(ov, ATT_HEAD_DIM, axis=1))
                o_ref[0, rows, lanes] = jnp.where(low, outs[0], outs[1]).astype(BF16)
            for r0 in range(qb * QB_ATT, (qb + 1) * QB_ATT, TR_RET):
                _ret_tile(tile_dec, slice(r0, r0 + TR_RET), zget, d_ref, qd_ref, kd_ref,
                          ro_ref, state_ref)
        while pending:
            issue_projection_block()

        kbuf[0:hist, :] = kbuf[ta:ta + hist, :]
        vbuf[0:hist, :] = vbuf[ta:ta + hist, :]

    @pl.when(t % 2 == 0)
    def _():
        step(zbuf0, zbuf1)

    @pl.when(t % 2 == 1)
    def _():
        step(zbuf1, zbuf0)


def _layer0_mixers(x, norm_g, w_in, rel_bias):
    B, T, D = x.shape
    ta = TA_ATT
    n_t = T // ta
    hist = ATT_PAST_CHUNKS * CHUNK
    assert ta == hist and n_t >= 2 and QB_ATT % TR_RET == 0 and AB_IN_W % IN_CB == 0
    g = _att_bias_vectors(rel_bias)
    dmat, qdec, kdec, tile_dec = _ret_tables(TR_RET)
    bases, offs = _rotary_tables(T, ta)
    proj_tile = lambda b, t: (b, jnp.minimum(t, n_t - 1), 0)
    mix_tile = lambda b, t: (b, jnp.maximum(t - 1, 0), 0)
    return pl.pallas_call(
        functools.partial(_layer0_kernel, tile_dec),
        grid=(B, n_t + 1),
        in_specs=[
            pl.BlockSpec((1, ta, D), proj_tile),
            _const_spec((1, D)),
            _const_spec((D, AB_IN_W)),
            pl.BlockSpec((1, 2, RET_QK_DIM), lambda b, t: (jnp.minimum(t, n_t - 1), 0, 0)),
            _const_spec(offs.shape),
            _const_spec(g.shape),
            _const_spec(dmat.shape), _const_spec(qdec.shape), _const_spec(kdec.shape),
        ],
        out_specs=[pl.BlockSpec((1, ta, ATT_W), mix_tile),
                   pl.BlockSpec((1, ta, RET_V_W), mix_tile)],
        out_shape=[jax.ShapeDtypeStruct((B, T, ATT_W), BF16),
                   jax.ShapeDtypeStruct((B, T, RET_V_W), BF16)],
        scratch_shapes=[pltpu.VMEM((ta, AB_IN_W), BF16),
                        pltpu.VMEM((ta, AB_IN_W), BF16),
                        pltpu.VMEM((hist + ta, ATT_W), BF16),
                        pltpu.VMEM((hist + ta, ATT_W), BF16),
                        pltpu.VMEM((ta // QB_ATT, ATT_HEADS, QB_ATT, QB_ATT + hist), F32),
                        pltpu.VMEM((RET_HEADS, RET_QK_DIM, RET_V_DIM), F32)],
        compiler_params=_params(2),
        name="layer0_mixers",
    )(x, norm_g, w_in, bases, offs, g, dmat, qdec, kdec)


def _ffn_kernel(final_norm, mixer_proj, *refs):
    if mixer_proj:
        x_ref, ya_ref, yb_ref, wmix_ref = refs[:4]
        refs = refs[4:]
    else:
        x_ref = refs[0]
        refs = refs[1:]
    (g_ref, wup_ref, cw_ref, cb_ref, wdn_ref, fg_ref, o_ref,
     hn_ref, zs_ref, zcarry_ref, act_ref, ys_ref) = refs
    tm = x_ref.shape[1]
    pad = SUBLANES
    n_slab = FFN_CB // LANES
    half_rows = tm // 2
    n_lane_slabs = D_MODEL // LANES

    @pl.when(pl.program_id(1) == 0)
    def _():
        zcarry_ref[...] = jnp.zeros(zcarry_ref.shape, F32)

    if mixer_proj:
        na = ya_ref.shape[2]
        mix = _dot(ya_ref[0], wmix_ref[0:na, :]) + _dot(yb_ref[0], wmix_ref[na:, :])
        for l in range(n_lane_slabs):
            ys_ref[l] = mix[:, l * LANES:(l + 1) * LANES]
    for r0 in range(0, tm, NORM_ROWS):
        rows = slice(r0, r0 + NORM_ROWS)
        xs = x_ref[0, rows, :]
        if mixer_proj:
            xs = xs + jnp.concatenate([ys_ref[l, rows, :] for l in range(n_lane_slabs)], axis=1)
            o_ref[0, rows, :] = xs
        hn_ref[rows, :] = _rms_norm(xs, g_ref[...]).astype(BF16)
    lhs = hn_ref[...]
    n_blocks = FFN_HIDDEN // FFN_CB

    def up_block(j):
        for half in range(2):
            c0 = half * FFN_HIDDEN + j * FFN_CB
            z = _dot(lhs, wup_ref[:, c0:c0 + FFN_CB])
            for l in range(n_slab):
                slab = c0 // LANES + l
                zslab = z[:, l * LANES:(l + 1) * LANES]
                zs_ref[j % 2, half * n_slab + l, pad - SUBLANES:pad, :] = zcarry_ref[slab]
                zs_ref[j % 2, half * n_slab + l, pad:pad + tm, :] = zslab
                zcarry_ref[slab] = zslab[tm - SUBLANES:, :]

    up_block(0)
    for j in range(n_blocks):
        buf = j % 2
        if j + 1 < n_blocks:
            up_block(j + 1)
        for par in range(2):
            for l in range(n_slab):
                for u0 in range(0, half_rows, CONV_ROWS):
                    conv = []
                    for half in range(2):
                        c0 = half * FFN_HIDDEN + j * FFN_CB + l * LANES
                        c = cb_ref[:, c0:c0 + LANES]
                        for tap in range(CONV_WIDTH):
                            off = pad - (CONV_WIDTH - 1) + tap + par + 2 * u0
                            c = c + (zs_ref[buf, half * n_slab + l,
                                            pl.ds(off, CONV_ROWS, stride=2), :]
                                     * cw_ref[tap:tap + 1, c0:c0 + LANES])
                        conv.append(c)
                    a0 = j * FFN_CB + l * LANES
                    act_ref[par * half_rows + u0:par * half_rows + u0 + CONV_ROWS,
                            a0:a0 + LANES] = (_gelu_tanh(conv[0]) * conv[1]).astype(BF16)
    yp = _dot(act_ref[...], wdn_ref[...])
    for l in range(n_lane_slabs):
        for par in range(2):
            ys_ref[l, pl.ds(par, half_rows, stride=2), :] = (
                yp[par * half_rows:(par + 1) * half_rows, l * LANES:(l + 1) * LANES])
    for r0 in range(0, tm, NORM_ROWS):
        rows = slice(r0, r0 + NORM_ROWS)
        y = (o_ref if mixer_proj else x_ref)[0, rows, :] + jnp.concatenate(
            [ys_ref[l, rows, :] for l in range(n_lane_slabs)], axis=1)
        if final_norm:
            y = _rms_norm(y, fg_ref[...])
        o_ref[0, rows, :] = y


def _layer_spec(layer, shape):
    zeros = (0,) * len(shape)
    return pl.BlockSpec((None,) + tuple(shape), lambda *_: (layer,) + zeros,
                        pipeline_mode=pl.Buffered(1))


def _ffn(x, layer, g, w_up, conv_w, conv_b, w_down, final_g=None, mixer=None):
    B, T, D = x.shape
    tm = TM_FFN
    final_norm = final_g is not None
    if final_g is None:
        final_g = jnp.ones((D,), F32)
    tile = lambda width: pl.BlockSpec((1, tm, width), lambda b, t: (b, t, 0))
    mixer_args, mixer_specs = (), []
    if mixer is not None:
        ya, yb, w_mix = mixer
        mixer_args = (ya, yb, w_mix)
        mixer_specs = [tile(ya.shape[2]), tile(yb.shape[2]), _const_spec(w_mix.shape)]
    return pl.pallas_call(
        functools.partial(_ffn_kernel, final_norm, mixer is not None),
        grid=(B, T // tm),
        in_specs=[tile(D)] + mixer_specs + [
            _layer_spec(layer, (1, D)),
            _layer_spec(layer, (D, 2 * FFN_HIDDEN)),
            _layer_spec(layer, (CONV_WIDTH, 2 * FFN_HIDDEN)),
            _layer_spec(layer, (1, 2 * FFN_HIDDEN)),
            _layer_spec(layer, (FFN_HIDDEN, D)),
            _const_spec((1, D)),
        ],
        out_specs=pl.BlockSpec((1, tm, D), lambda b, t: (b, t, 0)),
        out_shape=jax.ShapeDtypeStruct((B, T, D), F32),
        scratch_shapes=[
            pltpu.VMEM((tm, D), BF16),
            pltpu.VMEM((2, 2 * FFN_CB // LANES, tm + SUBLANES, LANES), F32),
            pltpu.VMEM((2 * FFN_HIDDEN // LANES, SUBLANES, LANES), F32),
            pltpu.VMEM((tm, FFN_HIDDEN), BF16),
            pltpu.VMEM((D // LANES, tm, LANES), F32),
        ],
        compiler_params=_params(2),
        name="conv_ffn_final" if final_norm else "conv_ffn",
    )(x, *mixer_args, g, w_up, conv_w, conv_b, w_down, final_g[None, :])


def _sgu_kernel(x_ref, g_ref, win_ref, lng_ref, lnb_ref, ws_ref, bs_ref, wout_ref, o_ref,
                hn_ref, v_ref, u_ref, y_ref):
    tm = x_ref.shape[1]
    gw = SGU_WIDTH // SGU_GROUPS
    pair = 2 * SGU_BLOCK
    for r0 in range(0, tm, NORM_ROWS):
        rows = slice(r0, r0 + NORM_ROWS)
        hn_ref[rows, :] = _rms_norm(x_ref[0, rows, :], g_ref[...]).astype(BF16)
    hn = hn_ref[...]

    vsum = jnp.zeros((tm, 1), F32)
    for j in range(SGU_WIDTH // SGU_CB):
        cols = slice(j * SGU_CB, (j + 1) * SGU_CB)
        zv = _gelu_tanh(_dot(hn, win_ref[:, SGU_WIDTH + j * SGU_CB:SGU_WIDTH + (j + 1) * SGU_CB]))
        v_ref[:, cols] = zv
        vsum = vsum + jnp.sum(zv, axis=-1, keepdims=True)
    for j in range(SGU_WIDTH // SGU_CB):
        cols = slice(j * SGU_CB, (j + 1) * SGU_CB)
        u_ref[:, cols] = _gelu_tanh(_dot(hn, win_ref[:, cols]))
    mu = vsum * (1.0 / SGU_WIDTH)
    vsq = jnp.zeros((tm, 1), F32)
    for j in range(SGU_WIDTH // SGU_CB):
        cols = slice(j * SGU_CB, (j + 1) * SGU_CB)
        vc = v_ref[:, cols] - mu
        vsq = vsq + jnp.sum(vc * vc, axis=-1, keepdims=True)
    rstd = lax.rsqrt(vsq * (1.0 / SGU_WIDTH) + EPS)

    ri = lax.broadcasted_iota(jnp.int32, (pair, pair), 0)
    ci = lax.broadcasted_iota(jnp.int32, (pair, pair), 1)
    allowed = (ci // CHUNK) <= (ri // CHUNK)
    same = (ci // SGU_BLOCK) == (ri // SGU_BLOCK)
    keep = jnp.logical_and(allowed, same)
    for gidx in range(SGU_GROUPS):
        cols = slice(gidx * gw, (gidx + 1) * gw)
        w = jnp.where(keep, ws_ref[gidx], jnp.zeros((pair, pair), BF16))
        vn = ((v_ref[:, cols] - mu) * rstd * lng_ref[:, cols] + lnb_ref[:, cols]).astype(BF16)
        for rp in range(tm // pair):
            rows = slice(rp * pair, (rp + 1) * pair)
            mixed = _dot(w, vn[rows]) + bs_ref[:, cols]
            y_ref[rows, cols] = (u_ref[rows, cols] * mixed).astype(BF16)
    o_ref[0] = x_ref[0] + _dot(y_ref[...], wout_ref[...])


def _sgu(x, g, w_in, ln_g, ln_b, w_s, b_s, w_out):
    B, T, D = x.shape
    tm = TM_PROJ
    gw = SGU_WIDTH // SGU_GROUPS
    pair = 2 * SGU_BLOCK
    zero = jnp.zeros_like(w_s)
    w_bd = jnp.concatenate([jnp.concatenate([w_s, zero], axis=2),
                            jnp.concatenate([zero, w_s], axis=2)], axis=1).astype(BF16)
    b_tab = jnp.repeat(b_s.T, gw, axis=1)
    b_tab = jnp.concatenate([b_tab, b_tab], axis=0)
    return pl.pallas_call(
        _sgu_kernel,
        grid=(B, T // tm),
        in_specs=[
            pl.BlockSpec((1, tm, D), lambda b, t: (b, t, 0)),
            _const_spec((1, D)),
            _const_spec((D, 2 * SGU_WIDTH)),
            _const_spec((1, SGU_WIDTH)),
            _const_spec((1, SGU_WIDTH)),
            _const_spec((SGU_GROUPS, pair, pair)),
            _const_spec((pair, SGU_WIDTH)),
            _const_spec((SGU_WIDTH, D)),
        ],
        out_specs=pl.BlockSpec((1, tm, D), lambda b, t: (b, t, 0)),
        out_shape=jax.ShapeDtypeStruct((B, T, D), F32),
        scratch_shapes=[
            pltpu.VMEM((tm, D), BF16),
            pltpu.VMEM((tm, SGU_WIDTH), F32),
            pltpu.VMEM((tm, SGU_WIDTH), F32),
            pltpu.VMEM((tm, SGU_WIDTH), BF16),
        ],
        compiler_params=_params(2),
        name="sgu_mixer",
    )(x, g[None, :], w_in.astype(BF16), ln_g[None, :], ln_b[None, :], w_bd, b_tab,
      w_out.astype(BF16))


def kernel(x, attn_norm_g, ffn_norm_g, ab_w_in, ab_w_out, ab_rel_bias, c_w_in, c_ln_g, c_ln_b,
           c_w_s, c_b_s, c_w_out, ffn_w_up, ffn_conv_w, ffn_conv_b, ffn_w_down, final_norm_g):
    ffn_params = (ffn_norm_g[:, None, :], ffn_w_up.astype(BF16), ffn_conv_w,
                  ffn_conv_b[:, None, :], ffn_w_down.astype(BF16))
    yb, ya = _layer0_mixers(x, attn_norm_g[0][None, :], ab_w_in[0].astype(BF16), ab_rel_bias[0])
    h = _ffn(x, 0, *ffn_params, mixer=(ya, yb, ab_w_out[0].astype(BF16)))
    h = _sgu(h, attn_norm_g[1], c_w_in[0], c_ln_g[0], c_ln_b[0], c_w_s[0], c_b_s[0], c_w_out[0])
    h = _ffn(h, 1, *ffn_params, final_g=final_norm_g)
    return h
```

```python
import functools
import math

import numpy as np
import jax
import jax.numpy as jnp
from jax import lax
from jax.experimental import pallas as pl
from jax.experimental.pallas import tpu as pltpu

F32 = jnp.float32
BF16 = jnp.bfloat16

D_MODEL = 1024
CHUNK = 64
EPS = 1e-6
NEG_INF = -1e30
LOG2E = math.log2(math.e)

RET_HEADS = 4
RET_QK_DIM = 128
RET_V_DIM = 256
ATT_HEADS = 8
ATT_HEAD_DIM = 64
ATT_PAST_CHUNKS = 8
MAX_REL = 128
SGU_BLOCK = 128
SGU_GROUPS = 8
SGU_WIDTH = 2048
FFN_HIDDEN = 2816
CONV_WIDTH = 3

RET_QK_W = RET_HEADS * RET_QK_DIM
RET_V_W = RET_HEADS * RET_V_DIM
ATT_W = ATT_HEADS * ATT_HEAD_DIM
AB_IN_W = 2 * RET_QK_W + 2 * RET_V_W + 3 * ATT_W
AB_OUT_W = RET_V_W + ATT_W
Z_RET_Q = 0
Z_RET_K = Z_RET_Q + RET_QK_W
Z_RET_V = Z_RET_K + RET_QK_W
Z_RET_G = Z_RET_V + RET_V_W
Z_ATT_Q = Z_RET_G + RET_V_W
Z_ATT_K = Z_ATT_Q + ATT_W
Z_ATT_V = Z_ATT_K + ATT_W

VMEM_LIMIT_BYTES = 56 * 1024 * 1024
SUBLANES = 8
BF16_SUBLANES = 16
LANES = 128

TM_PROJ = 1024
TM_FFN = 512
TR_RET = 256
TA_ATT = 512
QB_ATT = 256
FFN_CB = 256
IN_CB = 256
SGU_CB = 256
NORM_ROWS = 32
CONV_ROWS = 64


def _const_spec(shape):
    zeros = (0,) * len(shape)
    return pl.BlockSpec(shape, lambda *_: zeros, pipeline_mode=pl.Buffered(1))


def _params(n_axes, flags=None):
    return pltpu.CompilerParams(
        dimension_semantics=("arbitrary",) * n_axes,
        vmem_limit_bytes=VMEM_LIMIT_BYTES,
        flags=flags)


def _rms_norm(x, g):
    return x * lax.rsqrt(jnp.mean(x * x, axis=-1, keepdims=True) + EPS) * g


def _gelu_tanh(x):
    c = math.sqrt(2.0 / math.pi)
    return 0.5 * x * (1.0 + jnp.tanh(c * (x + 0.044715 * (x * x * x))))


def _dot(a, b):
    return jnp.dot(a, b, preferred_element_type=F32)


def _dot_nt(a, b):
    return lax.dot_general(a, b, (((1,), (1,)), ((), ())), preferred_element_type=F32)


def _dot_tn(a, b):
    return lax.dot_general(a, b, (((0,), (0,)), ((), ())), preferred_element_type=F32)


def _inproj_blocks(x_ref, g_ref, w_ref, base_ref, off_ref, zput):
    hn = _rms_norm(x_ref[0], g_ref[...]).astype(BF16)
    ca, sa = base_ref[0, 0:1, :], base_ref[0, 1:2, :]
    cos = ca * off_ref[0] - sa * off_ref[1]
    sin = sa * off_ref[2] + ca * off_ref[3]

    def block(j):
        z = _dot(hn, w_ref[:, j * IN_CB:(j + 1) * IN_CB])
        if j * IN_CB < Z_RET_V:
            parts = []
            for h in range(IN_CB // RET_QK_DIM):
                xh = z[:, h * RET_QK_DIM:(h + 1) * RET_QK_DIM]
                parts.append(xh * cos + pltpu.roll(xh, RET_QK_DIM // 2, axis=1) * sin)
            z = jnp.concatenate(parts, axis=1)
        elif Z_ATT_Q <= j * IN_CB < Z_ATT_K:
            z = z * (ATT_HEAD_DIM ** -0.5 * LOG2E)
        zput(j * IN_CB, z.astype(BF16))

    return [functools.partial(block, j) for j in range(AB_IN_W // IN_CB)]


def _rotary_tables(T, tm):
    half = RET_QK_DIM // 2
    inv = 1.0 / (np.float32(10000.0) ** np.linspace(0.0, 1.0, half, dtype=np.float32))
    inv = np.concatenate([inv, inv]).astype(np.float64)
    sign = np.concatenate([-np.ones(half), np.ones(half)])
    base = (np.arange(T // tm) * tm)[:, None] * inv[None, :]
    off = np.arange(tm)[:, None] * inv[None, :]
    bases = np.stack([np.cos(base), np.sin(base)], axis=1)
    offs = np.stack([np.cos(off), np.sin(off), sign * np.cos(off), sign * np.sin(off)])
    return jnp.asarray(bases, F32), jnp.asarray(offs, F32)


def _ret_tables(tr):
    h = np.arange(RET_HEADS, dtype=np.float64)
    log_g = np.log1p(-np.exp2(-5.0 - h))
    n = np.arange(tr)
    cn, cm = n[:, None] // CHUNK, n[None, :] // CHUNK
    diff = (n[:, None] - n[None, :]).astype(np.float64)
    expo = np.where(cn == cm, np.abs(diff), diff)
    dmat = np.where(cm <= cn, np.exp(log_g[:, None, None] * expo), 0.0)
    scale = RET_QK_DIM ** -0.5
    qdec = np.exp(log_g[:, None] * (n + 1.0)[None, :])
    kdec = np.exp(log_g[:, None] * (tr - 1.0 - n)[None, :]) * scale
    qdec = np.broadcast_to(qdec[:, :, None], (RET_HEADS, tr, RET_QK_DIM))
    kdec = np.broadcast_to(kdec[:, :, None], (RET_HEADS, tr, RET_QK_DIM))
    tile_dec = [float(v) for v in np.exp(log_g * tr)]
    return (jnp.asarray(dmat * scale, F32), jnp.asarray(qdec, F32), jnp.asarray(kdec, F32),
            tile_dec)


def _ret_tile(tile_dec, rows, zget, d_ref, qd_ref, kd_ref, o_ref, state_ref):
    for h in range(RET_HEADS):
        vv = slice(h * RET_V_DIM, (h + 1) * RET_V_DIM)
        q = zget(rows, Z_RET_Q + h * RET_QK_DIM, RET_QK_DIM)
        k = zget(rows, Z_RET_K + h * RET_QK_DIM, RET_QK_DIM)
        v = zget(rows, Z_RET_V + h * RET_V_DIM, RET_V_DIM)
        state = state_ref[h]
        s = _dot_nt(q, k) * d_ref[h]
        qd = (q.astype(F32) * qd_ref[h]).astype(BF16)
        kd = (k.astype(F32) * kd_ref[h]).astype(BF16)
        r = _dot(s.astype(BF16), v) + _dot(qd, state.astype(BF16))
        state_ref[h] = state * tile_dec[h] + _dot_tn(kd, v)
        mu = jnp.mean(r, axis=-1, keepdims=True)
        rc = r - mu
        var = jnp.mean(rc * rc, axis=-1, keepdims=True)
        rn = rc * lax.rsqrt(var + EPS)
        gate = zget(rows, Z_RET_G + h * RET_V_DIM, RET_V_DIM).astype(F32)
        o_ref[0, rows, vv] = (0.5 * gate * (1.0 + jnp.tanh(0.5 * gate)) * rn).astype(BF16)


def _att_bias_vectors(rel_bias):
    hist = ATT_PAST_CHUNKS * CHUNK
    kw = QB_ATT + hist
    nr = 2 * MAX_REL + 1
    period = 2 * hist
    assert period >= kw + QB_ATT - 1 and hist > MAX_REL
    n_far = hist - MAX_REL + 1
    n_near = kw - (hist + MAX_REL)
    n_wrap = period - kw
    first, last = rel_bias[:, 0:1], rel_bias[:, nr - 1:nr]
    return jnp.concatenate([
        jnp.broadcast_to(last, (ATT_HEADS, n_far)),
        rel_bias[:, nr - 2:0:-1],
        jnp.broadcast_to(first, (ATT_HEADS, n_near)),
        jnp.broadcast_to(last, (ATT_HEADS, n_wrap))], axis=1).astype(F32)


def _layer0_kernel(tile_dec, x_ref, ng_ref, w_ref, base_ref, off_ref, g_ref, d_ref, qd_ref,
                   kd_ref, o_ref, ro_ref, zbuf0, zbuf1, kbuf, vbuf, bias_ref, state_ref):
    t = pl.program_id(1)
    ta = TA_ATT
    hist = ATT_PAST_CHUNKS * CHUNK
    kw = QB_ATT + hist
    period = g_ref.shape[1]
    n_qb = ta // QB_ATT
    tile = jnp.maximum(t - 1, 0)

    @pl.when(t == 0)
    def _():
        zbuf1[...] = jnp.zeros(zbuf1.shape, BF16)

    @pl.when(t <= 1)
    def _():
        kbuf[0:hist, :] = jnp.zeros((hist, ATT_W), BF16)
        vbuf[0:hist, :] = jnp.zeros((hist, ATT_W), BF16)
        state_ref[...] = jnp.zeros_like(state_ref)

    @pl.when(t <= 2)
    def _():
        ci = lax.broadcasted_iota(jnp.int32, (QB_ATT, kw), 0) // CHUNK
        col = lax.broadcasted_iota(jnp.int32, (QB_ATT, kw), 1)
        cj = col // CHUNK
        in_band = jnp.logical_and(cj >= ci, cj <= ci + ATT_PAST_CHUNKS)
        for qb in range(n_qb):
            valid = jnp.logical_and(in_band, col >= hist - qb * QB_ATT - tile * ta)
            for h in range(ATT_HEADS):
                rows = jnp.broadcast_to(g_ref[h:h + 1, :], (QB_ATT, period))
                skew = pltpu.roll(rows, 0, axis=1, stride=1, stride_axis=0)
                bias_ref[qb, h] = jnp.where(valid, skew[:, :kw] * LOG2E, NEG_INF)

    def step(wbuf, rbuf):
        def zput(c0, block):
            wbuf[:, c0:c0 + block.shape[1]] = block

        def zget(rows, c0, width):
            return rbuf[rows, c0:c0 + width]

        pending = _inproj_blocks(x_ref, ng_ref, w_ref, base_ref, off_ref, zput)

        def issue_projection_block():
            if pending:
                pending.pop(0)()

        kbuf[hist:hist + ta, :] = zget(slice(0, ta), Z_ATT_K, ATT_W)
        vbuf[hist:hist + ta, :] = zget(slice(0, ta), Z_ATT_V, ATT_W)

        lane = lax.broadcasted_iota(jnp.int32, (1, 2 * ATT_HEAD_DIM), 1)
        low = lane < ATT_HEAD_DIM
        for qb in range(n_qb):
            rows = slice(qb * QB_ATT, (qb + 1) * QB_ATT)
            krows = slice(qb * QB_ATT, qb * QB_ATT + kw)
            for hp in range(ATT_HEADS // 2):
                lanes = slice(hp * 2 * ATT_HEAD_DIM, (hp + 1) * 2 * ATT_HEAD_DIM)
                qp = zget(rows, Z_ATT_Q + lanes.start, 2 * ATT_HEAD_DIM)
                kp = kbuf[krows, lanes]
                vp = vbuf[krows, lanes]
                outs = []
                for e in range(2):
                    keep = low if e == 0 else jnp.logical_not(low)
                    qm = jnp.where(keep, qp, jnp.zeros_like(qp))
                    s = _dot_nt(qm, kp)
                    issue_projection_block()
                    s = s + bias_ref[qb, 2 * hp + e]
                    m = jnp.max(s, axis=-1, keepdims=True)
                    p = jnp.exp2(s - m).astype(BF16)
                    ov = _dot(p, jnp.where(keep, vp, jnp.ones_like(vp)))
                    outs.append(ov / pltpu.roll(ov, ATT_HEAD_DIM, axis=1))
                o_ref[0, rows, lanes] = jnp.where(low, outs[0], outs[1]).astype(BF16)
            for r0 in range(qb * QB_ATT, (qb + 1) * QB_ATT, TR_RET):
                _ret_tile(tile_dec, slice(r0, r0 + TR_RET), zget, d_ref, qd_ref, kd_ref,
                          ro_ref, state_ref)
        while pending:
            issue_projection_block()

        kbuf[0:hist, :] = kbuf[ta:ta + hist, :]
        vbuf[0:hist, :] = vbuf[ta:ta + hist, :]

    @pl.when(t % 2 == 0)
    def _():
        step(zbuf0, zbuf1)

    @pl.when(t % 2 == 1)
    def _():
        step(zbuf1, zbuf0)


def _layer0_mixers(x, norm_g, w_in, rel_bias):
    B, T, D = x.shape
    ta = TA_ATT
    n_t = T // ta
    hist = ATT_PAST_CHUNKS * CHUNK
    assert ta == hist and n_t >= 2 and QB_ATT % TR_RET == 0 and AB_IN_W % IN_CB == 0
    g = _att_bias_vectors(rel_bias)
    dmat, qdec, kdec, tile_dec = _ret_tables(TR_RET)
    bases, offs = _rotary_tables(T, ta)
    proj_tile = lambda b, t: (b, jnp.minimum(t, n_t - 1), 0)
    mix_tile = lambda b, t: (b, jnp.maximum(t - 1, 0), 0)
    return pl.pallas_call(
        functools.partial(_layer0_kernel, tile_dec),
        grid=(B, n_t + 1),
        in_specs=[
            pl.BlockSpec((1, ta, D), proj_tile),
            _const_spec((1, D)),
            _const_spec((D, AB_IN_W)),
            pl.BlockSpec((1, 2, RET_QK_DIM), lambda b, t: (jnp.minimum(t, n_t - 1), 0, 0)),
            _const_spec(offs.shape),
            _const_spec(g.shape),
            _const_spec(dmat.shape), _const_spec(qdec.shape), _const_spec(kdec.shape),
        ],
        out_specs=[pl.BlockSpec((1, ta, ATT_W), mix_tile),
                   pl.BlockSpec((1, ta, RET_V_W), mix_tile)],
        out_shape=[jax.ShapeDtypeStruct((B, T, ATT_W), BF16),
                   jax.ShapeDtypeStruct((B, T, RET_V_W), BF16)],
        scratch_shapes=[pltpu.VMEM((ta, AB_IN_W), BF16),
                        pltpu.VMEM((ta, AB_IN_W), BF16),
                        pltpu.VMEM((hist + ta, ATT_W), BF16),
                        pltpu.VMEM((hist + ta, ATT_W), BF16),
                        pltpu.VMEM((ta // QB_ATT, ATT_HEADS, QB_ATT, QB_ATT + hist), F32),
                        pltpu.VMEM((RET_HEADS, RET_QK_DIM, RET_V_DIM), F32)],
        compiler_params=_params(2),
        name="layer0_mixers",
    )(x, norm_g, w_in, bases, offs, g, dmat, qdec, kdec)


def _ffn_kernel(final_norm, mixer_proj, *refs):
    if mixer_proj:
        x_ref, ya_ref, yb_ref, wmix_ref = refs[:4]
        refs = refs[4:]
    else:
        x_ref = refs[0]
        refs = refs[1:]
    (g_ref, wup_ref, cw_ref, cb_ref, wdn_ref, fg_ref, o_ref,
     hn_ref, zs_ref, zcarry_ref, act_ref, ys_ref) = refs
    tm = x_ref.shape[1]
    pad = SUBLANES
    n_slab = FFN_CB // LANES
    half_rows = tm // 2
    n_lane_slabs = D_MODEL // LANES

    @pl.when(pl.program_id(1) == 0)
    def _():
        zcarry_ref[...] = jnp.zeros(zcarry_ref.shape, F32)

    if mixer_proj:
        na = ya_ref.shape[2]
        mix = _dot(ya_ref[0], wmix_ref[0:na, :]) + _dot(yb_ref[0], wmix_ref[na:, :])
        for l in range(n_lane_slabs):
            ys_ref[l] = mix[:, l * LANES:(l + 1) * LANES]
    for r0 in range(0, tm, NORM_ROWS):
        rows = slice(r0, r0 + NORM_ROWS)
        xs = x_ref[0, rows, :]
        if mixer_proj:
            xs = xs + jnp.concatenate([ys_ref[l, rows, :] for l in range(n_lane_slabs)], axis=1)
            o_ref[0, rows, :] = xs
        hn_ref[rows, :] = _rms_norm(xs, g_ref[...]).astype(BF16)
    lhs = hn_ref[...]
    n_blocks = FFN_HIDDEN // FFN_CB

    def up_block(j):
        for half in range(2):
            c0 = half * FFN_HIDDEN + j * FFN_CB
            z = _dot(lhs, wup_ref[:, c0:c0 + FFN_CB])
            for l in range(n_slab):
                slab = c0 // LANES + l
                zslab = z[:, l * LANES:(l + 1) * LANES]
                zs_ref[j % 2, half * n_slab + l, pad - SUBLANES:pad, :] = zcarry_ref[slab]
                zs_ref[j % 2, half * n_slab + l, pad:pad + tm, :] = zslab
                zcarry_ref[slab] = zslab[tm - SUBLANES:, :]

    up_block(0)
    for j in range(n_blocks):
        buf = j % 2
        if j + 1 < n_blocks:
            up_block(j + 1)
        for par in range(2):
            for l in range(n_slab):
                for u0 in range(0, half_rows, CONV_ROWS):
                    conv = []
                    for half in range(2):
                        c0 = half * FFN_HIDDEN + j * FFN_CB + l * LANES
                        c = cb_ref[:, c0:c0 + LANES]
                        for tap in range(CONV_WIDTH):
                            off = pad - (CONV_WIDTH - 1) + tap + par + 2 * u0
                            c = c + (zs_ref[buf, half * n_slab + l,
                                            pl.ds(off, CONV_ROWS, stride=2), :]
                                     * cw_ref[tap:tap + 1, c0:c0 + LANES])
                        conv.append(c)
                    a0 = j * FFN_CB + l * LANES
                    act_ref[par * half_rows + u0:par * half_rows + u0 + CONV_ROWS,
                            a0:a0 + LANES] = (_gelu_tanh(conv[0]) * conv[1]).astype(BF16)
    yp = _dot(act_ref[...], wdn_ref[...])
    for l in range(n_lane_slabs):
        for par in range(2):
            ys_ref[l, pl.ds(par, half_rows, stride=2), :] = (
                yp[par * half_rows:(par + 1) * half_rows, l * LANES:(l + 1) * LANES])
    for r0 in range(0, tm, NORM_ROWS):
        rows = slice(r0, r0 + NORM_ROWS)
        y = (o_ref if mixer_proj else x_ref)[0, rows, :] + jnp.concatenate(
            [ys_ref[l, rows, :] for l in range(n_lane_slabs)], axis=1)
        if final_norm:
            y = _rms_norm(y, fg_ref[...])
        o_ref[0, rows, :] = y


def _layer_spec(layer, shape):
    zeros = (0,) * len(shape)
    return pl.BlockSpec((None,) + tuple(shape), lambda *_: (layer,) + zeros,
                        pipeline_mode=pl.Buffered(1))


def _ffn(x, layer, g, w_up, conv_w, conv_b, w_down, final_g=None, mixer=None):
    B, T, D = x.shape
    tm = TM_FFN
    final_norm = final_g is not None
    if final_g is None:
        final_g = jnp.ones((D,), F32)
    tile = lambda width: pl.BlockSpec((1, tm, width), lambda b, t: (b, t, 0))
    mixer_args, mixer_specs = (), []
    if mixer is not None:
        ya, yb, w_mix = mixer
        mixer_args = (ya, yb, w_mix)
        mixer_specs = [tile(ya.shape[2]), tile(yb.shape[2]), _const_spec(w_mix.shape)]
    return pl.pallas_call(
        functools.partial(_ffn_kernel, final_norm, mixer is not None),
        grid=(B, T // tm),
        in_specs=[tile(D)] + mixer_specs + [
            _layer_spec(layer, (1, D)),
            _layer_spec(layer, (D, 2 * FFN_HIDDEN)),
            _layer_spec(layer, (CONV_WIDTH, 2 * FFN_HIDDEN)),
            _layer_spec(layer, (1, 2 * FFN_HIDDEN)),
            _layer_spec(layer, (FFN_HIDDEN, D)),
            _const_spec((1, D)),
        ],
        out_specs=pl.BlockSpec((1, tm, D), lambda b, t: (b, t, 0)),
        out_shape=jax.ShapeDtypeStruct((B, T, D), F32),
        scratch_shapes=[
            pltpu.VMEM((tm, D), BF16),
            pltpu.VMEM((2, 2 * FFN_CB // LANES, tm + SUBLANES, LANES), F32),
            pltpu.VMEM((2 * FFN_HIDDEN // LANES, SUBLANES, LANES), F32),
            pltpu.VMEM((tm, FFN_HIDDEN), BF16),
            pltpu.VMEM((D // LANES, tm, LANES), F32),
        ],
        compiler_params=_params(2),
        name="conv_ffn_final" if final_norm else "conv_ffn",
    )(x, *mixer_args, g, w_up, conv_w, conv_b, w_down, final_g[None, :])


def _sgu_kernel(x_ref, g_ref, win_ref, lng_ref, lnb_ref, ws_ref, bs_ref, wout_ref, o_ref,
                hn_ref, v_ref, u_ref, y_ref):
    tm = x_ref.shape[1]
    gw = SGU_WIDTH // SGU_GROUPS
    pair = 2 * SGU_BLOCK
    for r0 in range(0, tm, NORM_ROWS):
        rows = slice(r0, r0 + NORM_ROWS)
        hn_ref[rows, :] = _rms_norm(x_ref[0, rows, :], g_ref[...]).astype(BF16)
    hn = hn_ref[...]

    vsum = jnp.zeros((tm, 1), F32)
    for j in range(SGU_WIDTH // SGU_CB):
        cols = slice(j * SGU_CB, (j + 1) * SGU_CB)
        zv = _gelu_tanh(_dot(hn, win_ref[:, SGU_WIDTH + j * SGU_CB:SGU_WIDTH + (j + 1) * SGU_CB]))
        v_ref[:, cols] = zv
        vsum = vsum + jnp.sum(zv, axis=-1, keepdims=True)
    for j in range(SGU_WIDTH // SGU_CB):
        cols = slice(j * SGU_CB, (j + 1) * SGU_CB)
        u_ref[:, cols] = _gelu_tanh(_dot(hn, win_ref[:, cols]))
    mu = vsum * (1.0 / SGU_WIDTH)
    vsq = jnp.zeros((tm, 1), F32)
    for j in range(SGU_WIDTH // SGU_CB):
        cols = slice(j * SGU_CB, (j + 1) * SGU_CB)
        vc = v_ref[:, cols] - mu
        vsq = vsq + jnp.sum(vc * vc, axis=-1, keepdims=True)
    rstd = lax.rsqrt(vsq * (1.0 / SGU_WIDTH) + EPS)

    ri = lax.broadcasted_iota(jnp.int32, (pair, pair), 0)
    ci = lax.broadcasted_iota(jnp.int32, (pair, pair), 1)
    allowed = (ci // CHUNK) <= (ri // CHUNK)
    same = (ci // SGU_BLOCK) == (ri // SGU_BLOCK)
    keep = jnp.logical_and(allowed, same)
    for gidx in range(SGU_GROUPS):
        cols = slice(gidx * gw, (gidx + 1) * gw)
        w = jnp.where(keep, ws_ref[gidx], jnp.zeros((pair, pair), BF16))
        vn = ((v_ref[:, cols] - mu) * rstd * lng_ref[:, cols] + lnb_ref[:, cols]).astype(BF16)
        for rp in range(tm // pair):
            rows = slice(rp * pair, (rp + 1) * pair)
            mixed = _dot(w, vn[rows]) + bs_ref[:, cols]
            y_ref[rows, cols] = (u_ref[rows, cols] * mixed).astype(BF16)
    o_ref[0] = x_ref[0] + _dot(y_ref[...], wout_ref[...])


def _sgu(x, g, w_in, ln_g, ln_b, w_s, b_s, w_out):
    B, T, D = x.shape
    tm = TM_PROJ
    gw = SGU_WIDTH // SGU_GROUPS
    pair = 2 * SGU_BLOCK
    zero = jnp.zeros_like(w_s)
    w_bd = jnp.concatenate([jnp.concatenate([w_s, zero], axis=2),
                            jnp.concatenate([zero, w_s], axis=2)], axis=1).astype(BF16)
    b_tab = jnp.repeat(b_s.T, gw, axis=1)
    b_tab = jnp.concatenate([b_tab, b_tab], axis=0)
    return pl.pallas_call(
        _sgu_kernel,
        grid=(B, T // tm),
        in_specs=[
            pl.BlockSpec((1, tm, D), lambda b, t: (b, t, 0)),
            _const_spec((1, D)),
            _const_spec((D, 2 * SGU_WIDTH)),
            _const_spec((1, SGU_WIDTH)),
            _const_spec((1, SGU_WIDTH)),
            _const_spec((SGU_GROUPS, pair, pair)),
            _const_spec((pair, SGU_WIDTH)),
            _const_spec((SGU_WIDTH, D)),
        ],
        out_specs=pl.BlockSpec((1, tm, D), lambda b, t: (b, t, 0)),
        out_shape=jax.ShapeDtypeStruct((B, T, D), F32),
        scratch_shapes=[
            pltpu.VMEM((tm, D), BF16),
            pltpu.VMEM((tm, SGU_WIDTH), F32),
            pltpu.VMEM((tm, SGU_WIDTH), F32),
            pltpu.VMEM((tm, SGU_WIDTH), BF16),
        ],
        compiler_params=_params(2),
        name="sgu_mixer",
    )(x, g[None, :], w_in.astype(BF16), ln_g[None, :], ln_b[None, :], w_bd, b_tab,
      w_out.astype(BF16))


def kernel(x, attn_norm_g, ffn_norm_g, ab_w_in, ab_w_out, ab_rel_bias, c_w_in, c_ln_g, c_ln_b,
           c_w_s, c_b_s, c_w_out, ffn_w_up, ffn_conv_w, ffn_conv_b, ffn_w_down, final_norm_g):
    ffn_params = (ffn_norm_g[:, None, :], ffn_w_up.astype(BF16), ffn_conv_w,
                  ffn_conv_b[:, None, :], ffn_w_down.astype(BF16))
    yb, ya = _layer0_mixers(x, attn_norm_g[0][None, :], ab_w_in[0].astype(BF16), ab_rel_bias[0])
    h = _ffn(x, 0, *ffn_params, mixer=(ya, yb, ab_w_out[0].astype(BF16)))
    h = _sgu(h, attn_norm_g[1], c_w_in[0], c_ln_g[0], c_ln_b[0], c_w_s[0], c_b_s[0], c_w_out[0])
    h = _ffn(h, 1, *ffn_params, final_g=final_norm_g)
    return h
```

```python
import functools
import math

import numpy as np
import jax
import jax.numpy as jnp
from jax import lax
from jax.experimental import pallas as pl
from jax.experimental.pallas import tpu as pltpu

F32 = jnp.float32
BF16 = jnp.bfloat16

D_MODEL = 1024
CHUNK = 64
EPS = 1e-6
NEG_INF = -1e30
LOG2E = math.log2(math.e)

RET_HEADS = 4
RET_QK_DIM = 128
RET_V_DIM = 256
ATT_HEADS = 8
ATT_HEAD_DIM = 64
ATT_PAST_CHUNKS = 8
MAX_REL = 128
SGU_BLOCK = 128
SGU_GROUPS = 8
SGU_WIDTH = 2048
FFN_HIDDEN = 2816
CONV_WIDTH = 3

RET_QK_W = RET_HEADS * RET_QK_DIM
RET_V_W = RET_HEADS * RET_V_DIM
ATT_W = ATT_HEADS * ATT_HEAD_DIM
AB_IN_W = 2 * RET_QK_W + 2 * RET_V_W + 3 * ATT_W
AB_OUT_W = RET_V_W + ATT_W
Z_RET_Q = 0
Z_RET_K = Z_RET_Q + RET_QK_W
Z_RET_V = Z_RET_K + RET_QK_W
Z_RET_G = Z_RET_V + RET_V_W
Z_ATT_Q = Z_RET_G + RET_V_W
Z_ATT_K = Z_ATT_Q + ATT_W
Z_ATT_V = Z_ATT_K + ATT_W

VMEM_LIMIT_BYTES = 56 * 1024 * 1024
SUBLANES = 8
BF16_SUBLANES = 16
LANES = 128

TM_PROJ = 1024
TM_FFN = 512
TR_RET = 256
TA_ATT = 512
QB_ATT = 256
FFN_CB = 256
IN_CB = 256
SGU_CB = 256
NORM_ROWS = 32
CONV_ROWS = 64


def _const_spec(shape):
    zeros = (0,) * len(shape)
    return pl.BlockSpec(shape, lambda *_: zeros, pipeline_mode=pl.Buffered(1))


def _params(n_axes, flags=None):
    return pltpu.CompilerParams(
        dimension_semantics=("arbitrary",) * n_axes,
        vmem_limit_bytes=VMEM_LIMIT_BYTES,
        flags=flags)


def _rms_norm(x, g):
    return x * lax.rsqrt(jnp.mean(x * x, axis=-1, keepdims=True) + EPS) * g


def _gelu_tanh(x):
    c = math.sqrt(2.0 / math.pi)
    return 0.5 * x * (1.0 + jnp.tanh(c * (x + 0.044715 * (x * x * x))))


def _dot(a, b):
    return jnp.dot(a, b, preferred_element_type=F32)


def _dot_nt(a, b):
    return lax.dot_general(a, b, (((1,), (1,)), ((), ())), preferred_element_type=F32)


def _dot_tn(a, b):
    return lax.dot_general(a, b, (((0,), (0,)), ((), ())), preferred_element_type=F32)


def _inproj_blocks(x_ref, g_ref, w_ref, base_ref, off_ref, zput):
    hn = _rms_norm(x_ref[0], g_ref[...]).astype(BF16)
    ca, sa = base_ref[0, 0:1, :], base_ref[0, 1:2, :]
    cos = ca * off_ref[0] - sa * off_ref[1]
    sin = sa * off_ref[2] + ca * off_ref[3]

    def block(j):
        z = _dot(hn, w_ref[:, j * IN_CB:(j + 1) * IN_CB])
        if j * IN_CB < Z_RET_V:
            parts = []
            for h in range(IN_CB // RET_QK_DIM):
                xh = z[:, h * RET_QK_DIM:(h + 1) * RET_QK_DIM]
                parts.append(xh * cos + pltpu.roll(xh, RET_QK_DIM // 2, axis=1) * sin)
            z = jnp.concatenate(parts, axis=1)
        elif Z_ATT_Q <= j * IN_CB < Z_ATT_K:
            z = z * (ATT_HEAD_DIM ** -0.5 * LOG2E)
        zput(j * IN_CB, z.astype(BF16))

    return [functools.partial(block, j) for j in range(AB_IN_W // IN_CB)]


def _rotary_tables(T, tm):
    half = RET_QK_DIM // 2
    inv = 1.0 / (np.float32(10000.0) ** np.linspace(0.0, 1.0, half, dtype=np.float32))
    inv = np.concatenate([inv, inv]).astype(np.float64)
    sign = np.concatenate([-np.ones(half), np.ones(half)])
    base = (np.arange(T // tm) * tm)[:, None] * inv[None, :]
    off = np.arange(tm)[:, None] * inv[None, :]
    bases = np.stack([np.cos(base), np.sin(base)], axis=1)
    offs = np.stack([np.cos(off), np.sin(off), sign * np.cos(off), sign * np.sin(off)])
    return jnp.asarray(bases, F32), jnp.asarray(offs, F32)


def _ret_tables(tr):
    h = np.arange(RET_HEADS, dtype=np.float64)
    log_g = np.log1p(-np.exp2(-5.0 - h))
    n = np.arange(tr)
    cn, cm = n[:, None] // CHUNK, n[None, :] // CHUNK
    diff = (n[:, None] - n[None, :]).astype(np.float64)
    expo = np.where(cn == cm, np.abs(diff), diff)
    dmat = np.where(cm <= cn, np.exp(log_g[:, None, None] * expo), 0.0)
    scale = RET_QK_DIM ** -0.5
    qdec = np.exp(log_g[:, None] * (n + 1.0)[None, :])
    kdec = np.exp(log_g[:, None] * (tr - 1.0 - n)[None, :]) * scale
    qdec = np.broadcast_to(qdec[:, :, None], (RET_HEADS, tr, RET_QK_DIM))
    kdec = np.broadcast_to(kdec[:, :, None], (RET_HEADS, tr, RET_QK_DIM))
    tile_dec = [float(v) for v in np.exp(log_g * tr)]
    return (jnp.asarray(dmat * scale, F32), jnp.asarray(qdec, F32), jnp.asarray(kdec, F32),
            tile_dec)


def _ret_tile(tile_dec, rows, zget, d_ref, qd_ref, kd_ref, o_ref, state_ref):
    for h in range(RET_HEADS):
        vv = slice(h * RET_V_DIM, (h + 1) * RET_V_DIM)
        q = zget(rows, Z_RET_Q + h * RET_QK_DIM, RET_QK_DIM)
        k = zget(rows, Z_RET_K + h * RET_QK_DIM, RET_QK_DIM)
        v = zget(rows, Z_RET_V + h * RET_V_DIM, RET_V_DIM)
        state = state_ref[h]
        s = _dot_nt(q, k) * d_ref[h]
        qd = (q.astype(F32) * qd_ref[h]).astype(BF16)
        kd = (k.astype(F32) * kd_ref[h]).astype(BF16)
        r = _dot(s.astype(BF16), v) + _dot(qd, state.astype(BF16))
        state_ref[h] = state * tile_dec[h] + _dot_tn(kd, v)
        mu = jnp.mean(r, axis=-1, keepdims=True)
        rc = r - mu
        var = jnp.mean(rc * rc, axis=-1, keepdims=True)
        rn = rc * lax.rsqrt(var + EPS)
        gate = zget(rows, Z_RET_G + h * RET_V_DIM, RET_V_DIM).astype(F32)
        o_ref[0, rows, vv] = (0.5 * gate * (1.0 + jnp.tanh(0.5 * gate)) * rn).astype(BF16)


def _att_bias_vectors(rel_bias):
    hist = ATT_PAST_CHUNKS * CHUNK
    kw = QB_ATT + hist
    nr = 2 * MAX_REL + 1
    period = 2 * hist
    assert period >= kw + QB_ATT - 1 and hist > MAX_REL
    n_far = hist - MAX_REL + 1
    n_near = kw - (hist + MAX_REL)
    n_wrap = period - kw
    first, last = rel_bias[:, 0:1], rel_bias[:, nr - 1:nr]
    return jnp.concatenate([
        jnp.broadcast_to(last, (ATT_HEADS, n_far)),
        rel_bias[:, nr - 2:0:-1],
        jnp.broadcast_to(first, (ATT_HEADS, n_near)),
        jnp.broadcast_to(last, (ATT_HEADS, n_wrap))], axis=1).astype(F32)


def _layer0_kernel(tile_dec, last_parity, x_ref, ng_ref, w_ref, base_ref, off_ref, g_ref, d_ref,
                   qd_ref, kd_ref, o_ref, ro_ref, zbuf0, zbuf1, kbuf, vbuf, bias_ref, state_ref):
    t = pl.program_id(1)
    ta = TA_ATT
    hist = ATT_PAST_CHUNKS * CHUNK
    kw = QB_ATT + hist
    period = g_ref.shape[1]
    n_qb = ta // QB_ATT
    last = pl.num_programs(1) - 1
    tile = t - 1

    @pl.when(t == 1)
    def _():
        kbuf[0:hist, :] = jnp.zeros((hist, ATT_W), BF16)
        vbuf[0:hist, :] = jnp.zeros((hist, ATT_W), BF16)
        state_ref[...] = jnp.zeros_like(state_ref)

    @pl.when(jnp.logical_and(t >= 1, t <= 2))
    def _():
        ci = lax.broadcasted_iota(jnp.int32, (QB_ATT, kw), 0) // CHUNK
        col = lax.broadcasted_iota(jnp.int32, (QB_ATT, kw), 1)
        cj = col // CHUNK
        in_band = jnp.logical_and(cj >= ci, cj <= ci + ATT_PAST_CHUNKS)
        for qb in range(n_qb):
            valid = jnp.logical_and(in_band, col >= hist - qb * QB_ATT - tile * ta)
            for h in range(ATT_HEADS):
                rows = jnp.broadcast_to(g_ref[h:h + 1, :], (QB_ATT, period))
                skew = pltpu.roll(rows, 0, axis=1, stride=1, stride_axis=0)
                bias_ref[qb, h] = jnp.where(valid, skew[:, :kw] * LOG2E, NEG_INF)

    def step(wbuf, rbuf, project=True, mix=True):
        def zput(c0, block):
            wbuf[:, c0:c0 + block.shape[1]] = block

        def zget(rows, c0, width):
            return rbuf[rows, c0:c0 + width]

        pending = (_inproj_blocks(x_ref, ng_ref, w_ref, base_ref, off_ref, zput)
                   if project else [])

        def issue_projection_block():
            if pending:
                pending.pop(0)()

        if not mix:
            while pending:
                issue_projection_block()
            return

        kbuf[hist:hist + ta, :] = zget(slice(0, ta), Z_ATT_K, ATT_W)
        vbuf[hist:hist + ta, :] = zget(slice(0, ta), Z_ATT_V, ATT_W)

        lane = lax.broadcasted_iota(jnp.int32, (1, 2 * ATT_HEAD_DIM), 1)
        low = lane < ATT_HEAD_DIM
        for qb in range(n_qb):
            rows = slice(qb * QB_ATT, (qb + 1) * QB_ATT)
            krows = slice(qb * QB_ATT, qb * QB_ATT + kw)
            for hp in range(ATT_HEADS // 2):
                lanes = slice(hp * 2 * ATT_HEAD_DIM, (hp + 1) * 2 * ATT_HEAD_DIM)
                qp = zget(rows, Z_ATT_Q + lanes.start, 2 * ATT_HEAD_DIM)
                kp = kbuf[krows, lanes]
                vp = vbuf[krows, lanes]
                outs = []
                for e in range(2):
                    keep = low if e == 0 else jnp.logical_not(low)
                    qm = jnp.where(keep, qp, jnp.zeros_like(qp))
                    s = _dot_nt(qm, kp)
                    issue_projection_block()
                    s = s + bias_ref[qb, 2 * hp + e]
                    m = jnp.max(s, axis=-1, keepdims=True)
                    p = jnp.exp2(s - m).astype(BF16)
                    ov = _dot(p, jnp.where(keep, vp, jnp.ones_like(vp)))
                    outs.append(ov / pltpu.roll(ov, ATT_HEAD_DIM, axis=1))
                o_ref[0, rows, lanes] = jnp.where(low, outs[0], outs[1]).astype(BF16)
            for r0 in range(qb * QB_ATT, (qb + 1) * QB_ATT, TR_RET):
                _ret_tile(tile_dec, slice(r0, r0 + TR_RET), zget, d_ref, qd_ref, kd_ref,
                          ro_ref, state_ref)
        while pending:
            issue_projection_block()

        kbuf[0:hist, :] = kbuf[ta:ta + hist, :]
        vbuf[0:hist, :] = vbuf[ta:ta + hist, :]

    bufs = ((zbuf0, zbuf1), (zbuf1, zbuf0))
    inner = jnp.logical_and(t > 0, t < last)

    @pl.when(t == 0)
    def _():
        step(*bufs[0], mix=False)

    for parity in range(2):
        @pl.when(jnp.logical_and(inner, t % 2 == parity))
        def _():
            step(*bufs[parity])

    @pl.when(t == last)
    def _():
        step(*bufs[last_parity], project=False)


def _layer0_mixers(x, norm_g, w_in, rel_bias):
    B, T, D = x.shape
    ta = TA_ATT
    n_t = T // ta
    hist = ATT_PAST_CHUNKS * CHUNK
    assert ta == hist and n_t >= 2 and QB_ATT % TR_RET == 0 and AB_IN_W % IN_CB == 0
    g = _att_bias_vectors(rel_bias)
    dmat, qdec, kdec, tile_dec = _ret_tables(TR_RET)
    bases, offs = _rotary_tables(T, ta)
    proj_tile = lambda b, t: (b, jnp.minimum(t, n_t - 1), 0)
    mix_tile = lambda b, t: (b, jnp.maximum(t - 1, 0), 0)
    return pl.pallas_call(
        functools.partial(_layer0_kernel, tile_dec, n_t % 2),
        grid=(B, n_t + 1),
        in_specs=[
            pl.BlockSpec((1, ta, D), proj_tile),
            _const_spec((1, D)),
            _const_spec((D, AB_IN_W)),
            pl.BlockSpec((1, 2, RET_QK_DIM), lambda b, t: (jnp.minimum(t, n_t - 1), 0, 0)),
            _const_spec(offs.shape),
            _const_spec(g.shape),
            _const_spec(dmat.shape), _const_spec(qdec.shape), _const_spec(kdec.shape),
        ],
        out_specs=[pl.BlockSpec((1, ta, ATT_W), mix_tile),
                   pl.BlockSpec((1, ta, RET_V_W), mix_tile)],
        out_shape=[jax.ShapeDtypeStruct((B, T, ATT_W), BF16),
                   jax.ShapeDtypeStruct((B, T, RET_V_W), BF16)],
        scratch_shapes=[pltpu.VMEM((ta, AB_IN_W), BF16),
                        pltpu.VMEM((ta, AB_IN_W), BF16),
                        pltpu.VMEM((hist + ta, ATT_W), BF16),
                        pltpu.VMEM((hist + ta, ATT_W), BF16),
                        pltpu.VMEM((ta // QB_ATT, ATT_HEADS, QB_ATT, QB_ATT + hist), F32),
                        pltpu.VMEM((RET_HEADS, RET_QK_DIM, RET_V_DIM), F32)],
        compiler_params=_params(2),
        name="layer0_mixers",
    )(x, norm_g, w_in, bases, offs, g, dmat, qdec, kdec)


def _ffn_kernel(final_norm, mixer_proj, *refs):
    if mixer_proj:
        x_ref, ya_ref, yb_ref, wmix_ref = refs[:4]
        refs = refs[4:]
    else:
        x_ref = refs[0]
        refs = refs[1:]
    (g_ref, wup_ref, cw_ref, cb_ref, wdn_ref, fg_ref, o_ref,
     hn_ref, zs_ref, zcarry_ref, act_ref, ys_ref) = refs
    tm = x_ref.shape[1]
    pad = SUBLANES
    n_slab = FFN_CB // LANES
    half_rows = tm // 2
    n_lane_slabs = D_MODEL // LANES

    @pl.when(pl.program_id(1) == 0)
    def _():
        zcarry_ref[...] = jnp.zeros(zcarry_ref.shape, F32)

    if mixer_proj:
        na = ya_ref.shape[2]
        mix = _dot(ya_ref[0], wmix_ref[0:na, :]) + _dot(yb_ref[0], wmix_ref[na:, :])
        for l in range(n_lane_slabs):
            ys_ref[l] = mix[:, l * LANES:(l + 1) * LANES]
    for r0 in range(0, tm, NORM_ROWS):
        rows = slice(r0, r0 + NORM_ROWS)
        xs = x_ref[0, rows, :]
        if mixer_proj:
            xs = xs + jnp.concatenate([ys_ref[l, rows, :] for l in range(n_lane_slabs)], axis=1)
            o_ref[0, rows, :] = xs
        hn_ref[rows, :] = _rms_norm(xs, g_ref[...]).astype(BF16)
    lhs = hn_ref[...]
    n_blocks = FFN_HIDDEN // FFN_CB

    def up_block(j):
        for half in range(2):
            c0 = half * FFN_HIDDEN + j * FFN_CB
            z = _dot(lhs, wup_ref[:, c0:c0 + FFN_CB])
            for l in range(n_slab):
                slab = c0 // LANES + l
                zslab = z[:, l * LANES:(l + 1) * LANES]
                zs_ref[j % 2, half * n_slab + l, pad - SUBLANES:pad, :] = zcarry_ref[slab]
                zs_ref[j % 2, half * n_slab + l, pad:pad + tm, :] = zslab
                zcarry_ref[slab] = zslab[tm - SUBLANES:, :]

    up_block(0)
    for j in range(n_blocks):
        buf = j % 2
        if j + 1 < n_blocks:
            up_block(j + 1)
        for par in range(2):
            for l in range(n_slab):
                for u0 in range(0, half_rows, CONV_ROWS):
                    conv = []
                    for half in range(2):
                        c0 = half * FFN_HIDDEN + j * FFN_CB + l * LANES
                        c = cb_ref[:, c0:c0 + LANES]
                        for tap in range(CONV_WIDTH):
                            off = pad - (CONV_WIDTH - 1) + tap + par + 2 * u0
                            c = c + (zs_ref[buf, half * n_slab + l,
                                            pl.ds(off, CONV_ROWS, stride=2), :]
                                     * cw_ref[tap:tap + 1, c0:c0 + LANES])
                        conv.append(c)
                    a0 = j * FFN_CB + l * LANES
                    act_ref[par * half_rows + u0:par * half_rows + u0 + CONV_ROWS,
                            a0:a0 + LANES] = (_gelu_tanh(conv[0]) * conv[1]).astype(BF16)
    yp = _dot(act_ref[...], wdn_ref[...])
    for l in range(n_lane_slabs):
        for par in range(2):
            ys_ref[l, pl.ds(par, half_rows, stride=2), :] = (
                yp[par * half_rows:(par + 1) * half_rows, l * LANES:(l + 1) * LANES])
    for r0 in range(0, tm, NORM_ROWS):
        rows = slice(r0, r0 + NORM_ROWS)
        y = (o_ref if mixer_proj else x_ref)[0, rows, :] + jnp.concatenate(
            [ys_ref[l, rows, :] for l in range(n_lane_slabs)], axis=1)
        if final_norm:
            y = _rms_norm(y, fg_ref[...])
        o_ref[0, rows, :] = y


def _layer_spec(layer, shape):
    zeros = (0,) * len(shape)
    return pl.BlockSpec((None,) + tuple(shape), lambda *_: (layer,) + zeros,
                        pipeline_mode=pl.Buffered(1))


def _ffn(x, layer, g, w_up, conv_w, conv_b, w_down, final_g=None, mixer=None):
    B, T, D = x.shape
    tm = TM_FFN
    final_norm = final_g is not None
    if final_g is None:
        final_g = jnp.ones((D,), F32)
    tile = lambda width: pl.BlockSpec((1, tm, width), lambda b, t: (b, t, 0))
    mixer_args, mixer_specs = (), []
    if mixer is not None:
        ya, yb, w_mix = mixer
        mixer_args = (ya, yb, w_mix)
        mixer_specs = [tile(ya.shape[2]), tile(yb.shape[2]), _const_spec(w_mix.shape)]
    return pl.pallas_call(
        functools.partial(_ffn_kernel, final_norm, mixer is not None),
        grid=(B, T // tm),
        in_specs=[tile(D)] + mixer_specs + [
            _layer_spec(layer, (1, D)),
            _layer_spec(layer, (D, 2 * FFN_HIDDEN)),
            _layer_spec(layer, (CONV_WIDTH, 2 * FFN_HIDDEN)),
            _layer_spec(layer, (1, 2 * FFN_HIDDEN)),
            _layer_spec(layer, (FFN_HIDDEN, D)),
            _const_spec((1, D)),
        ],
        out_specs=pl.BlockSpec((1, tm, D), lambda b, t: (b, t, 0)),
        out_shape=jax.ShapeDtypeStruct((B, T, D), F32),
        scratch_shapes=[
            pltpu.VMEM((tm, D), BF16),
            pltpu.VMEM((2, 2 * FFN_CB // LANES, tm + SUBLANES, LANES), F32),
            pltpu.VMEM((2 * FFN_HIDDEN // LANES, SUBLANES, LANES), F32),
            pltpu.VMEM((tm, FFN_HIDDEN), BF16),
            pltpu.VMEM((D // LANES, tm, LANES), F32),
        ],
        compiler_params=_params(2),
        name="conv_ffn_final" if final_norm else "conv_ffn",
    )(x, *mixer_args, g, w_up, conv_w, conv_b, w_down, final_g[None, :])


def _sgu_kernel(x_ref, g_ref, win_ref, lng_ref, lnb_ref, ws_ref, bs_ref, wout_ref, o_ref,
                hn_ref, v_ref, u_ref, y_ref):
    tm = x_ref.shape[1]
    gw = SGU_WIDTH // SGU_GROUPS
    pair = 2 * SGU_BLOCK
    for r0 in range(0, tm, NORM_ROWS):
        rows = slice(r0, r0 + NORM_ROWS)
        hn_ref[rows, :] = _rms_norm(x_ref[0, rows, :], g_ref[...]).astype(BF16)
    hn = hn_ref[...]

    vsum = jnp.zeros((tm, 1), F32)
    for j in range(SGU_WIDTH // SGU_CB):
        cols = slice(j * SGU_CB, (j + 1) * SGU_CB)
        zv = _gelu_tanh(_dot(hn, win_ref[:, SGU_WIDTH + j * SGU_CB:SGU_WIDTH + (j + 1) * SGU_CB]))
        v_ref[:, cols] = zv
        vsum = vsum + jnp.sum(zv, axis=-1, keepdims=True)
    for j in range(SGU_WIDTH // SGU_CB):
        cols = slice(j * SGU_CB, (j + 1) * SGU_CB)
        u_ref[:, cols] = _gelu_tanh(_dot(hn, win_ref[:, cols]))
    mu = vsum * (1.0 / SGU_WIDTH)
    vsq = jnp.zeros((tm, 1), F32)
    for j in range(SGU_WIDTH // SGU_CB):
        cols = slice(j * SGU_CB, (j + 1) * SGU_CB)
        vc = v_ref[:, cols] - mu
        vsq = vsq + jnp.sum(vc * vc, axis=-1, keepdims=True)
    rstd = lax.rsqrt(vsq * (1.0 / SGU_WIDTH) + EPS)

    ri = lax.broadcasted_iota(jnp.int32, (pair, pair), 0)
    ci = lax.broadcasted_iota(jnp.int32, (pair, pair), 1)
    allowed = (ci // CHUNK) <= (ri // CHUNK)
    same = (ci // SGU_BLOCK) == (ri // SGU_BLOCK)
    keep = jnp.logical_and(allowed, same)
    for gidx in range(SGU_GROUPS):
        cols = slice(gidx * gw, (gidx + 1) * gw)
        w = jnp.where(keep, ws_ref[gidx], jnp.zeros((pair, pair), BF16))
        vn = ((v_ref[:, cols] - mu) * rstd * lng_ref[:, cols] + lnb_ref[:, cols]).astype(BF16)
        for rp in range(tm // pair):
            rows = slice(rp * pair, (rp + 1) * pair)
            mixed = _dot(w, vn[rows]) + bs_ref[:, cols]
            y_ref[rows, cols] = (u_ref[rows, cols] * mixed).astype(BF16)
    o_ref[0] = x_ref[0] + _dot(y_ref[...], wout_ref[...])


def _sgu(x, g, w_in, ln_g, ln_b, w_s, b_s, w_out):
    B, T, D = x.shape
    tm = TM_PROJ
    gw = SGU_WIDTH // SGU_GROUPS
    pair = 2 * SGU_BLOCK
    zero = jnp.zeros_like(w_s)
    w_bd = jnp.concatenate([jnp.concatenate([w_s, zero], axis=2),
                            jnp.concatenate([zero, w_s], axis=2)], axis=1).astype(BF16)
    b_tab = jnp.repeat(b_s.T, gw, axis=1)
    b_tab = jnp.concatenate([b_tab, b_tab], axis=0)
    return pl.pallas_call(
        _sgu_kernel,
        grid=(B, T // tm),
        in_specs=[
            pl.BlockSpec((1, tm, D), lambda b, t: (b, t, 0)),
            _const_spec((1, D)),
            _const_spec((D, 2 * SGU_WIDTH)),
            _const_spec((1, SGU_WIDTH)),
            _const_spec((1, SGU_WIDTH)),
            _const_spec((SGU_GROUPS, pair, pair)),
            _const_spec((pair, SGU_WIDTH)),
            _const_spec((SGU_WIDTH, D)),
        ],
        out_specs=pl.BlockSpec((1, tm, D), lambda b, t: (b, t, 0)),
        out_shape=jax.ShapeDtypeStruct((B, T, D), F32),
        scratch_shapes=[
            pltpu.VMEM((tm, D), BF16),
            pltpu.VMEM((tm, SGU_WIDTH), F32),
            pltpu.VMEM((tm, SGU_WIDTH), F32),
            pltpu.VMEM((tm, SGU_WIDTH), BF16),
        ],
        compiler_params=_params(2),
        name="sgu_mixer",
    )(x, g[None, :], w_in.astype(BF16), ln_g[None, :], ln_b[None, :], w_bd, b_tab,
      w_out.astype(BF16))


def kernel(x, attn_norm_g, ffn_norm_g, ab_w_in, ab_w_out, ab_rel_bias, c_w_in, c_ln_g, c_ln_b,
           c_w_s, c_b_s, c_w_out, ffn_w_up, ffn_conv_w, ffn_conv_b, ffn_w_down, final_norm_g):
    ffn_params = (ffn_norm_g[:, None, :], ffn_w_up.astype(BF16), ffn_conv_w,
                  ffn_conv_b[:, None, :], ffn_w_down.astype(BF16))
    yb, ya = _layer0_mixers(x, attn_norm_g[0][None, :], ab_w_in[0].astype(BF16), ab_rel_bias[0])
    h = _ffn(x, 0, *ffn_params, mixer=(ya, yb, ab_w_out[0].astype(BF16)))
    h = _sgu(h, attn_norm_g[1], c_w_in[0], c_ln_g[0], c_ln_b[0], c_w_s[0], c_b_s[0], c_w_out[0])
    h = _ffn(h, 1, *ffn_params, final_g=final_norm_g)
    return h
```

```python
import functools
import math

import numpy as np
import jax
import jax.numpy as jnp
from jax import lax
from jax.experimental import pallas as pl
from jax.experimental.pallas import tpu as pltpu

F32 = jnp.float32
BF16 = jnp.bfloat16

D_MODEL = 1024
CHUNK = 64
EPS = 1e-6
NEG_INF = -1e30
LOG2E = math.log2(math.e)

RET_HEADS = 4
RET_QK_DIM = 128
RET_V_DIM = 256
ATT_HEADS = 8
ATT_HEAD_DIM = 64
ATT_PAST_CHUNKS = 8
MAX_REL = 128
SGU_BLOCK = 128
SGU_GROUPS = 8
SGU_WIDTH = 2048
FFN_HIDDEN = 2816
CONV_WIDTH = 3

RET_QK_W = RET_HEADS * RET_QK_DIM
RET_V_W = RET_HEADS * RET_V_DIM
ATT_W = ATT_HEADS * ATT_HEAD_DIM
AB_IN_W = 2 * RET_QK_W + 2 * RET_V_W + 3 * ATT_W
Z_RET_Q = 0
Z_RET_K = Z_RET_Q + RET_QK_W
Z_RET_V = Z_RET_K + RET_QK_W
Z_RET_G = Z_RET_V + RET_V_W
Z_ATT_Q = Z_RET_G + RET_V_W
Z_ATT_K = Z_ATT_Q + ATT_W
Z_ATT_V = Z_ATT_K + ATT_W

VMEM_LIMIT_BYTES = 56 * 1024 * 1024
SUBLANES = 8
LANES = 128

TM_PROJ = 1024
TM_FFN = 512
TR_RET = 256
TA_ATT = 512
QB_ATT = 256
FFN_CB = 256
IN_CB = 256
SGU_CB = 256
NORM_ROWS = 32
CONV_ROWS = 64


def _const_spec(shape):
    zeros = (0,) * len(shape)
    return pl.BlockSpec(shape, lambda *_: zeros, pipeline_mode=pl.Buffered(1))


def _params(n_axes):
    return pltpu.CompilerParams(
        dimension_semantics=("arbitrary",) * n_axes,
        vmem_limit_bytes=VMEM_LIMIT_BYTES)


def _rms_norm(x, g):
    return x * lax.rsqrt(jnp.mean(x * x, axis=-1, keepdims=True) + EPS) * g


def _gelu_tanh(x):
    c = math.sqrt(2.0 / math.pi)
    return 0.5 * x * (1.0 + jnp.tanh(c * (x + 0.044715 * (x * x * x))))


def _dot(a, b):
    return jnp.dot(a, b, preferred_element_type=F32)


def _dot_nt(a, b):
    return lax.dot_general(a, b, (((1,), (1,)), ((), ())), preferred_element_type=F32)


def _dot_tn(a, b):
    return lax.dot_general(a, b, (((0,), (0,)), ((), ())), preferred_element_type=F32)


def _inproj_blocks(x_ref, g_ref, w_ref, base_ref, off_ref, zput):
    hn = _rms_norm(x_ref[0], g_ref[...]).astype(BF16)
    ca, sa = base_ref[0, 0:1, :], base_ref[0, 1:2, :]
    cos = ca * off_ref[0] - sa * off_ref[1]
    sin = sa * off_ref[2] + ca * off_ref[3]

    def block(j):
        z = _dot(hn, w_ref[:, j * IN_CB:(j + 1) * IN_CB])
        if j * IN_CB < Z_RET_V:
            parts = []
            for h in range(IN_CB // RET_QK_DIM):
                xh = z[:, h * RET_QK_DIM:(h + 1) * RET_QK_DIM]
                parts.append(xh * cos + pltpu.roll(xh, RET_QK_DIM // 2, axis=1) * sin)
            z = jnp.concatenate(parts, axis=1)
        elif Z_ATT_Q <= j * IN_CB < Z_ATT_K:
            z = z * (ATT_HEAD_DIM ** -0.5 * LOG2E)
        zput(j * IN_CB, z.astype(BF16))

    return [functools.partial(block, j) for j in range(AB_IN_W // IN_CB)]


def _rotary_tables(T, tm):
    half = RET_QK_DIM // 2
    inv = 1.0 / (np.float32(10000.0) ** np.linspace(0.0, 1.0, half, dtype=np.float32))
    inv = np.concatenate([inv, inv]).astype(np.float64)
    sign = np.concatenate([-np.ones(half), np.ones(half)])
    base = (np.arange(T // tm) * tm)[:, None] * inv[None, :]
    off = np.arange(tm)[:, None] * inv[None, :]
    bases = np.stack([np.cos(base), np.sin(base)], axis=1)
    offs = np.stack([np.cos(off), np.sin(off), sign * np.cos(off), sign * np.sin(off)])
    return jnp.asarray(bases, F32), jnp.asarray(offs, F32)


def _ret_tables(tr):
    h = np.arange(RET_HEADS, dtype=np.float64)
    log_g = np.log1p(-np.exp2(-5.0 - h))
    n = np.arange(tr)
    cn, cm = n[:, None] // CHUNK, n[None, :] // CHUNK
    diff = (n[:, None] - n[None, :]).astype(np.float64)
    expo = np.where(cn == cm, np.abs(diff), diff)
    dmat = np.where(cm <= cn, np.exp(log_g[:, None, None] * expo), 0.0)
    scale = RET_QK_DIM ** -0.5
    qdec = np.exp(log_g[:, None] * (n + 1.0)[None, :])
    kdec = np.exp(log_g[:, None] * (tr - 1.0 - n)[None, :]) * scale
    qdec = np.broadcast_to(qdec[:, :, None], (RET_HEADS, tr, RET_QK_DIM))
    kdec = np.broadcast_to(kdec[:, :, None], (RET_HEADS, tr, RET_QK_DIM))
    tile_dec = [float(v) for v in np.exp(log_g * tr)]
    return (jnp.asarray(dmat * scale, F32), jnp.asarray(qdec, F32), jnp.asarray(kdec, F32),
            tile_dec)


def _ret_tile(tile_dec, rows, zget, d_ref, qd_ref, kd_ref, o_ref, state_ref):
    heads = range(RET_HEADS)
    q = [zget(rows, Z_RET_Q + h * RET_QK_DIM, RET_QK_DIM) for h in heads]
    k = [zget(rows, Z_RET_K + h * RET_QK_DIM, RET_QK_DIM) for h in heads]
    scores = [_dot_nt(q[h], k[h]) for h in heads]
    for h in heads:
        vv = slice(h * RET_V_DIM, (h + 1) * RET_V_DIM)
        v = zget(rows, Z_RET_V + h * RET_V_DIM, RET_V_DIM)
        state = state_ref[h]
        qd = (q[h].astype(F32) * qd_ref[h]).astype(BF16)
        kd = (k[h].astype(F32) * kd_ref[h]).astype(BF16)
        inter = _dot(qd, state.astype(BF16))
        state_ref[h] = state * tile_dec[h] + _dot_tn(kd, v)
        s = scores[h] * d_ref[h]
        r = _dot(s.astype(BF16), v) + inter
        mu = jnp.mean(r, axis=-1, keepdims=True)
        rc = r - mu
        var = jnp.mean(rc * rc, axis=-1, keepdims=True)
        rn = rc * lax.rsqrt(var + EPS)
        gate = zget(rows, Z_RET_G + h * RET_V_DIM, RET_V_DIM).astype(F32)
        o_ref[0, rows, vv] = (0.5 * gate * (1.0 + jnp.tanh(0.5 * gate)) * rn).astype(BF16)


def _att_bias_vectors(rel_bias):
    hist = ATT_PAST_CHUNKS * CHUNK
    kw = QB_ATT + hist
    nr = 2 * MAX_REL + 1
    period = 2 * hist
    assert period >= kw + QB_ATT - 1 and hist > MAX_REL
    n_far = hist - MAX_REL + 1
    n_near = kw - (hist + MAX_REL)
    n_wrap = period - kw
    first, last = rel_bias[:, 0:1], rel_bias[:, nr - 1:nr]
    return jnp.concatenate([
        jnp.broadcast_to(last, (ATT_HEADS, n_far)),
        rel_bias[:, nr - 2:0:-1],
        jnp.broadcast_to(first, (ATT_HEADS, n_near)),
        jnp.broadcast_to(last, (ATT_HEADS, n_wrap))], axis=1).astype(F32)


def _layer0_kernel(tile_dec, x_ref, ng_ref, w_ref, base_ref, off_ref, g_ref, d_ref, qd_ref,
                   kd_ref, o_ref, ro_ref, zbuf0, zbuf1, kbuf, vbuf, bias_ref, state_ref):
    t = pl.program_id(1)
    ta = TA_ATT
    hist = ATT_PAST_CHUNKS * CHUNK
    kw = QB_ATT + hist
    period = g_ref.shape[1]
    n_qb = ta // QB_ATT
    tile = jnp.maximum(t - 1, 0)

    @pl.when(t == 0)
    def _():
        zbuf1[...] = jnp.zeros(zbuf1.shape, BF16)

    @pl.when(t <= 1)
    def _():
        kbuf[0:hist, :] = jnp.zeros((hist, ATT_W), BF16)
        vbuf[0:hist, :] = jnp.zeros((hist, ATT_W), BF16)
        state_ref[...] = jnp.zeros_like(state_ref)

    @pl.when(t <= 2)
    def _():
        ci = lax.broadcasted_iota(jnp.int32, (QB_ATT, kw), 0) // CHUNK
        col = lax.broadcasted_iota(jnp.int32, (QB_ATT, kw), 1)
        cj = col // CHUNK
        in_band = jnp.logical_and(cj >= ci, cj <= ci + ATT_PAST_CHUNKS)
        for qb in range(n_qb):
            valid = jnp.logical_and(in_band, col >= hist - qb * QB_ATT - tile * ta)
            for h in range(ATT_HEADS):
                rows = jnp.broadcast_to(g_ref[h:h + 1, :], (QB_ATT, period))
                skew = pltpu.roll(rows, 0, axis=1, stride=1, stride_axis=0)
                bias_ref[qb, h] = jnp.where(valid, skew[:, :kw] * LOG2E, NEG_INF)

    def step(wbuf, rbuf):
        def zput(c0, block):
            wbuf[:, c0:c0 + block.shape[1]] = block

        def zget(rows, c0, width):
            return rbuf[rows, c0:c0 + width]

        pending = _inproj_blocks(x_ref, ng_ref, w_ref, base_ref, off_ref, zput)

        def issue_projection_block():
            if pending:
                pending.pop(0)()

        kbuf[hist:hist + ta, :] = zget(slice(0, ta), Z_ATT_K, ATT_W)
        vbuf[hist:hist + ta, :] = zget(slice(0, ta), Z_ATT_V, ATT_W)

        lane = lax.broadcasted_iota(jnp.int32, (1, 2 * ATT_HEAD_DIM), 1)
        low = lane < ATT_HEAD_DIM
        for qb in range(n_qb):
            rows = slice(qb * QB_ATT, (qb + 1) * QB_ATT)
            krows = slice(qb * QB_ATT, qb * QB_ATT + kw)
            for hp in range(ATT_HEADS // 2):
                lanes = slice(hp * 2 * ATT_HEAD_DIM, (hp + 1) * 2 * ATT_HEAD_DIM)
                qp = zget(rows, Z_ATT_Q + lanes.start, 2 * ATT_HEAD_DIM)
                kp = kbuf[krows, lanes]
                vp = vbuf[krows, lanes]
                outs = []
                for e in range(2):
                    keep = low if e == 0 else jnp.logical_not(low)
                    qm = jnp.where(keep, qp, jnp.zeros_like(qp))
                    s = _dot_nt(qm, kp)
                    issue_projection_block()
                    s = s + bias_ref[qb, 2 * hp + e]
                    m = jnp.max(s, axis=-1, keepdims=True)
                    p = jnp.exp2(s - m).astype(BF16)
                    ov = _dot(p, jnp.where(keep, vp, jnp.ones_like(vp)))
                    outs.append(ov / pltpu.roll(ov, ATT_HEAD_DIM, axis=1))
                o_ref[0, rows, lanes] = jnp.where(low, outs[0], outs[1]).astype(BF16)
            for r0 in range(qb * QB_ATT, (qb + 1) * QB_ATT, TR_RET):
                _ret_tile(tile_dec, slice(r0, r0 + TR_RET), zget, d_ref, qd_ref, kd_ref,
                          ro_ref, state_ref)
        while pending:
            issue_projection_block()

        kbuf[0:hist, :] = kbuf[ta:ta + hist, :]
        vbuf[0:hist, :] = vbuf[ta:ta + hist, :]

    @pl.when(t % 2 == 0)
    def _():
        step(zbuf0, zbuf1)

    @pl.when(t % 2 == 1)
    def _():
        step(zbuf1, zbuf0)


def _layer0_mixers(x, norm_g, w_in, rel_bias):
    B, T, D = x.shape
    ta = TA_ATT
    n_t = T // ta
    hist = ATT_PAST_CHUNKS * CHUNK
    assert ta == hist and n_t >= 2 and QB_ATT % TR_RET == 0 and AB_IN_W % IN_CB == 0
    g = _att_bias_vectors(rel_bias)
    dmat, qdec, kdec, tile_dec = _ret_tables(TR_RET)
    bases, offs = _rotary_tables(T, ta)
    proj_tile = lambda b, t: (b, jnp.minimum(t, n_t - 1), 0)
    mix_tile = lambda b, t: (b, jnp.maximum(t - 1, 0), 0)
    return pl.pallas_call(
        functools.partial(_layer0_kernel, tile_dec),
        grid=(B, n_t + 1),
        in_specs=[
            pl.BlockSpec((1, ta, D), proj_tile),
            _const_spec((1, D)),
            _const_spec((D, AB_IN_W)),
            pl.BlockSpec((1, 2, RET_QK_DIM), lambda b, t: (jnp.minimum(t, n_t - 1), 0, 0)),
            _const_spec(offs.shape),
            _const_spec(g.shape),
            _const_spec(dmat.shape), _const_spec(qdec.shape), _const_spec(kdec.shape),
        ],
        out_specs=[pl.BlockSpec((1, ta, ATT_W), mix_tile),
                   pl.BlockSpec((1, ta, RET_V_W), mix_tile)],
        out_shape=[jax.ShapeDtypeStruct((B, T, ATT_W), BF16),
                   jax.ShapeDtypeStruct((B, T, RET_V_W), BF16)],
        scratch_shapes=[pltpu.VMEM((ta, AB_IN_W), BF16),
                        pltpu.VMEM((ta, AB_IN_W), BF16),
                        pltpu.VMEM((hist + ta, ATT_W), BF16),
                        pltpu.VMEM((hist + ta, ATT_W), BF16),
                        pltpu.VMEM((ta // QB_ATT, ATT_HEADS, QB_ATT, QB_ATT + hist), F32),
                        pltpu.VMEM((RET_HEADS, RET_QK_DIM, RET_V_DIM), F32)],
        compiler_params=_params(2),
        name="layer0_mixers",
    )(x, norm_g, w_in, bases, offs, g, dmat, qdec, kdec)


def _ffn_kernel(final_norm, mixer_proj, *refs):
    if mixer_proj:
        x_ref, ya_ref, yb_ref, wmix_ref = refs[:4]
        refs = refs[4:]
    else:
        x_ref = refs[0]
        refs = refs[1:]
    (g_ref, wup_ref, cw_ref, cb_ref, wdn_ref, fg_ref, o_ref,
     hn_ref, zs_ref, zcarry_ref, act_ref, ys_ref) = refs
    tm = x_ref.shape[1]
    pad = SUBLANES
    n_slab = FFN_CB // LANES
    half_rows = tm // 2
    n_lane_slabs = D_MODEL // LANES

    @pl.when(pl.program_id(1) == 0)
    def _():
        zcarry_ref[...] = jnp.zeros(zcarry_ref.shape, F32)

    if mixer_proj:
        na = ya_ref.shape[2]
        mix = _dot(ya_ref[0], wmix_ref[0:na, :]) + _dot(yb_ref[0], wmix_ref[na:, :])
        for l in range(n_lane_slabs):
            ys_ref[l] = mix[:, l * LANES:(l + 1) * LANES]
    for r0 in range(0, tm, NORM_ROWS):
        rows = slice(r0, r0 + NORM_ROWS)
        xs = x_ref[0, rows, :]
        if mixer_proj:
            xs = xs + jnp.concatenate([ys_ref[l, rows, :] for l in range(n_lane_slabs)], axis=1)
            o_ref[0, rows, :] = xs
        hn_ref[rows, :] = _rms_norm(xs, g_ref[...]).astype(BF16)
    lhs = hn_ref[...]
    n_blocks = FFN_HIDDEN // FFN_CB

    def up_block(j):
        for half in range(2):
            c0 = half * FFN_HIDDEN + j * FFN_CB
            z = _dot(lhs, wup_ref[:, c0:c0 + FFN_CB])
            for l in range(n_slab):
                slab = c0 // LANES + l
                zslab = z[:, l * LANES:(l + 1) * LANES]
                zs_ref[j % 2, half * n_slab + l, pad - SUBLANES:pad, :] = zcarry_ref[slab]
                zs_ref[j % 2, half * n_slab + l, pad:pad + tm, :] = zslab
                zcarry_ref[slab] = zslab[tm - SUBLANES:, :]

    up_block(0)
    for j in range(n_blocks):
        buf = j % 2
        if j + 1 < n_blocks:
            up_block(j + 1)
        for par in range(2):
            for l in range(n_slab):
                for u0 in range(0, half_rows, CONV_ROWS):
                    conv = []
                    for half in range(2):
                        c0 = half * FFN_HIDDEN + j * FFN_CB + l * LANES
                        c = cb_ref[:, c0:c0 + LANES]
                        for tap in range(CONV_WIDTH):
                            off = pad - (CONV_WIDTH - 1) + tap + par + 2 * u0
                            c = c + (zs_ref[buf, half * n_slab + l,
                                            pl.ds(off, CONV_ROWS, stride=2), :]
                                     * cw_ref[tap:tap + 1, c0:c0 + LANES])
                        conv.append(c)
                    a0 = j * FFN_CB + l * LANES
                    act_ref[par * half_rows + u0:par * half_rows + u0 + CONV_ROWS,
                            a0:a0 + LANES] = (_gelu_tanh(conv[0]) * conv[1]).astype(BF16)
    yp = _dot(act_ref[...], wdn_ref[...])
    for l in range(n_lane_slabs):
        for par in range(2):
            ys_ref[l, pl.ds(par, half_rows, stride=2), :] = (
                yp[par * half_rows:(par + 1) * half_rows, l * LANES:(l + 1) * LANES])
    for r0 in range(0, tm, NORM_ROWS):
        rows = slice(r0, r0 + NORM_ROWS)
        y = (o_ref if mixer_proj else x_ref)[0, rows, :] + jnp.concatenate(
            [ys_ref[l, rows, :] for l in range(n_lane_slabs)], axis=1)
        if final_norm:
            y = _rms_norm(y, fg_ref[...])
        o_ref[0, rows, :] = y


def _layer_spec(layer, shape):
    zeros = (0,) * len(shape)
    return pl.BlockSpec((None,) + tuple(shape), lambda *_: (layer,) + zeros,
                        pipeline_mode=pl.Buffered(1))


def _ffn(x, layer, g, w_up, conv_w, conv_b, w_down, final_g=None, mixer=None):
    B, T, D = x.shape
    tm = TM_FFN
    final_norm = final_g is not None
    if final_g is None:
        final_g = jnp.ones((D,), F32)
    tile = lambda width: pl.BlockSpec((1, tm, width), lambda b, t: (b, t, 0))
    mixer_args, mixer_specs = (), []
    if mixer is not None:
        ya, yb, w_mix = mixer
        mixer_args = (ya, yb, w_mix)
        mixer_specs = [tile(ya.shape[2]), tile(yb.shape[2]), _const_spec(w_mix.shape)]
    return pl.pallas_call(
        functools.partial(_ffn_kernel, final_norm, mixer is not None),
        grid=(B, T // tm),
        in_specs=[tile(D)] + mixer_specs + [
            _layer_spec(layer, (1, D)),
            _layer_spec(layer, (D, 2 * FFN_HIDDEN)),
            _layer_spec(layer, (CONV_WIDTH, 2 * FFN_HIDDEN)),
            _layer_spec(layer, (1, 2 * FFN_HIDDEN)),
            _layer_spec(layer, (FFN_HIDDEN, D)),
            _const_spec((1, D)),
        ],
        out_specs=pl.BlockSpec((1, tm, D), lambda b, t: (b, t, 0)),
        out_shape=jax.ShapeDtypeStruct((B, T, D), F32),
        scratch_shapes=[
            pltpu.VMEM((tm, D), BF16),
            pltpu.VMEM((2, 2 * FFN_CB // LANES, tm + SUBLANES, LANES), F32),
            pltpu.VMEM((2 * FFN_HIDDEN // LANES, SUBLANES, LANES), F32),
            pltpu.VMEM((tm, FFN_HIDDEN), BF16),
            pltpu.VMEM((D // LANES, tm, LANES), F32),
        ],
        compiler_params=_params(2),
        name="conv_ffn_final" if final_norm else "conv_ffn",
    )(x, *mixer_args, g, w_up, conv_w, conv_b, w_down, final_g[None, :])


def _sgu_kernel(x_ref, g_ref, win_ref, lng_ref, lnb_ref, ws_ref, bs_ref, wout_ref, o_ref,
                hn_ref, v_ref, u_ref, y_ref):
    tm = x_ref.shape[1]
    gw = SGU_WIDTH // SGU_GROUPS
    pair = 2 * SGU_BLOCK
    for r0 in range(0, tm, NORM_ROWS):
        rows = slice(r0, r0 + NORM_ROWS)
        hn_ref[rows, :] = _rms_norm(x_ref[0, rows, :], g_ref[...]).astype(BF16)
    hn = hn_ref[...]

    vsum = jnp.zeros((tm, 1), F32)
    for j in range(SGU_WIDTH // SGU_CB):
        cols = slice(j * SGU_CB, (j + 1) * SGU_CB)
        zv = _gelu_tanh(_dot(hn, win_ref[:, SGU_WIDTH + j * SGU_CB:SGU_WIDTH + (j + 1) * SGU_CB]))
        v_ref[:, cols] = zv
        vsum = vsum + jnp.sum(zv, axis=-1, keepdims=True)
    for j in range(SGU_WIDTH // SGU_CB):
        cols = slice(j * SGU_CB, (j + 1) * SGU_CB)
        u_ref[:, cols] = _gelu_tanh(_dot(hn, win_ref[:, cols]))
    mu = vsum * (1.0 / SGU_WIDTH)
    vsq = jnp.zeros((tm, 1), F32)
    for j in range(SGU_WIDTH // SGU_CB):
        cols = slice(j * SGU_CB, (j + 1) * SGU_CB)
        vc = v_ref[:, cols] - mu
        vsq = vsq + jnp.sum(vc * vc, axis=-1, keepdims=True)
    rstd = lax.rsqrt(vsq * (1.0 / SGU_WIDTH) + EPS)

    ri = lax.broadcasted_iota(jnp.int32, (pair, pair), 0)
    ci = lax.broadcasted_iota(jnp.int32, (pair, pair), 1)
    allowed = (ci // CHUNK) <= (ri // CHUNK)
    same = (ci // SGU_BLOCK) == (ri // SGU_BLOCK)
    keep = jnp.logical_and(allowed, same)
    for gidx in range(SGU_GROUPS):
        cols = slice(gidx * gw, (gidx + 1) * gw)
        w = jnp.where(keep, ws_ref[gidx], jnp.zeros((pair, pair), BF16))
        vn = ((v_ref[:, cols] - mu) * rstd * lng_ref[:, cols] + lnb_ref[:, cols]).astype(BF16)
        for rp in range(tm // pair):
            rows = slice(rp * pair, (rp + 1) * pair)
            mixed = _dot(w, vn[rows]) + bs_ref[:, cols]
            y_ref[rows, cols] = (u_ref[rows, cols] * mixed).astype(BF16)
    o_ref[0] = x_ref[0] + _dot(y_ref[...], wout_ref[...])


def _sgu(x, g, w_in, ln_g, ln_b, w_s, b_s, w_out):
    B, T, D = x.shape
    tm = TM_PROJ
    gw = SGU_WIDTH // SGU_GROUPS
    pair = 2 * SGU_BLOCK
    zero = jnp.zeros_like(w_s)
    w_bd = jnp.concatenate([jnp.concatenate([w_s, zero], axis=2),
                            jnp.concatenate([zero, w_s], axis=2)], axis=1).astype(BF16)
    b_tab = jnp.repeat(b_s.T, gw, axis=1)
    b_tab = jnp.concatenate([b_tab, b_tab], axis=0)
    return pl.pallas_call(
        _sgu_kernel,
        grid=(B, T // tm),
        in_specs=[
            pl.BlockSpec((1, tm, D), lambda b, t: (b, t, 0)),
            _const_spec((1, D)),
            _const_spec((D, 2 * SGU_WIDTH)),
            _const_spec((1, SGU_WIDTH)),
            _const_spec((1, SGU_WIDTH)),
            _const_spec((SGU_GROUPS, pair, pair)),
            _const_spec((pair, SGU_WIDTH)),
            _const_spec((SGU_WIDTH, D)),
        ],
        out_specs=pl.BlockSpec((1, tm, D), lambda b, t: (b, t, 0)),
        out_shape=jax.ShapeDtypeStruct((B, T, D), F32),
        scratch_shapes=[
            pltpu.VMEM((tm, D), BF16),
            pltpu.VMEM((tm, SGU_WIDTH), F32),
            pltpu.VMEM((tm, SGU_WIDTH), F32),
            pltpu.VMEM((tm, SGU_WIDTH), BF16),
        ],
        compiler_params=_params(2),
        name="sgu_mixer",
    )(x, g[None, :], w_in.astype(BF16), ln_g[None, :], ln_b[None, :], w_bd, b_tab,
      w_out.astype(BF16))


def kernel(x, attn_norm_g, ffn_norm_g, ab_w_in, ab_w_out, ab_rel_bias, c_w_in, c_ln_g, c_ln_b,
           c_w_s, c_b_s, c_w_out, ffn_w_up, ffn_conv_w, ffn_conv_b, ffn_w_down, final_norm_g):
    ffn_params = (ffn_norm_g[:, None, :], ffn_w_up.astype(BF16), ffn_conv_w,
                  ffn_conv_b[:, None, :], ffn_w_down.astype(BF16))
    yb, ya = _layer0_mixers(x, attn_norm_g[0][None, :], ab_w_in[0].astype(BF16), ab_rel_bias[0])
    h = _ffn(x, 0, *ffn_params, mixer=(ya, yb, ab_w_out[0].astype(BF16)))
    h = _sgu(h, attn_norm_g[1], c_w_in[0], c_ln_g[0], c_ln_b[0], c_w_s[0], c_b_s[0], c_w_out[0])
    h = _ffn(h, 1, *ffn_params, final_g=final_norm_g)
    return h
```

```python
import functools
import math

import numpy as np
import jax
import jax.numpy as jnp
from jax import lax
from jax.experimental import pallas as pl
from jax.experimental.pallas import tpu as pltpu

F32 = jnp.float32
BF16 = jnp.bfloat16

D_MODEL = 1024
CHUNK = 64
EPS = 1e-6
NEG_INF = -1e30
LOG2E = math.log2(math.e)

RET_HEADS = 4
RET_QK_DIM = 128
RET_V_DIM = 256
ATT_HEADS = 8
ATT_HEAD_DIM = 64
ATT_PAST_CHUNKS = 8
MAX_REL = 128
SGU_BLOCK = 128
SGU_GROUPS = 8
SGU_WIDTH = 2048
FFN_HIDDEN = 2816
CONV_WIDTH = 3

RET_QK_W = RET_HEADS * RET_QK_DIM
RET_V_W = RET_HEADS * RET_V_DIM
ATT_W = ATT_HEADS * ATT_HEAD_DIM
AB_IN_W = 2 * RET_QK_W + 2 * RET_V_W + 3 * ATT_W
Z_RET_Q = 0
Z_RET_K = Z_RET_Q + RET_QK_W
Z_RET_V = Z_RET_K + RET_QK_W
Z_RET_G = Z_RET_V + RET_V_W
Z_ATT_Q = Z_RET_G + RET_V_W
Z_ATT_K = Z_ATT_Q + ATT_W
Z_ATT_V = Z_ATT_K + ATT_W

VMEM_LIMIT_BYTES = 56 * 1024 * 1024
SUBLANES = 8
LANES = 128

TM_PROJ = 1024
TM_FFN = 512
TR_RET = 256
TA_ATT = 512
QB_ATT = 256
FFN_CB = 256
IN_CB = 256
SGU_CB = 256
U_LEAD = 2
NORM_ROWS = 32
CONV_ROWS = 64


def _const_spec(shape):
    zeros = (0,) * len(shape)
    return pl.BlockSpec(shape, lambda *_: zeros, pipeline_mode=pl.Buffered(1))


def _params(n_axes):
    return pltpu.CompilerParams(
        dimension_semantics=("arbitrary",) * n_axes,
        vmem_limit_bytes=VMEM_LIMIT_BYTES)


def _rms_norm(x, g):
    return x * lax.rsqrt(jnp.mean(x * x, axis=-1, keepdims=True) + EPS) * g


def _gelu_tanh(x):
    c = math.sqrt(2.0 / math.pi)
    return 0.5 * x * (1.0 + jnp.tanh(c * (x + 0.044715 * (x * x * x))))


def _dot(a, b):
    return jnp.dot(a, b, preferred_element_type=F32)


def _dot_nt(a, b):
    return lax.dot_general(a, b, (((1,), (1,)), ((), ())), preferred_element_type=F32)


def _dot_tn(a, b):
    return lax.dot_general(a, b, (((0,), (0,)), ((), ())), preferred_element_type=F32)


def _inproj_blocks(x_ref, g_ref, w_ref, base_ref, off_ref, zput):
    hn = _rms_norm(x_ref[0], g_ref[...]).astype(BF16)
    ca, sa = base_ref[0, 0:1, :], base_ref[0, 1:2, :]
    cos = ca * off_ref[0] - sa * off_ref[1]
    sin = sa * off_ref[2] + ca * off_ref[3]

    def block(j):
        z = _dot(hn, w_ref[:, j * IN_CB:(j + 1) * IN_CB])
        if j * IN_CB < Z_RET_V:
            parts = []
            for h in range(IN_CB // RET_QK_DIM):
                xh = z[:, h * RET_QK_DIM:(h + 1) * RET_QK_DIM]
                parts.append(xh * cos + pltpu.roll(xh, RET_QK_DIM // 2, axis=1) * sin)
            z = jnp.concatenate(parts, axis=1)
        elif Z_ATT_Q <= j * IN_CB < Z_ATT_K:
            z = z * (ATT_HEAD_DIM ** -0.5 * LOG2E)
        zput(j * IN_CB, z.astype(BF16))

    return [functools.partial(block, j) for j in range(AB_IN_W // IN_CB)]


def _rotary_tables(T, tm):
    half = RET_QK_DIM // 2
    inv = 1.0 / (np.float32(10000.0) ** np.linspace(0.0, 1.0, half, dtype=np.float32))
    inv = np.concatenate([inv, inv]).astype(np.float64)
    sign = np.concatenate([-np.ones(half), np.ones(half)])
    base = (np.arange(T // tm) * tm)[:, None] * inv[None, :]
    off = np.arange(tm)[:, None] * inv[None, :]
    bases = np.stack([np.cos(base), np.sin(base)], axis=1)
    offs = np.stack([np.cos(off), np.sin(off), sign * np.cos(off), sign * np.sin(off)])
    return jnp.asarray(bases, F32), jnp.asarray(offs, F32)


def _ret_tables(tr):
    h = np.arange(RET_HEADS, dtype=np.float64)
    log_g = np.log1p(-np.exp2(-5.0 - h))
    n = np.arange(tr)
    cn, cm = n[:, None] // CHUNK, n[None, :] // CHUNK
    diff = (n[:, None] - n[None, :]).astype(np.float64)
    expo = np.where(cn == cm, np.abs(diff), diff)
    dmat = np.where(cm <= cn, np.exp(log_g[:, None, None] * expo), 0.0)
    scale = RET_QK_DIM ** -0.5
    qdec = np.exp(log_g[:, None] * (n + 1.0)[None, :])
    kdec = np.exp(log_g[:, None] * (tr - 1.0 - n)[None, :]) * scale
    qdec = np.broadcast_to(qdec[:, :, None], (RET_HEADS, tr, RET_QK_DIM))
    kdec = np.broadcast_to(kdec[:, :, None], (RET_HEADS, tr, RET_QK_DIM))
    tile_dec = [float(v) for v in np.exp(log_g * tr)]
    return (jnp.asarray(dmat * scale, F32), jnp.asarray(qdec, F32), jnp.asarray(kdec, F32),
            tile_dec)


def _ret_tile(tile_dec, rows, zget, d_ref, qd_ref, kd_ref, o_ref, state_ref):
    heads = range(RET_HEADS)
    q = [zget(rows, Z_RET_Q + h * RET_QK_DIM, RET_QK_DIM) for h in heads]
    k = [zget(rows, Z_RET_K + h * RET_QK_DIM, RET_QK_DIM) for h in heads]
    scores = [_dot_nt(q[h], k[h]) for h in heads]
    for h in heads:
        vv = slice(h * RET_V_DIM, (h + 1) * RET_V_DIM)
        v = zget(rows, Z_RET_V + h * RET_V_DIM, RET_V_DIM)
        state = state_ref[h]
        qd = (q[h].astype(F32) * qd_ref[h]).astype(BF16)
        kd = (k[h].astype(F32) * kd_ref[h]).astype(BF16)
        inter = _dot(qd, state.astype(BF16))
        state_ref[h] = state * tile_dec[h] + _dot_tn(kd, v)
        s = scores[h] * d_ref[h]
        r = _dot(s.astype(BF16), v) + inter
        mu = jnp.mean(r, axis=-1, keepdims=True)
        rc = r - mu
        var = jnp.mean(rc * rc, axis=-1, keepdims=True)
        rn = rc * lax.rsqrt(var + EPS)
        gate = zget(rows, Z_RET_G + h * RET_V_DIM, RET_V_DIM).astype(F32)
        o_ref[0, rows, vv] = (0.5 * gate * (1.0 + jnp.tanh(0.5 * gate)) * rn).astype(BF16)


def _att_bias_vectors(rel_bias):
    hist = ATT_PAST_CHUNKS * CHUNK
    kw = QB_ATT + hist
    nr = 2 * MAX_REL + 1
    period = 2 * hist
    assert period >= kw + QB_ATT - 1 and hist > MAX_REL
    n_far = hist - MAX_REL + 1
    n_near = kw - (hist + MAX_REL)
    n_wrap = period - kw
    first, last = rel_bias[:, 0:1], rel_bias[:, nr - 1:nr]
    return jnp.concatenate([
        jnp.broadcast_to(last, (ATT_HEADS, n_far)),
        rel_bias[:, nr - 2:0:-1],
        jnp.broadcast_to(first, (ATT_HEADS, n_near)),
        jnp.broadcast_to(last, (ATT_HEADS, n_wrap))], axis=1).astype(F32)


def _layer0_kernel(tile_dec, x_ref, ng_ref, w_ref, base_ref, off_ref, g_ref, d_ref, qd_ref,
                   kd_ref, o_ref, ro_ref, zbuf0, zbuf1, kbuf, vbuf, bias_ref, state_ref):
    t = pl.program_id(1)
    ta = TA_ATT
    hist = ATT_PAST_CHUNKS * CHUNK
    kw = QB_ATT + hist
    period = g_ref.shape[1]
    n_qb = ta // QB_ATT
    tile = jnp.maximum(t - 1, 0)

    @pl.when(t == 0)
    def _():
        zbuf1[...] = jnp.zeros(zbuf1.shape, BF16)

    @pl.when(t <= 1)
    def _():
        kbuf[0:hist, :] = jnp.zeros((hist, ATT_W), BF16)
        vbuf[0:hist, :] = jnp.zeros((hist, ATT_W), BF16)
        state_ref[...] = jnp.zeros_like(state_ref)

    @pl.when(t <= 2)
    def _():
        ci = lax.broadcasted_iota(jnp.int32, (QB_ATT, kw), 0) // CHUNK
        col = lax.broadcasted_iota(jnp.int32, (QB_ATT, kw), 1)
        cj = col // CHUNK
        in_band = jnp.logical_and(cj >= ci, cj <= ci + ATT_PAST_CHUNKS)
        for qb in range(n_qb):
            valid = jnp.logical_and(in_band, col >= hist - qb * QB_ATT - tile * ta)
            for h in range(ATT_HEADS):
                rows = jnp.broadcast_to(g_ref[h:h + 1, :], (QB_ATT, period))
                skew = pltpu.roll(rows, 0, axis=1, stride=1, stride_axis=0)
                bias_ref[qb, h] = jnp.where(valid, skew[:, :kw] * LOG2E, NEG_INF)

    def step(wbuf, rbuf):
        def zput(c0, block):
            wbuf[:, c0:c0 + block.shape[1]] = block

        def zget(rows, c0, width):
            return rbuf[rows, c0:c0 + width]

        pending = _inproj_blocks(x_ref, ng_ref, w_ref, base_ref, off_ref, zput)

        def issue_projection_block():
            if pending:
                pending.pop(0)()

        kbuf[hist:hist + ta, :] = zget(slice(0, ta), Z_ATT_K, ATT_W)
        vbuf[hist:hist + ta, :] = zget(slice(0, ta), Z_ATT_V, ATT_W)

        lane = lax.broadcasted_iota(jnp.int32, (1, 2 * ATT_HEAD_DIM), 1)
        low = lane < ATT_HEAD_DIM
        for qb in range(n_qb):
            rows = slice(qb * QB_ATT, (qb + 1) * QB_ATT)
            krows = slice(qb * QB_ATT, qb * QB_ATT + kw)
            for hp in range(ATT_HEADS // 2):
                lanes = slice(hp * 2 * ATT_HEAD_DIM, (hp + 1) * 2 * ATT_HEAD_DIM)
                qp = zget(rows, Z_ATT_Q + lanes.start, 2 * ATT_HEAD_DIM)
                kp = kbuf[krows, lanes]
                vp = vbuf[krows, lanes]
                outs = []
                for e in range(2):
                    keep = low if e == 0 else jnp.logical_not(low)
                    qm = jnp.where(keep, qp, jnp.zeros_like(qp))
                    s = _dot_nt(qm, kp)
                    issue_projection_block()
                    s = s + bias_ref[qb, 2 * hp + e]
                    m = jnp.max(s, axis=-1, keepdims=True)
                    p = jnp.exp2(s - m).astype(BF16)
                    ov = _dot(p, jnp.where(keep, vp, jnp.ones_like(vp)))
                    outs.append(ov / pltpu.roll(ov, ATT_HEAD_DIM, axis=1))
                o_ref[0, rows, lanes] = jnp.where(low, outs[0], outs[1]).astype(BF16)
            for r0 in range(qb * QB_ATT, (qb + 1) * QB_ATT, TR_RET):
                _ret_tile(tile_dec, slice(r0, r0 + TR_RET), zget, d_ref, qd_ref, kd_ref,
                          ro_ref, state_ref)
        while pending:
            issue_projection_block()

        kbuf[0:hist, :] = kbuf[ta:ta + hist, :]
        vbuf[0:hist, :] = vbuf[ta:ta + hist, :]

    @pl.when(t % 2 == 0)
    def _():
        step(zbuf0, zbuf1)

    @pl.when(t % 2 == 1)
    def _():
        step(zbuf1, zbuf0)


def _layer0_mixers(x, norm_g, w_in, rel_bias):
    B, T, D = x.shape
    ta = TA_ATT
    n_t = T // ta
    hist = ATT_PAST_CHUNKS * CHUNK
    assert ta == hist and n_t >= 2 and QB_ATT % TR_RET == 0 and AB_IN_W % IN_CB == 0
    g = _att_bias_vectors(rel_bias)
    dmat, qdec, kdec, tile_dec = _ret_tables(TR_RET)
    bases, offs = _rotary_tables(T, ta)
    proj_tile = lambda b, t: (b, jnp.minimum(t, n_t - 1), 0)
    mix_tile = lambda b, t: (b, jnp.maximum(t - 1, 0), 0)
    return pl.pallas_call(
        functools.partial(_layer0_kernel, tile_dec),
        grid=(B, n_t + 1),
        in_specs=[
            pl.BlockSpec((1, ta, D), proj_tile),
            _const_spec((1, D)),
            _const_spec((D, AB_IN_W)),
            pl.BlockSpec((1, 2, RET_QK_DIM), lambda b, t: (jnp.minimum(t, n_t - 1), 0, 0)),
            _const_spec(offs.shape),
            _const_spec(g.shape),
            _const_spec(dmat.shape), _const_spec(qdec.shape), _const_spec(kdec.shape),
        ],
        out_specs=[pl.BlockSpec((1, ta, ATT_W), mix_tile),
                   pl.BlockSpec((1, ta, RET_V_W), mix_tile)],
        out_shape=[jax.ShapeDtypeStruct((B, T, ATT_W), BF16),
                   jax.ShapeDtypeStruct((B, T, RET_V_W), BF16)],
        scratch_shapes=[pltpu.VMEM((ta, AB_IN_W), BF16),
                        pltpu.VMEM((ta, AB_IN_W), BF16),
                        pltpu.VMEM((hist + ta, ATT_W), BF16),
                        pltpu.VMEM((hist + ta, ATT_W), BF16),
                        pltpu.VMEM((ta // QB_ATT, ATT_HEADS, QB_ATT, QB_ATT + hist), F32),
                        pltpu.VMEM((RET_HEADS, RET_QK_DIM, RET_V_DIM), F32)],
        compiler_params=_params(2),
        name="layer0_mixers",
    )(x, norm_g, w_in, bases, offs, g, dmat, qdec, kdec)


def _ffn_kernel(final_norm, mixer_proj, *refs):
    if mixer_proj:
        x_ref, ya_ref, yb_ref, wmix_ref = refs[:4]
        refs = refs[4:]
    else:
        x_ref = refs[0]
        refs = refs[1:]
    (g_ref, wup_ref, cw_ref, cb_ref, wdn_ref, fg_ref, o_ref,
     hn_ref, zs_ref, zcarry_ref, act_ref, ys_ref) = refs
    tm = x_ref.shape[1]
    pad = SUBLANES
    n_slab = FFN_CB // LANES
    half_rows = tm // 2
    n_lane_slabs = D_MODEL // LANES

    @pl.when(pl.program_id(1) == 0)
    def _():
        zcarry_ref[...] = jnp.zeros(zcarry_ref.shape, F32)

    if mixer_proj:
        na = ya_ref.shape[2]
        for n in range(D_MODEL // FFN_CB):
            cols = slice(n * FFN_CB, (n + 1) * FFN_CB)
            mix = _dot(ya_ref[0], wmix_ref[0:na, cols]) + _dot(yb_ref[0], wmix_ref[na:, cols])
            for l in range(n_slab):
                ys_ref[n * n_slab + l] = mix[:, l * LANES:(l + 1) * LANES]
    for r0 in range(0, tm, NORM_ROWS):
        rows = slice(r0, r0 + NORM_ROWS)
        xs = x_ref[0, rows, :]
        if mixer_proj:
            xs = xs + jnp.concatenate([ys_ref[l, rows, :] for l in range(n_lane_slabs)], axis=1)
            o_ref[0, rows, :] = xs
        hn_ref[rows, :] = _rms_norm(xs, g_ref[...]).astype(BF16)
    lhs = hn_ref[...]
    n_blocks = FFN_HIDDEN // FFN_CB

    def up_block(j):
        for half in range(2):
            c0 = half * FFN_HIDDEN + j * FFN_CB
            z = _dot(lhs, wup_ref[:, c0:c0 + FFN_CB])
            for l in range(n_slab):
                slab = c0 // LANES + l
                zslab = z[:, l * LANES:(l + 1) * LANES]
                zs_ref[j % 2, half * n_slab + l, pad - SUBLANES:pad, :] = zcarry_ref[slab]
                zs_ref[j % 2, half * n_slab + l, pad:pad + tm, :] = zslab
                zcarry_ref[slab] = zslab[tm - SUBLANES:, :]

    up_block(0)
    for j in range(n_blocks):
        buf = j % 2
        if j + 1 < n_blocks:
            up_block(j + 1)
        for par in range(2):
            for l in range(n_slab):
                for u0 in range(0, half_rows, CONV_ROWS):
                    conv = []
                    for half in range(2):
                        c0 = half * FFN_HIDDEN + j * FFN_CB + l * LANES
                        c = cb_ref[:, c0:c0 + LANES]
                        for tap in range(CONV_WIDTH):
                            off = pad - (CONV_WIDTH - 1) + tap + par + 2 * u0
                            c = c + (zs_ref[buf, half * n_slab + l,
                                            pl.ds(off, CONV_ROWS, stride=2), :]
                                     * cw_ref[tap:tap + 1, c0:c0 + LANES])
                        conv.append(c)
                    a0 = j * FFN_CB + l * LANES
                    act_ref[par * half_rows + u0:par * half_rows + u0 + CONV_ROWS,
                            a0:a0 + LANES] = (_gelu_tanh(conv[0]) * conv[1]).astype(BF16)
    for n in range(D_MODEL // FFN_CB):
        yp = _dot(act_ref[...], wdn_ref[:, n * FFN_CB:(n + 1) * FFN_CB])
        for l in range(n_slab):
            for par in range(2):
                ys_ref[n * n_slab + l, pl.ds(par, half_rows, stride=2), :] = (
                    yp[par * half_rows:(par + 1) * half_rows, l * LANES:(l + 1) * LANES])
    for r0 in range(0, tm, NORM_ROWS):
        rows = slice(r0, r0 + NORM_ROWS)
        y = (o_ref if mixer_proj else x_ref)[0, rows, :] + jnp.concatenate(
            [ys_ref[l, rows, :] for l in range(n_lane_slabs)], axis=1)
        if final_norm:
            y = _rms_norm(y, fg_ref[...])
        o_ref[0, rows, :] = y


def _layer_spec(layer, shape):
    zeros = (0,) * len(shape)
    return pl.BlockSpec((None,) + tuple(shape), lambda *_: (layer,) + zeros,
                        pipeline_mode=pl.Buffered(1))


def _ffn(x, layer, g, w_up, conv_w, conv_b, w_down, final_g=None, mixer=None):
    B, T, D = x.shape
    tm = TM_FFN
    final_norm = final_g is not None
    if final_g is None:
        final_g = jnp.ones((D,), F32)
    tile = lambda width: pl.BlockSpec((1, tm, width), lambda b, t: (b, t, 0))
    mixer_args, mixer_specs = (), []
    if mixer is not None:
        ya, yb, w_mix = mixer
        mixer_args = (ya, yb, w_mix)
        mixer_specs = [tile(ya.shape[2]), tile(yb.shape[2]), _const_spec(w_mix.shape)]
    return pl.pallas_call(
        functools.partial(_ffn_kernel, final_norm, mixer is not None),
        grid=(B, T // tm),
        in_specs=[tile(D)] + mixer_specs + [
            _layer_spec(layer, (1, D)),
            _layer_spec(layer, (D, 2 * FFN_HIDDEN)),
            _layer_spec(layer, (CONV_WIDTH, 2 * FFN_HIDDEN)),
            _layer_spec(layer, (1, 2 * FFN_HIDDEN)),
            _layer_spec(layer, (FFN_HIDDEN, D)),
            _const_spec((1, D)),
        ],
        out_specs=pl.BlockSpec((1, tm, D), lambda b, t: (b, t, 0)),
        out_shape=jax.ShapeDtypeStruct((B, T, D), F32),
        scratch_shapes=[
            pltpu.VMEM((tm, D), BF16),
            pltpu.VMEM((2, 2 * FFN_CB // LANES, tm + SUBLANES, LANES), F32),
            pltpu.VMEM((2 * FFN_HIDDEN // LANES, SUBLANES, LANES), F32),
            pltpu.VMEM((tm, FFN_HIDDEN), BF16),
            pltpu.VMEM((D // LANES, tm, LANES), F32),
        ],
        compiler_params=_params(2),
        name="conv_ffn_final" if final_norm else "conv_ffn",
    )(x, *mixer_args, g, w_up, conv_w, conv_b, w_down, final_g[None, :])


def _sgu_kernel(x_ref, g_ref, win_ref, lng_ref, lnb_ref, ws_ref, bs_ref, wout_ref, o_ref,
                hn_ref, v_ref, u_ref, y_ref):
    tm = x_ref.shape[1]
    gw = SGU_WIDTH // SGU_GROUPS
    pair = 2 * SGU_BLOCK
    for r0 in range(0, tm, NORM_ROWS):
        rows = slice(r0, r0 + NORM_ROWS)
        hn_ref[rows, :] = _rms_norm(x_ref[0, rows, :], g_ref[...]).astype(BF16)
    hn = hn_ref[...]

    vsum = jnp.zeros((tm, 1), F32)
    for j in range(SGU_WIDTH // SGU_CB):
        cols = slice(j * SGU_CB, (j + 1) * SGU_CB)
        zv = _gelu_tanh(_dot(hn, win_ref[:, SGU_WIDTH + j * SGU_CB:SGU_WIDTH + (j + 1) * SGU_CB]))
        v_ref[:, cols] = zv
        vsum = vsum + jnp.sum(zv, axis=-1, keepdims=True)
    def u_block(j):
        cols = slice(j * gw, (j + 1) * gw)
        u_ref[:, cols] = _gelu_tanh(_dot(hn, win_ref[:, cols]))

    for j in range(U_LEAD):
        u_block(j)
    mu = vsum * (1.0 / SGU_WIDTH)
    vsq = jnp.zeros((tm, 1), F32)
    for j in range(SGU_WIDTH // SGU_CB):
        cols = slice(j * SGU_CB, (j + 1) * SGU_CB)
        vc = v_ref[:, cols] - mu
        vsq = vsq + jnp.sum(vc * vc, axis=-1, keepdims=True)
    rstd = lax.rsqrt(vsq * (1.0 / SGU_WIDTH) + EPS)

    ri = lax.broadcasted_iota(jnp.int32, (pair, pair), 0)
    ci = lax.broadcasted_iota(jnp.int32, (pair, pair), 1)
    allowed = (ci // CHUNK) <= (ri // CHUNK)
    same = (ci // SGU_BLOCK) == (ri // SGU_BLOCK)
    keep = jnp.logical_and(allowed, same)
    for gidx in range(SGU_GROUPS):
        cols = slice(gidx * gw, (gidx + 1) * gw)
        w = jnp.where(keep, ws_ref[gidx], jnp.zeros((pair, pair), BF16))
        vn = ((v_ref[:, cols] - mu) * rstd * lng_ref[:, cols] + lnb_ref[:, cols]).astype(BF16)
        for rp in range(tm // pair):
            rows = slice(rp * pair, (rp + 1) * pair)
            mixed = _dot(w, vn[rows]) + bs_ref[:, cols]
            y_ref[rows, cols] = (u_ref[rows, cols] * mixed).astype(BF16)
        if gidx + U_LEAD < SGU_GROUPS:
            u_block(gidx + U_LEAD)
    o_ref[0] = x_ref[0] + _dot(y_ref[...], wout_ref[...])


def _sgu(x, g, w_in, ln_g, ln_b, w_s, b_s, w_out):
    B, T, D = x.shape
    tm = TM_PROJ
    gw = SGU_WIDTH // SGU_GROUPS
    pair = 2 * SGU_BLOCK
    zero = jnp.zeros_like(w_s)
    w_bd = jnp.concatenate([jnp.concatenate([w_s, zero], axis=2),
                            jnp.concatenate([zero, w_s], axis=2)], axis=1).astype(BF16)
    b_tab = jnp.repeat(b_s.T, gw, axis=1)
    b_tab = jnp.concatenate([b_tab, b_tab], axis=0)
    return pl.pallas_call(
        _sgu_kernel,
        grid=(B, T // tm),
        in_specs=[
            pl.BlockSpec((1, tm, D), lambda b, t: (b, t, 0)),
            _const_spec((1, D)),
            _const_spec((D, 2 * SGU_WIDTH)),
            _const_spec((1, SGU_WIDTH)),
            _const_spec((1, SGU_WIDTH)),
            _const_spec((SGU_GROUPS, pair, pair)),
            _const_spec((pair, SGU_WIDTH)),
            _const_spec((SGU_WIDTH, D)),
        ],
        out_specs=pl.BlockSpec((1, tm, D), lambda b, t: (b, t, 0)),
        out_shape=jax.ShapeDtypeStruct((B, T, D), F32),
        scratch_shapes=[
            pltpu.VMEM((tm, D), BF16),
            pltpu.VMEM((tm, SGU_WIDTH), F32),
            pltpu.VMEM((tm, SGU_WIDTH), F32),
            pltpu.VMEM((tm, SGU_WIDTH), BF16),
        ],
        compiler_params=_params(2),
        name="sgu_mixer",
    )(x, g[None, :], w_in.astype(BF16), ln_g[None, :], ln_b[None, :], w_bd, b_tab,
      w_out.astype(BF16))


def kernel(x, attn_norm_g, ffn_norm_g, ab_w_in, ab_w_out, ab_rel_bias, c_w_in, c_ln_g, c_ln_b,
           c_w_s, c_b_s, c_w_out, ffn_w_up, ffn_conv_w, ffn_conv_b, ffn_w_down, final_norm_g):
    ffn_params = (ffn_norm_g[:, None, :], ffn_w_up.astype(BF16), ffn_conv_w,
                  ffn_conv_b[:, None, :], ffn_w_down.astype(BF16))
    yb, ya = _layer0_mixers(x, attn_norm_g[0][None, :], ab_w_in[0].astype(BF16), ab_rel_bias[0])
    h = _ffn(x, 0, *ffn_params, mixer=(ya, yb, ab_w_out[0].astype(BF16)))
    h = _sgu(h, attn_norm_g[1], c_w_in[0], c_ln_g[0], c_ln_b[0], c_w_s[0], c_b_s[0], c_w_out[0])
    h = _ffn(h, 1, *ffn_params, final_g=final_norm_g)
    return h
```

```python
import functools
import math

import numpy as np
import jax
import jax.numpy as jnp
from jax import lax
from jax.experimental import pallas as pl
from jax.experimental.pallas import tpu as pltpu

F32 = jnp.float32
BF16 = jnp.bfloat16

D_MODEL = 1024
CHUNK = 64
EPS = 1e-6
NEG_INF = -1e30
LOG2E = math.log2(math.e)

RET_HEADS = 4
RET_QK_DIM = 128
RET_V_DIM = 256
ATT_HEADS = 8
ATT_HEAD_DIM = 64
ATT_PAST_CHUNKS = 8
MAX_REL = 128
SGU_BLOCK = 128
SGU_GROUPS = 8
SGU_WIDTH = 2048
FFN_HIDDEN = 2816
CONV_WIDTH = 3

RET_QK_W = RET_HEADS * RET_QK_DIM
RET_V_W = RET_HEADS * RET_V_DIM
ATT_W = ATT_HEADS * ATT_HEAD_DIM
AB_IN_W = 2 * RET_QK_W + 2 * RET_V_W + 3 * ATT_W
Z_RET_Q = 0
Z_RET_K = Z_RET_Q + RET_QK_W
Z_RET_V = Z_RET_K + RET_QK_W
Z_RET_G = Z_RET_V + RET_V_W
Z_ATT_Q = Z_RET_G + RET_V_W
Z_ATT_K = Z_ATT_Q + ATT_W
Z_ATT_V = Z_ATT_K + ATT_W

VMEM_LIMIT_BYTES = 56 * 1024 * 1024
SUBLANES = 8
LANES = 128

TM_PROJ = 1024
TM_FFN = 512
TR_RET = 256
TA_ATT = 512
QB_ATT = 256
FFN_CB = 256
IN_CB = 256
SGU_CB = 256
U_LEAD = 2
NORM_ROWS = 32
CONV_ROWS = 64


def _const_spec(shape):
    zeros = (0,) * len(shape)
    return pl.BlockSpec(shape, lambda *_: zeros, pipeline_mode=pl.Buffered(1))


def _params(n_axes):
    return pltpu.CompilerParams(
        dimension_semantics=("arbitrary",) * n_axes,
        vmem_limit_bytes=VMEM_LIMIT_BYTES)


def _rms_norm(x, g):
    return x * lax.rsqrt(jnp.mean(x * x, axis=-1, keepdims=True) + EPS) * g


def _gelu_tanh(x):
    c = math.sqrt(2.0 / math.pi)
    return 0.5 * x * (1.0 + jnp.tanh(c * (x + 0.044715 * (x * x * x))))


def _dot(a, b):
    return jnp.dot(a, b, preferred_element_type=F32)


def _dot_nt(a, b):
    return lax.dot_general(a, b, (((1,), (1,)), ((), ())), preferred_element_type=F32)


def _dot_tn(a, b):
    return lax.dot_general(a, b, (((0,), (0,)), ((), ())), preferred_element_type=F32)


def _inproj_blocks(x_ref, g_ref, w_ref, base_ref, off_ref, zput):
    hn = _rms_norm(x_ref[0], g_ref[...]).astype(BF16)
    ca, sa = base_ref[0, 0:1, :], base_ref[0, 1:2, :]
    cos = ca * off_ref[0] - sa * off_ref[1]
    sin = sa * off_ref[2] + ca * off_ref[3]

    def block(j):
        z = _dot(hn, w_ref[:, j * IN_CB:(j + 1) * IN_CB])
        if j * IN_CB < Z_RET_V:
            parts = []
            for h in range(IN_CB // RET_QK_DIM):
                xh = z[:, h * RET_QK_DIM:(h + 1) * RET_QK_DIM]
                parts.append(xh * cos + pltpu.roll(xh, RET_QK_DIM // 2, axis=1) * sin)
            z = jnp.concatenate(parts, axis=1)
        elif Z_ATT_Q <= j * IN_CB < Z_ATT_K:
            z = z * (ATT_HEAD_DIM ** -0.5 * LOG2E)
        zput(j * IN_CB, z.astype(BF16))

    return [functools.partial(block, j) for j in range(AB_IN_W // IN_CB)]


def _rotary_tables(T, tm):
    half = RET_QK_DIM // 2
    inv = 1.0 / (np.float32(10000.0) ** np.linspace(0.0, 1.0, half, dtype=np.float32))
    inv = np.concatenate([inv, inv]).astype(np.float64)
    sign = np.concatenate([-np.ones(half), np.ones(half)])
    base = (np.arange(T // tm) * tm)[:, None] * inv[None, :]
    off = np.arange(tm)[:, None] * inv[None, :]
    bases = np.stack([np.cos(base), np.sin(base)], axis=1)
    offs = np.stack([np.cos(off), np.sin(off), sign * np.cos(off), sign * np.sin(off)])
    return jnp.asarray(bases, F32), jnp.asarray(offs, F32)


def _ret_tables(tr):
    h = np.arange(RET_HEADS, dtype=np.float64)
    log_g = np.log1p(-np.exp2(-5.0 - h))
    n = np.arange(tr)
    cn, cm = n[:, None] // CHUNK, n[None, :] // CHUNK
    diff = (n[:, None] - n[None, :]).astype(np.float64)
    expo = np.where(cn == cm, np.abs(diff), diff)
    dmat = np.where(cm <= cn, np.exp(log_g[:, None, None] * expo), 0.0)
    scale = RET_QK_DIM ** -0.5
    qdec = np.exp(log_g[:, None] * (n + 1.0)[None, :])
    kdec = np.exp(log_g[:, None] * (tr - 1.0 - n)[None, :]) * scale
    qdec = np.broadcast_to(qdec[:, :, None], (RET_HEADS, tr, RET_QK_DIM))
    kdec = np.broadcast_to(kdec[:, :, None], (RET_HEADS, tr, RET_QK_DIM))
    tile_dec = [float(v) for v in np.exp(log_g * tr)]
    return (jnp.asarray(dmat * scale, F32), jnp.asarray(qdec, F32), jnp.asarray(kdec, F32),
            tile_dec)


def _ret_tile(tile_dec, rows, zget, d_ref, qd_ref, kd_ref, o_ref, state_ref):
    heads = range(RET_HEADS)
    q = [zget(rows, Z_RET_Q + h * RET_QK_DIM, RET_QK_DIM) for h in heads]
    k = [zget(rows, Z_RET_K + h * RET_QK_DIM, RET_QK_DIM) for h in heads]
    scores = [_dot_nt(q[h], k[h]) for h in heads]
    for h in heads:
        vv = slice(h * RET_V_DIM, (h + 1) * RET_V_DIM)
        v = zget(rows, Z_RET_V + h * RET_V_DIM, RET_V_DIM)
        state = state_ref[h]
        qd = (q[h].astype(F32) * qd_ref[h]).astype(BF16)
        kd = (k[h].astype(F32) * kd_ref[h]).astype(BF16)
        inter = _dot(qd, state.astype(BF16))
        state_ref[h] = state * tile_dec[h] + _dot_tn(kd, v)
        s = scores[h] * d_ref[h]
        r = _dot(s.astype(BF16), v) + inter
        mu = jnp.mean(r, axis=-1, keepdims=True)
        rc = r - mu
        var = jnp.mean(rc * rc, axis=-1, keepdims=True)
        rn = rc * lax.rsqrt(var + EPS)
        gate = zget(rows, Z_RET_G + h * RET_V_DIM, RET_V_DIM).astype(F32)
        o_ref[0, rows, vv] = (0.5 * gate * (1.0 + jnp.tanh(0.5 * gate)) * rn).astype(BF16)


def _att_bias_vectors(rel_bias):
    hist = ATT_PAST_CHUNKS * CHUNK
    kw = QB_ATT + hist
    nr = 2 * MAX_REL + 1
    period = 2 * hist
    assert period >= kw + QB_ATT - 1 and hist > MAX_REL
    n_far = hist - MAX_REL + 1
    n_near = kw - (hist + MAX_REL)
    n_wrap = period - kw
    first, last = rel_bias[:, 0:1], rel_bias[:, nr - 1:nr]
    return jnp.concatenate([
        jnp.broadcast_to(last, (ATT_HEADS, n_far)),
        rel_bias[:, nr - 2:0:-1],
        jnp.broadcast_to(first, (ATT_HEADS, n_near)),
        jnp.broadcast_to(last, (ATT_HEADS, n_wrap))], axis=1).astype(F32)


def _layer0_kernel(tile_dec, x_ref, ng_ref, w_ref, base_ref, off_ref, g_ref, d_ref, qd_ref,
                   kd_ref, o_ref, ro_ref, zbuf0, zbuf1, kbuf, vbuf, bias_ref, state_ref):
    t = pl.program_id(1)
    ta = TA_ATT
    hist = ATT_PAST_CHUNKS * CHUNK
    kw = QB_ATT + hist
    period = g_ref.shape[1]
    n_qb = ta // QB_ATT
    tile = jnp.maximum(t - 1, 0)

    @pl.when(t == 0)
    def _():
        zbuf1[...] = jnp.zeros(zbuf1.shape, BF16)

    @pl.when(t <= 1)
    def _():
        kbuf[0:hist, :] = jnp.zeros((hist, ATT_W), BF16)
        vbuf[0:hist, :] = jnp.zeros((hist, ATT_W), BF16)
        state_ref[...] = jnp.zeros_like(state_ref)

    @pl.when(t <= 2)
    def _():
        ci = lax.broadcasted_iota(jnp.int32, (QB_ATT, kw), 0) // CHUNK
        col = lax.broadcasted_iota(jnp.int32, (QB_ATT, kw), 1)
        cj = col // CHUNK
        in_band = jnp.logical_and(cj >= ci, cj <= ci + ATT_PAST_CHUNKS)
        for qb in range(n_qb):
            valid = jnp.logical_and(in_band, col >= hist - qb * QB_ATT - tile * ta)
            for h in range(ATT_HEADS):
                rows = jnp.broadcast_to(g_ref[h:h + 1, :], (QB_ATT, period))
                skew = pltpu.roll(rows, 0, axis=1, stride=1, stride_axis=0)
                bias_ref[qb, h] = jnp.where(valid, skew[:, :kw] * LOG2E, NEG_INF)

    def step(wbuf, rbuf):
        def zput(c0, block):
            wbuf[:, c0:c0 + block.shape[1]] = block

        def zget(rows, c0, width):
            return rbuf[rows, c0:c0 + width]

        pending = _inproj_blocks(x_ref, ng_ref, w_ref, base_ref, off_ref, zput)

        def issue_projection_block():
            if pending:
                pending.pop(0)()

        kbuf[hist:hist + ta, :] = zget(slice(0, ta), Z_ATT_K, ATT_W)
        vbuf[hist:hist + ta, :] = zget(slice(0, ta), Z_ATT_V, ATT_W)

        lane = lax.broadcasted_iota(jnp.int32, (1, 2 * ATT_HEAD_DIM), 1)
        low = lane < ATT_HEAD_DIM
        for qb in range(n_qb):
            rows = slice(qb * QB_ATT, (qb + 1) * QB_ATT)
            krows = slice(qb * QB_ATT, qb * QB_ATT + kw)
            for hp in range(ATT_HEADS // 2):
                lanes = slice(hp * 2 * ATT_HEAD_DIM, (hp + 1) * 2 * ATT_HEAD_DIM)
                qp = zget(rows, Z_ATT_Q + lanes.start, 2 * ATT_HEAD_DIM)
                kp = kbuf[krows, lanes]
                vp = vbuf[krows, lanes]
                keeps = (low, jnp.logical_not(low))
                scores = []
                for e in range(2):
                    scores.append(_dot_nt(jnp.where(keeps[e], qp, jnp.zeros_like(qp)), kp))
                    if e == 0:
                        issue_projection_block()
                outs = []
                for e in range(2):
                    keep = keeps[e]
                    if e == 1:
                        issue_projection_block()
                    s = scores[e] + bias_ref[qb, 2 * hp + e]
                    m = jnp.max(s, axis=-1, keepdims=True)
                    p = jnp.exp2(s - m).astype(BF16)
                    ov = _dot(p, jnp.where(keep, vp, jnp.ones_like(vp)))
                    outs.append(ov / pltpu.roll(ov, ATT_HEAD_DIM, axis=1))
                o_ref[0, rows, lanes] = jnp.where(low, outs[0], outs[1]).astype(BF16)
            for r0 in range(qb * QB_ATT, (qb + 1) * QB_ATT, TR_RET):
                _ret_tile(tile_dec, slice(r0, r0 + TR_RET), zget, d_ref, qd_ref, kd_ref,
                          ro_ref, state_ref)
        while pending:
            issue_projection_block()

        kbuf[0:hist, :] = kbuf[ta:ta + hist, :]
        vbuf[0:hist, :] = vbuf[ta:ta + hist, :]

    @pl.when(t % 2 == 0)
    def _():
        step(zbuf0, zbuf1)

    @pl.when(t % 2 == 1)
    def _():
        step(zbuf1, zbuf0)


def _layer0_mixers(x, norm_g, w_in, rel_bias):
    B, T, D = x.shape
    ta = TA_ATT
    n_t = T // ta
    hist = ATT_PAST_CHUNKS * CHUNK
    assert ta == hist and n_t >= 2 and QB_ATT % TR_RET == 0 and AB_IN_W % IN_CB == 0
    g = _att_bias_vectors(rel_bias)
    dmat, qdec, kdec, tile_dec = _ret_tables(TR_RET)
    bases, offs = _rotary_tables(T, ta)
    proj_tile = lambda b, t: (b, jnp.minimum(t, n_t - 1), 0)
    mix_tile = lambda b, t: (b, jnp.maximum(t - 1, 0), 0)
    return pl.pallas_call(
        functools.partial(_layer0_kernel, tile_dec),
        grid=(B, n_t + 1),
        in_specs=[
            pl.BlockSpec((1, ta, D), proj_tile),
            _const_spec((1, D)),
            _const_spec((D, AB_IN_W)),
            pl.BlockSpec((1, 2, RET_QK_DIM), lambda b, t: (jnp.minimum(t, n_t - 1), 0, 0)),
            _const_spec(offs.shape),
            _const_spec(g.shape),
            _const_spec(dmat.shape), _const_spec(qdec.shape), _const_spec(kdec.shape),
        ],
        out_specs=[pl.BlockSpec((1, ta, ATT_W), mix_tile),
                   pl.BlockSpec((1, ta, RET_V_W), mix_tile)],
        out_shape=[jax.ShapeDtypeStruct((B, T, ATT_W), BF16),
                   jax.ShapeDtypeStruct((B, T, RET_V_W), BF16)],
        scratch_shapes=[pltpu.VMEM((ta, AB_IN_W), BF16),
                        pltpu.VMEM((ta, AB_IN_W), BF16),
                        pltpu.VMEM((hist + ta, ATT_W), BF16),
                        pltpu.VMEM((hist + ta, ATT_W), BF16),
                        pltpu.VMEM((ta // QB_ATT, ATT_HEADS, QB_ATT, QB_ATT + hist), F32),
                        pltpu.VMEM((RET_HEADS, RET_QK_DIM, RET_V_DIM), F32)],
        compiler_params=_params(2),
        name="layer0_mixers",
    )(x, norm_g, w_in, bases, offs, g, dmat, qdec, kdec)


def _ffn_kernel(final_norm, mixer_proj, *refs):
    if mixer_proj:
        x_ref, ya_ref, yb_ref, wmix_ref = refs[:4]
        refs = refs[4:]
    else:
        x_ref = refs[0]
        refs = refs[1:]
    (g_ref, wup_ref, cw_ref, cb_ref, wdn_ref, fg_ref, o_ref,
     hn_ref, zs_ref, zcarry_ref, act_ref, ys_ref) = refs
    tm = x_ref.shape[1]
    pad = SUBLANES
    n_slab = FFN_CB // LANES
    half_rows = tm // 2
    n_lane_slabs = D_MODEL // LANES

    @pl.when(pl.program_id(1) == 0)
    def _():
        zcarry_ref[...] = jnp.zeros(zcarry_ref.shape, F32)

    if mixer_proj:
        na = ya_ref.shape[2]
        mix = _dot(ya_ref[0], wmix_ref[0:na, :]) + _dot(yb_ref[0], wmix_ref[na:, :])
        for l in range(n_lane_slabs):
            ys_ref[l] = mix[:, l * LANES:(l + 1) * LANES]
    for r0 in range(0, tm, NORM_ROWS):
        rows = slice(r0, r0 + NORM_ROWS)
        xs = x_ref[0, rows, :]
        if mixer_proj:
            xs = xs + jnp.concatenate([ys_ref[l, rows, :] for l in range(n_lane_slabs)], axis=1)
            o_ref[0, rows, :] = xs
        hn_ref[rows, :] = _rms_norm(xs, g_ref[...]).astype(BF16)
    lhs = hn_ref[...]
    n_blocks = FFN_HIDDEN // FFN_CB

    def up_block(j):
        for half in range(2):
            c0 = half * FFN_HIDDEN + j * FFN_CB
            z = _dot(lhs, wup_ref[:, c0:c0 + FFN_CB])
            for l in range(n_slab):
                slab = c0 // LANES + l
                zslab = z[:, l * LANES:(l + 1) * LANES]
                zs_ref[j % 2, half * n_slab + l, pad - SUBLANES:pad, :] = zcarry_ref[slab]
                zs_ref[j % 2, half * n_slab + l, pad:pad + tm, :] = zslab
                zcarry_ref[slab] = zslab[tm - SUBLANES:, :]

    up_block(0)
    for j in range(n_blocks):
        buf = j % 2
        if j + 1 < n_blocks:
            up_block(j + 1)
        for par in range(2):
            for l in range(n_slab):
                for u0 in range(0, half_rows, CONV_ROWS):
                    conv = []
                    for half in range(2):
                        c0 = half * FFN_HIDDEN + j * FFN_CB + l * LANES
                        c = cb_ref[:, c0:c0 + LANES]
                        for tap in range(CONV_WIDTH):
                            off = pad - (CONV_WIDTH - 1) + tap + par + 2 * u0
                            c = c + (zs_ref[buf, half * n_slab + l,
                                            pl.ds(off, CONV_ROWS, stride=2), :]
                                     * cw_ref[tap:tap + 1, c0:c0 + LANES])
                        conv.append(c)
                    a0 = j * FFN_CB + l * LANES
                    act_ref[par * half_rows + u0:par * half_rows + u0 + CONV_ROWS,
                            a0:a0 + LANES] = (_gelu_tanh(conv[0]) * conv[1]).astype(BF16)
    yp = _dot(act_ref[...], wdn_ref[...])
    for l in range(n_lane_slabs):
        for par in range(2):
            ys_ref[l, pl.ds(par, half_rows, stride=2), :] = (
                yp[par * half_rows:(par + 1) * half_rows, l * LANES:(l + 1) * LANES])
    for r0 in range(0, tm, NORM_ROWS):
        rows = slice(r0, r0 + NORM_ROWS)
        y = (o_ref if mixer_proj else x_ref)[0, rows, :] + jnp.concatenate(
            [ys_ref[l, rows, :] for l in range(n_lane_slabs)], axis=1)
        if final_norm:
            y = _rms_norm(y, fg_ref[...])
        o_ref[0, rows, :] = y


def _layer_spec(layer, shape):
    zeros = (0,) * len(shape)
    return pl.BlockSpec((None,) + tuple(shape), lambda *_: (layer,) + zeros,
                        pipeline_mode=pl.Buffered(1))


def _ffn(x, layer, g, w_up, conv_w, conv_b, w_down, final_g=None, mixer=None):
    B, T, D = x.shape
    tm = TM_FFN
    final_norm = final_g is not None
    if final_g is None:
        final_g = jnp.ones((D,), F32)
    tile = lambda width: pl.BlockSpec((1, tm, width), lambda b, t: (b, t, 0))
    mixer_args, mixer_specs = (), []
    if mixer is not None:
        ya, yb, w_mix = mixer
        mixer_args = (ya, yb, w_mix)
        mixer_specs = [tile(ya.shape[2]), tile(yb.shape[2]), _const_spec(w_mix.shape)]
    return pl.pallas_call(
        functools.partial(_ffn_kernel, final_norm, mixer is not None),
        grid=(B, T // tm),
        in_specs=[tile(D)] + mixer_specs + [
            _layer_spec(layer, (1, D)),
            _layer_spec(layer, (D, 2 * FFN_HIDDEN)),
            _layer_spec(layer, (CONV_WIDTH, 2 * FFN_HIDDEN)),
            _layer_spec(layer, (1, 2 * FFN_HIDDEN)),
            _layer_spec(layer, (FFN_HIDDEN, D)),
            _const_spec((1, D)),
        ],
        out_specs=pl.BlockSpec((1, tm, D), lambda b, t: (b, t, 0)),
        out_shape=jax.ShapeDtypeStruct((B, T, D), F32),
        scratch_shapes=[
            pltpu.VMEM((tm, D), BF16),
            pltpu.VMEM((2, 2 * FFN_CB // LANES, tm + SUBLANES, LANES), F32),
            pltpu.VMEM((2 * FFN_HIDDEN // LANES, SUBLANES, LANES), F32),
            pltpu.VMEM((tm, FFN_HIDDEN), BF16),
            pltpu.VMEM((D // LANES, tm, LANES), F32),
        ],
        compiler_params=_params(2),
        name="conv_ffn_final" if final_norm else "conv_ffn",
    )(x, *mixer_args, g, w_up, conv_w, conv_b, w_down, final_g[None, :])


def _sgu_kernel(x_ref, g_ref, win_ref, lng_ref, lnb_ref, ws_ref, bs_ref, wout_ref, o_ref,
                hn_ref, v_ref, u_ref, y_ref):
    tm = x_ref.shape[1]
    gw = SGU_WIDTH // SGU_GROUPS
    pair = 2 * SGU_BLOCK
    for r0 in range(0, tm, NORM_ROWS):
        rows = slice(r0, r0 + NORM_ROWS)
        hn_ref[rows, :] = _rms_norm(x_ref[0, rows, :], g_ref[...]).astype(BF16)
    hn = hn_ref[...]

    vsum = jnp.zeros((tm, 1), F32)
    for j in range(SGU_WIDTH // SGU_CB):
        cols = slice(j * SGU_CB, (j + 1) * SGU_CB)
        zv = _gelu_tanh(_dot(hn, win_ref[:, SGU_WIDTH + j * SGU_CB:SGU_WIDTH + (j + 1) * SGU_CB]))
        v_ref[:, cols] = zv
        vsum = vsum + jnp.sum(zv, axis=-1, keepdims=True)
    def u_block(j):
        cols = slice(j * gw, (j + 1) * gw)
        u_ref[:, cols] = _gelu_tanh(_dot(hn, win_ref[:, cols]))

    for j in range(U_LEAD):
        u_block(j)
    mu = vsum * (1.0 / SGU_WIDTH)
    vsq = jnp.zeros((tm, 1), F32)
    for j in range(SGU_WIDTH // SGU_CB):
        cols = slice(j * SGU_CB, (j + 1) * SGU_CB)
        vc = v_ref[:, cols] - mu
        vsq = vsq + jnp.sum(vc * vc, axis=-1, keepdims=True)
    rstd = lax.rsqrt(vsq * (1.0 / SGU_WIDTH) + EPS)

    ri = lax.broadcasted_iota(jnp.int32, (pair, pair), 0)
    ci = lax.broadcasted_iota(jnp.int32, (pair, pair), 1)
    allowed = (ci // CHUNK) <= (ri // CHUNK)
    same = (ci // SGU_BLOCK) == (ri // SGU_BLOCK)
    keep = jnp.logical_and(allowed, same)
    for gidx in range(SGU_GROUPS):
        cols = slice(gidx * gw, (gidx + 1) * gw)
        w = jnp.where(keep, ws_ref[gidx], jnp.zeros((pair, pair), BF16))
        vn = ((v_ref[:, cols] - mu) * rstd * lng_ref[:, cols] + lnb_ref[:, cols]).astype(BF16)
        for rp in range(tm // pair):
            rows = slice(rp * pair, (rp + 1) * pair)
            mixed = _dot(w, vn[rows]) + bs_ref[:, cols]
            y_ref[rows, cols] = (u_ref[rows, cols] * mixed).astype(BF16)
        if gidx + U_LEAD < SGU_GROUPS:
            u_block(gidx + U_LEAD)
    o_ref[0] = x_ref[0] + _dot(y_ref[...], wout_ref[...])


def _sgu(x, g, w_in, ln_g, ln_b, w_s, b_s, w_out):
    B, T, D = x.shape
    tm = TM_PROJ
    gw = SGU_WIDTH // SGU_GROUPS
    pair = 2 * SGU_BLOCK
    zero = jnp.zeros_like(w_s)
    w_bd = jnp.concatenate([jnp.concatenate([w_s, zero], axis=2),
                            jnp.concatenate([zero, w_s], axis=2)], axis=1).astype(BF16)
    b_tab = jnp.repeat(b_s.T, gw, axis=1)
    b_tab = jnp.concatenate([b_tab, b_tab], axis=0)
    return pl.pallas_call(
        _sgu_kernel,
        grid=(B, T // tm),
        in_specs=[
            pl.BlockSpec((1, tm, D), lambda b, t: (b, t, 0)),
            _const_spec((1, D)),
            _const_spec((D, 2 * SGU_WIDTH)),
            _const_spec((1, SGU_WIDTH)),
            _const_spec((1, SGU_WIDTH)),
            _const_spec((SGU_GROUPS, pair, pair)),
            _const_spec((pair, SGU_WIDTH)),
            _const_spec((SGU_WIDTH, D)),
        ],
        out_specs=pl.BlockSpec((1, tm, D), lambda b, t: (b, t, 0)),
        out_shape=jax.ShapeDtypeStruct((B, T, D), F32),
        scratch_shapes=[
            pltpu.VMEM((tm, D), BF16),
            pltpu.VMEM((tm, SGU_WIDTH), F32),
            pltpu.VMEM((tm, SGU_WIDTH), F32),
            pltpu.VMEM((tm, SGU_WIDTH), BF16),
        ],
        compiler_params=_params(2),
        name="sgu_mixer",
    )(x, g[None, :], w_in.astype(BF16), ln_g[None, :], ln_b[None, :], w_bd, b_tab,
      w_out.astype(BF16))


def kernel(x, attn_norm_g, ffn_norm_g, ab_w_in, ab_w_out, ab_rel_bias, c_w_in, c_ln_g, c_ln_b,
           c_w_s, c_b_s, c_w_out, ffn_w_up, ffn_conv_w, ffn_conv_b, ffn_w_down, final_norm_g):
    ffn_params = (ffn_norm_g[:, None, :], ffn_w_up.astype(BF16), ffn_conv_w,
                  ffn_conv_b[:, None, :], ffn_w_down.astype(BF16))
    yb, ya = _layer0_mixers(x, attn_norm_g[0][None, :], ab_w_in[0].astype(BF16), ab_rel_bias[0])
    h = _ffn(x, 0, *ffn_params, mixer=(ya, yb, ab_w_out[0].astype(BF16)))
    h = _sgu(h, attn_norm_g[1], c_w_in[0], c_ln_g[0], c_ln_b[0], c_w_s[0], c_b_s[0], c_w_out[0])
    h = _ffn(h, 1, *ffn_params, final_g=final_norm_g)
    return h
```

```python
import functools
import math

import numpy as np
import jax
import jax.numpy as jnp
from jax import lax
from jax.experimental import pallas as pl
from jax.experimental.pallas import tpu as pltpu

F32 = jnp.float32
BF16 = jnp.bfloat16

D_MODEL = 1024
CHUNK = 64
EPS = 1e-6
NEG_INF = -1e30
LOG2E = math.log2(math.e)

RET_HEADS = 4
RET_QK_DIM = 128
RET_V_DIM = 256
ATT_HEADS = 8
ATT_HEAD_DIM = 64
ATT_PAST_CHUNKS = 8
MAX_REL = 128
SGU_BLOCK = 128
SGU_GROUPS = 8
SGU_WIDTH = 2048
FFN_HIDDEN = 2816
CONV_WIDTH = 3

RET_QK_W = RET_HEADS * RET_QK_DIM
RET_V_W = RET_HEADS * RET_V_DIM
ATT_W = ATT_HEADS * ATT_HEAD_DIM
AB_IN_W = 2 * RET_QK_W + 2 * RET_V_W + 3 * ATT_W
Z_RET_Q = 0
Z_RET_K = Z_RET_Q + RET_QK_W
Z_RET_V = Z_RET_K + RET_QK_W
Z_RET_G = Z_RET_V + RET_V_W
Z_ATT_Q = Z_RET_G + RET_V_W
Z_ATT_K = Z_ATT_Q + ATT_W
Z_ATT_V = Z_ATT_K + ATT_W

VMEM_LIMIT_BYTES = 56 * 1024 * 1024
SUBLANES = 8
LANES = 128

TM_PROJ = 1024
TM_FFN = 512
TR_RET = 256
TA_ATT = 512
QB_ATT = 256
FFN_CB = 256
IN_CB = 256
SGU_CB = 256
U_LEAD = 1
NORM_ROWS = 32
CONV_ROWS = 64


def _const_spec(shape):
    zeros = (0,) * len(shape)
    return pl.BlockSpec(shape, lambda *_: zeros, pipeline_mode=pl.Buffered(1))


def _params(n_axes):
    return pltpu.CompilerParams(
        dimension_semantics=("arbitrary",) * n_axes,
        vmem_limit_bytes=VMEM_LIMIT_BYTES)


def _rms_norm(x, g):
    return x * lax.rsqrt(jnp.mean(x * x, axis=-1, keepdims=True) + EPS) * g


def _gelu_tanh(x):
    c = math.sqrt(2.0 / math.pi)
    return 0.5 * x * (1.0 + jnp.tanh(c * (x + 0.044715 * (x * x * x))))


def _dot(a, b):
    return jnp.dot(a, b, preferred_element_type=F32)


def _dot_nt(a, b):
    return lax.dot_general(a, b, (((1,), (1,)), ((), ())), preferred_element_type=F32)


def _dot_tn(a, b):
    return lax.dot_general(a, b, (((0,), (0,)), ((), ())), preferred_element_type=F32)


def _inproj_blocks(x_ref, g_ref, w_ref, base_ref, off_ref, zput):
    hn = _rms_norm(x_ref[0], g_ref[...]).astype(BF16)
    ca, sa = base_ref[0, 0:1, :], base_ref[0, 1:2, :]
    cos = ca * off_ref[0] - sa * off_ref[1]
    sin = sa * off_ref[2] + ca * off_ref[3]

    def block(j):
        z = _dot(hn, w_ref[:, j * IN_CB:(j + 1) * IN_CB])
        if j * IN_CB < Z_RET_V:
            parts = []
            for h in range(IN_CB // RET_QK_DIM):
                xh = z[:, h * RET_QK_DIM:(h + 1) * RET_QK_DIM]
                parts.append(xh * cos + pltpu.roll(xh, RET_QK_DIM // 2, axis=1) * sin)
            z = jnp.concatenate(parts, axis=1)
        elif Z_ATT_Q <= j * IN_CB < Z_ATT_K:
            z = z * (ATT_HEAD_DIM ** -0.5 * LOG2E)
        zput(j * IN_CB, z.astype(BF16))

    return [functools.partial(block, j) for j in range(AB_IN_W // IN_CB)]


def _rotary_tables(T, tm):
    half = RET_QK_DIM // 2
    inv = 1.0 / (np.float32(10000.0) ** np.linspace(0.0, 1.0, half, dtype=np.float32))
    inv = np.concatenate([inv, inv]).astype(np.float64)
    sign = np.concatenate([-np.ones(half), np.ones(half)])
    base = (np.arange(T // tm) * tm)[:, None] * inv[None, :]
    off = np.arange(tm)[:, None] * inv[None, :]
    bases = np.stack([np.cos(base), np.sin(base)], axis=1)
    offs = np.stack([np.cos(off), np.sin(off), sign * np.cos(off), sign * np.sin(off)])
    return jnp.asarray(bases, F32), jnp.asarray(offs, F32)


def _ret_tables(tr):
    h = np.arange(RET_HEADS, dtype=np.float64)
    log_g = np.log1p(-np.exp2(-5.0 - h))
    n = np.arange(tr)
    cn, cm = n[:, None] // CHUNK, n[None, :] // CHUNK
    diff = (n[:, None] - n[None, :]).astype(np.float64)
    expo = np.where(cn == cm, np.abs(diff), diff)
    dmat = np.where(cm <= cn, np.exp(log_g[:, None, None] * expo), 0.0)
    scale = RET_QK_DIM ** -0.5
    qdec = np.exp(log_g[:, None] * (n + 1.0)[None, :])
    kdec = np.exp(log_g[:, None] * (tr - 1.0 - n)[None, :]) * scale
    qdec = np.broadcast_to(qdec[:, :, None], (RET_HEADS, tr, RET_QK_DIM))
    kdec = np.broadcast_to(kdec[:, :, None], (RET_HEADS, tr, RET_QK_DIM))
    tile_dec = [float(v) for v in np.exp(log_g * tr)]
    return (jnp.asarray(dmat * scale, F32), jnp.asarray(qdec, F32), jnp.asarray(kdec, F32),
            tile_dec)


def _ret_tile(tile_dec, rows, zget, d_ref, qd_ref, kd_ref, o_ref, state_ref):
    heads = range(RET_HEADS)
    q = [zget(rows, Z_RET_Q + h * RET_QK_DIM, RET_QK_DIM) for h in heads]
    k = [zget(rows, Z_RET_K + h * RET_QK_DIM, RET_QK_DIM) for h in heads]
    scores = [_dot_nt(q[h], k[h]) for h in heads]
    for h in heads:
        vv = slice(h * RET_V_DIM, (h + 1) * RET_V_DIM)
        v = zget(rows, Z_RET_V + h * RET_V_DIM, RET_V_DIM)
        state = state_ref[h]
        qd = (q[h].astype(F32) * qd_ref[h]).astype(BF16)
        kd = (k[h].astype(F32) * kd_ref[h]).astype(BF16)
        inter = _dot(qd, state.astype(BF16))
        state_ref[h] = state * tile_dec[h] + _dot_tn(kd, v)
        s = scores[h] * d_ref[h]
        r = _dot(s.astype(BF16), v) + inter
        mu = jnp.mean(r, axis=-1, keepdims=True)
        rc = r - mu
        var = jnp.mean(rc * rc, axis=-1, keepdims=True)
        rn = rc * lax.rsqrt(var + EPS)
        gate = zget(rows, Z_RET_G + h * RET_V_DIM, RET_V_DIM).astype(F32)
        o_ref[0, rows, vv] = (0.5 * gate * (1.0 + jnp.tanh(0.5 * gate)) * rn).astype(BF16)


def _att_bias_vectors(rel_bias):
    hist = ATT_PAST_CHUNKS * CHUNK
    kw = QB_ATT + hist
    nr = 2 * MAX_REL + 1
    period = 2 * hist
    assert period >= kw + QB_ATT - 1 and hist > MAX_REL
    n_far = hist - MAX_REL + 1
    n_near = kw - (hist + MAX_REL)
    n_wrap = period - kw
    first, last = rel_bias[:, 0:1], rel_bias[:, nr - 1:nr]
    return jnp.concatenate([
        jnp.broadcast_to(last, (ATT_HEADS, n_far)),
        rel_bias[:, nr - 2:0:-1],
        jnp.broadcast_to(first, (ATT_HEADS, n_near)),
        jnp.broadcast_to(last, (ATT_HEADS, n_wrap))], axis=1).astype(F32)


def _layer0_kernel(tile_dec, x_ref, ng_ref, w_ref, base_ref, off_ref, g_ref, d_ref, qd_ref,
                   kd_ref, o_ref, ro_ref, zbuf0, zbuf1, kbuf, vbuf, bias_ref, state_ref):
    t = pl.program_id(1)
    ta = TA_ATT
    hist = ATT_PAST_CHUNKS * CHUNK
    kw = QB_ATT + hist
    period = g_ref.shape[1]
    n_qb = ta // QB_ATT
    tile = jnp.maximum(t - 1, 0)

    @pl.when(t == 0)
    def _():
        zbuf1[...] = jnp.zeros(zbuf1.shape, BF16)

    @pl.when(t <= 1)
    def _():
        kbuf[0:hist, :] = jnp.zeros((hist, ATT_W), BF16)
        vbuf[0:hist, :] = jnp.zeros((hist, ATT_W), BF16)
        state_ref[...] = jnp.zeros_like(state_ref)

    @pl.when(t <= 2)
    def _():
        ci = lax.broadcasted_iota(jnp.int32, (QB_ATT, kw), 0) // CHUNK
        col = lax.broadcasted_iota(jnp.int32, (QB_ATT, kw), 1)
        cj = col // CHUNK
        in_band = jnp.logical_and(cj >= ci, cj <= ci + ATT_PAST_CHUNKS)
        for qb in range(n_qb):
            valid = jnp.logical_and(in_band, col >= hist - qb * QB_ATT - tile * ta)
            for h in range(ATT_HEADS):
                rows = jnp.broadcast_to(g_ref[h:h + 1, :], (QB_ATT, period))
                skew = pltpu.roll(rows, 0, axis=1, stride=1, stride_axis=0)
                bias_ref[qb, h] = jnp.where(valid, skew[:, :kw] * LOG2E, NEG_INF)

    def step(wbuf, rbuf):
        def zput(c0, block):
            wbuf[:, c0:c0 + block.shape[1]] = block

        def zget(rows, c0, width):
            return rbuf[rows, c0:c0 + width]

        pending = _inproj_blocks(x_ref, ng_ref, w_ref, base_ref, off_ref, zput)

        def issue_projection_block():
            if pending:
                pending.pop(0)()

        kbuf[hist:hist + ta, :] = zget(slice(0, ta), Z_ATT_K, ATT_W)
        vbuf[hist:hist + ta, :] = zget(slice(0, ta), Z_ATT_V, ATT_W)

        lane = lax.broadcasted_iota(jnp.int32, (1, 2 * ATT_HEAD_DIM), 1)
        low = lane < ATT_HEAD_DIM
        for qb in range(n_qb):
            rows = slice(qb * QB_ATT, (qb + 1) * QB_ATT)
            krows = slice(qb * QB_ATT, qb * QB_ATT + kw)
            for hp in range(ATT_HEADS // 2):
                lanes = slice(hp * 2 * ATT_HEAD_DIM, (hp + 1) * 2 * ATT_HEAD_DIM)
                qp = zget(rows, Z_ATT_Q + lanes.start, 2 * ATT_HEAD_DIM)
                kp = kbuf[krows, lanes]
                vp = vbuf[krows, lanes]
                outs = []
                for e in range(2):
                    keep = low if e == 0 else jnp.logical_not(low)
                    qm = jnp.where(keep, qp, jnp.zeros_like(qp))
                    s = _dot_nt(qm, kp)
                    issue_projection_block()
                    s = s + bias_ref[qb, 2 * hp + e]
                    m = jnp.max(s, axis=-1, keepdims=True)
                    p = jnp.exp2(s - m).astype(BF16)
                    ov = _dot(p, jnp.where(keep, vp, jnp.ones_like(vp)))
                    outs.append(ov / pltpu.roll(ov, ATT_HEAD_DIM, axis=1))
                o_ref[0, rows, lanes] = jnp.where(low, outs[0], outs[1]).astype(BF16)
            for r0 in range(qb * QB_ATT, (qb + 1) * QB_ATT, TR_RET):
                _ret_tile(tile_dec, slice(r0, r0 + TR_RET), zget, d_ref, qd_ref, kd_ref,
                          ro_ref, state_ref)
        while pending:
            issue_projection_block()

        kbuf[0:hist, :] = kbuf[ta:ta + hist, :]
        vbuf[0:hist, :] = vbuf[ta:ta + hist, :]

    @pl.when(t % 2 == 0)
    def _():
        step(zbuf0, zbuf1)

    @pl.when(t % 2 == 1)
    def _():
        step(zbuf1, zbuf0)


def _layer0_mixers(x, norm_g, w_in, rel_bias):
    B, T, D = x.shape
    ta = TA_ATT
    n_t = T // ta
    hist = ATT_PAST_CHUNKS * CHUNK
    assert ta == hist and n_t >= 2 and QB_ATT % TR_RET == 0 and AB_IN_W % IN_CB == 0
    g = _att_bias_vectors(rel_bias)
    dmat, qdec, kdec, tile_dec = _ret_tables(TR_RET)
    bases, offs = _rotary_tables(T, ta)
    proj_tile = lambda b, t: (b, jnp.minimum(t, n_t - 1), 0)
    mix_tile = lambda b, t: (b, jnp.maximum(t - 1, 0), 0)
    return pl.pallas_call(
        functools.partial(_layer0_kernel, tile_dec),
        grid=(B, n_t + 1),
        in_specs=[
            pl.BlockSpec((1, ta, D), proj_tile),
            _const_spec((1, D)),
            _const_spec((D, AB_IN_W)),
            pl.BlockSpec((1, 2, RET_QK_DIM), lambda b, t: (jnp.minimum(t, n_t - 1), 0, 0)),
            _const_spec(offs.shape),
            _const_spec(g.shape),
            _const_spec(dmat.shape), _const_spec(qdec.shape), _const_spec(kdec.shape),
        ],
        out_specs=[pl.BlockSpec((1, ta, ATT_W), mix_tile),
                   pl.BlockSpec((1, ta, RET_V_W), mix_tile)],
        out_shape=[jax.ShapeDtypeStruct((B, T, ATT_W), BF16),
                   jax.ShapeDtypeStruct((B, T, RET_V_W), BF16)],
        scratch_shapes=[pltpu.VMEM((ta, AB_IN_W), BF16),
                        pltpu.VMEM((ta, AB_IN_W), BF16),
                        pltpu.VMEM((hist + ta, ATT_W), BF16),
                        pltpu.VMEM((hist + ta, ATT_W), BF16),
                        pltpu.VMEM((ta // QB_ATT, ATT_HEADS, QB_ATT, QB_ATT + hist), F32),
                        pltpu.VMEM((RET_HEADS, RET_QK_DIM, RET_V_DIM), F32)],
        compiler_params=_params(2),
        name="layer0_mixers",
    )(x, norm_g, w_in, bases, offs, g, dmat, qdec, kdec)


def _ffn_kernel(final_norm, mixer_proj, *refs):
    if mixer_proj:
        x_ref, ya_ref, yb_ref, wmix_ref = refs[:4]
        refs = refs[4:]
    else:
        x_ref = refs[0]
        refs = refs[1:]
    (g_ref, wup_ref, cw_ref, cb_ref, wdn_ref, fg_ref, o_ref,
     hn_ref, zs_ref, zcarry_ref, act_ref, ys_ref) = refs
    tm = x_ref.shape[1]
    pad = SUBLANES
    n_slab = FFN_CB // LANES
    half_rows = tm // 2
    n_lane_slabs = D_MODEL // LANES

    @pl.when(pl.program_id(1) == 0)
    def _():
        zcarry_ref[...] = jnp.zeros(zcarry_ref.shape, F32)

    if mixer_proj:
        na = ya_ref.shape[2]
        mix = _dot(ya_ref[0], wmix_ref[0:na, :]) + _dot(yb_ref[0], wmix_ref[na:, :])
        for l in range(n_lane_slabs):
            ys_ref[l] = mix[:, l * LANES:(l + 1) * LANES]
    for r0 in range(0, tm, NORM_ROWS):
        rows = slice(r0, r0 + NORM_ROWS)
        xs = x_ref[0, rows, :]
        if mixer_proj:
            xs = xs + jnp.concatenate([ys_ref[l, rows, :] for l in range(n_lane_slabs)], axis=1)
            o_ref[0, rows, :] = xs
        hn_ref[rows, :] = _rms_norm(xs, g_ref[...]).astype(BF16)
    lhs = hn_ref[...]
    n_blocks = FFN_HIDDEN // FFN_CB

    def up_block(j):
        for half in range(2):
            c0 = half * FFN_HIDDEN + j * FFN_CB
            z = _dot(lhs, wup_ref[:, c0:c0 + FFN_CB])
            for l in range(n_slab):
                slab = c0 // LANES + l
                zslab = z[:, l * LANES:(l + 1) * LANES]
                zs_ref[j % 2, half * n_slab + l, pad - SUBLANES:pad, :] = zcarry_ref[slab]
                zs_ref[j % 2, half * n_slab + l, pad:pad + tm, :] = zslab
                zcarry_ref[slab] = zslab[tm - SUBLANES:, :]

    up_block(0)
    for j in range(n_blocks):
        buf = j % 2
        if j + 1 < n_blocks:
            up_block(j + 1)
        for par in range(2):
            for l in range(n_slab):
                for u0 in range(0, half_rows, CONV_ROWS):
                    conv = []
                    for half in range(2):
                        c0 = half * FFN_HIDDEN + j * FFN_CB + l * LANES
                        c = cb_ref[:, c0:c0 + LANES]
                        for tap in range(CONV_WIDTH):
                            off = pad - (CONV_WIDTH - 1) + tap + par + 2 * u0
                            c = c + (zs_ref[buf, half * n_slab + l,
                                            pl.ds(off, CONV_ROWS, stride=2), :]
                                     * cw_ref[tap:tap + 1, c0:c0 + LANES])
                        conv.append(c)
                    a0 = j * FFN_CB + l * LANES
                    act_ref[par * half_rows + u0:par * half_rows + u0 + CONV_ROWS,
                            a0:a0 + LANES] = (_gelu_tanh(conv[0]) * conv[1]).astype(BF16)
    yp = _dot(act_ref[...], wdn_ref[...])
    for l in range(n_lane_slabs):
        for par in range(2):
            ys_ref[l, pl.ds(par, half_rows, stride=2), :] = (
                yp[par * half_rows:(par + 1) * half_rows, l * LANES:(l + 1) * LANES])
    for r0 in range(0, tm, NORM_ROWS):
        rows = slice(r0, r0 + NORM_ROWS)
        y = (o_ref if mixer_proj else x_ref)[0, rows, :] + jnp.concatenate(
            [ys_ref[l, rows, :] for l in range(n_lane_slabs)], axis=1)
        if final_norm:
            y = _rms_norm(y, fg_ref[...])
        o_ref[0, rows, :] = y


def _layer_spec(layer, shape):
    zeros = (0,) * len(shape)
    return pl.BlockSpec((None,) + tuple(shape), lambda *_: (layer,) + zeros,
                        pipeline_mode=pl.Buffered(1))


def _ffn(x, layer, g, w_up, conv_w, conv_b, w_down, final_g=None, mixer=None):
    B, T, D = x.shape
    tm = TM_FFN
    final_norm = final_g is not None
    if final_g is None:
        final_g = jnp.ones((D,), F32)
    tile = lambda width: pl.BlockSpec((1, tm, width), lambda b, t: (b, t, 0))
    mixer_args, mixer_specs = (), []
    if mixer is not None:
        ya, yb, w_mix = mixer
        mixer_args = (ya, yb, w_mix)
        mixer_specs = [tile(ya.shape[2]), tile(yb.shape[2]), _const_spec(w_mix.shape)]
    return pl.pallas_call(
        functools.partial(_ffn_kernel, final_norm, mixer is not None),
        grid=(B, T // tm),
        in_specs=[tile(D)] + mixer_specs + [
            _layer_spec(layer, (1, D)),
            _layer_spec(layer, (D, 2 * FFN_HIDDEN)),
            _layer_spec(layer, (CONV_WIDTH, 2 * FFN_HIDDEN)),
            _layer_spec(layer, (1, 2 * FFN_HIDDEN)),
            _layer_spec(layer, (FFN_HIDDEN, D)),
            _const_spec((1, D)),
        ],
        out_specs=pl.BlockSpec((1, tm, D), lambda b, t: (b, t, 0)),
        out_shape=jax.ShapeDtypeStruct((B, T, D), F32),
        scratch_shapes=[
            pltpu.VMEM((tm, D), BF16),
            pltpu.VMEM((2, 2 * FFN_CB // LANES, tm + SUBLANES, LANES), F32),
            pltpu.VMEM((2 * FFN_HIDDEN // LANES, SUBLANES, LANES), F32),
            pltpu.VMEM((tm, FFN_HIDDEN), BF16),
            pltpu.VMEM((D // LANES, tm, LANES), F32),
        ],
        compiler_params=_params(2),
        name="conv_ffn_final" if final_norm else "conv_ffn",
    )(x, *mixer_args, g, w_up, conv_w, conv_b, w_down, final_g[None, :])


def _sgu_kernel(x_ref, g_ref, win_ref, lng_ref, lnb_ref, ws_ref, bs_ref, wout_ref, o_ref,
                hn_ref, v_ref, u_ref, y_ref):
    tm = x_ref.shape[1]
    gw = SGU_WIDTH // SGU_GROUPS
    pair = 2 * SGU_BLOCK
    for r0 in range(0, tm, NORM_ROWS):
        rows = slice(r0, r0 + NORM_ROWS)
        hn_ref[rows, :] = _rms_norm(x_ref[0, rows, :], g_ref[...]).astype(BF16)
    hn = hn_ref[...]

    vsum = jnp.zeros((tm, 1), F32)
    for j in range(SGU_WIDTH // SGU_CB):
        cols = slice(j * SGU_CB, (j + 1) * SGU_CB)
        zv = _gelu_tanh(_dot(hn, win_ref[:, SGU_WIDTH + j * SGU_CB:SGU_WIDTH + (j + 1) * SGU_CB]))
        v_ref[:, cols] = zv
        vsum = vsum + jnp.sum(zv, axis=-1, keepdims=True)
    def u_block(j):
        cols = slice(j * gw, (j + 1) * gw)
        u_ref[:, cols] = _gelu_tanh(_dot(hn, win_ref[:, cols]))

    for j in range(U_LEAD):
        u_block(j)
    mu = vsum * (1.0 / SGU_WIDTH)
    vsq = jnp.zeros((tm, 1), F32)
    for j in range(SGU_WIDTH // SGU_CB):
        cols = slice(j * SGU_CB, (j + 1) * SGU_CB)
        vc = v_ref[:, cols] - mu
        vsq = vsq + jnp.sum(vc * vc, axis=-1, keepdims=True)
    rstd = lax.rsqrt(vsq * (1.0 / SGU_WIDTH) + EPS)

    ri = lax.broadcasted_iota(jnp.int32, (pair, pair), 0)
    ci = lax.broadcasted_iota(jnp.int32, (pair, pair), 1)
    allowed = (ci // CHUNK) <= (ri // CHUNK)
    same = (ci // SGU_BLOCK) == (ri // SGU_BLOCK)
    keep = jnp.logical_and(allowed, same)
    for gidx in range(SGU_GROUPS):
        cols = slice(gidx * gw, (gidx + 1) * gw)
        w = jnp.where(keep, ws_ref[gidx], jnp.zeros((pair, pair), BF16))
        vn = ((v_ref[:, cols] - mu) * rstd * lng_ref[:, cols] + lnb_ref[:, cols]).astype(BF16)
        for rp in range(tm // pair):
            rows = slice(rp * pair, (rp + 1) * pair)
            mixed = _dot(w, vn[rows]) + bs_ref[:, cols]
            y_ref[rows, cols] = (u_ref[rows, cols] * mixed).astype(BF16)
        if gidx + U_LEAD < SGU_GROUPS:
            u_block(gidx + U_LEAD)
    o_ref[0] = x_ref[0] + _dot(y_ref[...], wout_ref[...])


def _sgu(x, g, w_in, ln_g, ln_b, w_s, b_s, w_out):
    B, T, D = x.shape
    tm = TM_PROJ
    gw = SGU_WIDTH // SGU_GROUPS
    pair = 2 * SGU_BLOCK
    zero = jnp.zeros_like(w_s)
    w_bd = jnp.concatenate([jnp.concatenate([w_s, zero], axis=2),
                            jnp.concatenate([zero, w_s], axis=2)], axis=1).astype(BF16)
    b_tab = jnp.repeat(b_s.T, gw, axis=1)
    b_tab = jnp.concatenate([b_tab, b_tab], axis=0)
    return pl.pallas_call(
        _sgu_kernel,
        grid=(B, T // tm),
        in_specs=[
            pl.BlockSpec((1, tm, D), lambda b, t: (b, t, 0)),
            _const_spec((1, D)),
            _const_spec((D, 2 * SGU_WIDTH)),
            _const_spec((1, SGU_WIDTH)),
            _const_spec((1, SGU_WIDTH)),
            _const_spec((SGU_GROUPS, pair, pair)),
            _const_spec((pair, SGU_WIDTH)),
            _const_spec((SGU_WIDTH, D)),
        ],
        out_specs=pl.BlockSpec((1, tm, D), lambda b, t: (b, t, 0)),
        out_shape=jax.ShapeDtypeStruct((B, T, D), F32),
        scratch_shapes=[
            pltpu.VMEM((tm, D), BF16),
            pltpu.VMEM((tm, SGU_WIDTH), F32),
            pltpu.VMEM((tm, SGU_WIDTH), F32),
            pltpu.VMEM((tm, SGU_WIDTH), BF16),
        ],
        compiler_params=_params(2),
        name="sgu_mixer",
    )(x, g[None, :], w_in.astype(BF16), ln_g[None, :], ln_b[None, :], w_bd, b_tab,
      w_out.astype(BF16))


def kernel(x, attn_norm_g, ffn_norm_g, ab_w_in, ab_w_out, ab_rel_bias, c_w_in, c_ln_g, c_ln_b,
           c_w_s, c_b_s, c_w_out, ffn_w_up, ffn_conv_w, ffn_conv_b, ffn_w_down, final_norm_g):
    ffn_params = (ffn_norm_g[:, None, :], ffn_w_up.astype(BF16), ffn_conv_w,
                  ffn_conv_b[:, None, :], ffn_w_down.astype(BF16))
    yb, ya = _layer0_mixers(x, attn_norm_g[0][None, :], ab_w_in[0].astype(BF16), ab_rel_bias[0])
    h = _ffn(x, 0, *ffn_params, mixer=(ya, yb, ab_w_out[0].astype(BF16)))
    h = _sgu(h, attn_norm_g[1], c_w_in[0], c_ln_g[0], c_ln_b[0], c_w_s[0], c_b_s[0], c_w_out[0])
    h = _ffn(h, 1, *ffn_params, final_g=final_norm_g)
    return h
```

```python
import functools
import math

import numpy as np
import jax
import jax.numpy as jnp
from jax import lax
from jax.experimental import pallas as pl
from jax.experimental.pallas import tpu as pltpu

F32 = jnp.float32
BF16 = jnp.bfloat16

D_MODEL = 1024
CHUNK = 64
EPS = 1e-6
NEG_INF = -1e30
LOG2E = math.log2(math.e)

RET_HEADS = 4
RET_QK_DIM = 128
RET_V_DIM = 256
ATT_HEADS = 8
ATT_HEAD_DIM = 64
ATT_PAST_CHUNKS = 8
MAX_REL = 128
SGU_BLOCK = 128
SGU_GROUPS = 8
SGU_WIDTH = 2048
FFN_HIDDEN = 2816
CONV_WIDTH = 3

RET_QK_W = RET_HEADS * RET_QK_DIM
RET_V_W = RET_HEADS * RET_V_DIM
ATT_W = ATT_HEADS * ATT_HEAD_DIM
AB_IN_W = 2 * RET_QK_W + 2 * RET_V_W + 3 * ATT_W
Z_RET_Q = 0
Z_RET_K = Z_RET_Q + RET_QK_W
Z_RET_V = Z_RET_K + RET_QK_W
Z_RET_G = Z_RET_V + RET_V_W
Z_ATT_Q = Z_RET_G + RET_V_W
Z_ATT_K = Z_ATT_Q + ATT_W
Z_ATT_V = Z_ATT_K + ATT_W

VMEM_LIMIT_BYTES = 56 * 1024 * 1024
SUBLANES = 8
LANES = 128

TM_PROJ = 1024
TM_FFN = 512
TR_RET = 256
TA_ATT = 512
QB_ATT = 256
FFN_CB = 256
IN_CB = 256
SGU_CB = 256
U_LEAD = 3
NORM_ROWS = 32
CONV_ROWS = 64


def _const_spec(shape):
    zeros = (0,) * len(shape)
    return pl.BlockSpec(shape, lambda *_: zeros, pipeline_mode=pl.Buffered(1))


def _params(n_axes):
    return pltpu.CompilerParams(
        dimension_semantics=("arbitrary",) * n_axes,
        vmem_limit_bytes=VMEM_LIMIT_BYTES)


def _rms_norm(x, g):
    return x * lax.rsqrt(jnp.mean(x * x, axis=-1, keepdims=True) + EPS) * g


def _gelu_tanh(x):
    c = math.sqrt(2.0 / math.pi)
    return 0.5 * x * (1.0 + jnp.tanh(c * (x + 0.044715 * (x * x * x))))


def _dot(a, b):
    return jnp.dot(a, b, preferred_element_type=F32)


def _dot_nt(a, b):
    return lax.dot_general(a, b, (((1,), (1,)), ((), ())), preferred_element_type=F32)


def _dot_tn(a, b):
    return lax.dot_general(a, b, (((0,), (0,)), ((), ())), preferred_element_type=F32)


def _inproj_blocks(x_ref, g_ref, w_ref, base_ref, off_ref, zput):
    hn = _rms_norm(x_ref[0], g_ref[...]).astype(BF16)
    ca, sa = base_ref[0, 0:1, :], base_ref[0, 1:2, :]
    cos = ca * off_ref[0] - sa * off_ref[1]
    sin = sa * off_ref[2] + ca * off_ref[3]

    def block(j):
        z = _dot(hn, w_ref[:, j * IN_CB:(j + 1) * IN_CB])
        if j * IN_CB < Z_RET_V:
            parts = []
            for h in range(IN_CB // RET_QK_DIM):
                xh = z[:, h * RET_QK_DIM:(h + 1) * RET_QK_DIM]
                parts.append(xh * cos + pltpu.roll(xh, RET_QK_DIM // 2, axis=1) * sin)
            z = jnp.concatenate(parts, axis=1)
        elif Z_ATT_Q <= j * IN_CB < Z_ATT_K:
            z = z * (ATT_HEAD_DIM ** -0.5 * LOG2E)
        zput(j * IN_CB, z.astype(BF16))

    return [functools.partial(block, j) for j in range(AB_IN_W // IN_CB)]


def _rotary_tables(T, tm):
    half = RET_QK_DIM // 2
    inv = 1.0 / (np.float32(10000.0) ** np.linspace(0.0, 1.0, half, dtype=np.float32))
    inv = np.concatenate([inv, inv]).astype(np.float64)
    sign = np.concatenate([-np.ones(half), np.ones(half)])
    base = (np.arange(T // tm) * tm)[:, None] * inv[None, :]
    off = np.arange(tm)[:, None] * inv[None, :]
    bases = np.stack([np.cos(base), np.sin(base)], axis=1)
    offs = np.stack([np.cos(off), np.sin(off), sign * np.cos(off), sign * np.sin(off)])
    return jnp.asarray(bases, F32), jnp.asarray(offs, F32)


def _ret_tables(tr):
    h = np.arange(RET_HEADS, dtype=np.float64)
    log_g = np.log1p(-np.exp2(-5.0 - h))
    n = np.arange(tr)
    cn, cm = n[:, None] // CHUNK, n[None, :] // CHUNK
    diff = (n[:, None] - n[None, :]).astype(np.float64)
    expo = np.where(cn == cm, np.abs(diff), diff)
    dmat = np.where(cm <= cn, np.exp(log_g[:, None, None] * expo), 0.0)
    scale = RET_QK_DIM ** -0.5
    qdec = np.exp(log_g[:, None] * (n + 1.0)[None, :])
    kdec = np.exp(log_g[:, None] * (tr - 1.0 - n)[None, :]) * scale
    qdec = np.broadcast_to(qdec[:, :, None], (RET_HEADS, tr, RET_QK_DIM))
    kdec = np.broadcast_to(kdec[:, :, None], (RET_HEADS, tr, RET_QK_DIM))
    tile_dec = [float(v) for v in np.exp(log_g * tr)]
    return (jnp.asarray(dmat * scale, F32), jnp.asarray(qdec, F32), jnp.asarray(kdec, F32),
            tile_dec)


def _ret_tile(tile_dec, rows, zget, d_ref, qd_ref, kd_ref, o_ref, state_ref):
    heads = range(RET_HEADS)
    q = [zget(rows, Z_RET_Q + h * RET_QK_DIM, RET_QK_DIM) for h in heads]
    k = [zget(rows, Z_RET_K + h * RET_QK_DIM, RET_QK_DIM) for h in heads]
    scores = [_dot_nt(q[h], k[h]) for h in heads]
    for h in heads:
        vv = slice(h * RET_V_DIM, (h + 1) * RET_V_DIM)
        v = zget(rows, Z_RET_V + h * RET_V_DIM, RET_V_DIM)
        state = state_ref[h]
        qd = (q[h].astype(F32) * qd_ref[h]).astype(BF16)
        kd = (k[h].astype(F32) * kd_ref[h]).astype(BF16)
        inter = _dot(qd, state.astype(BF16))
        state_ref[h] = state * tile_dec[h] + _dot_tn(kd, v)
        s = scores[h] * d_ref[h]
        r = _dot(s.astype(BF16), v) + inter
        mu = jnp.mean(r, axis=-1, keepdims=True)
        rc = r - mu
        var = jnp.mean(rc * rc, axis=-1, keepdims=True)
        rn = rc * lax.rsqrt(var + EPS)
        gate = zget(rows, Z_RET_G + h * RET_V_DIM, RET_V_DIM).astype(F32)
        o_ref[0, rows, vv] = (0.5 * gate * (1.0 + jnp.tanh(0.5 * gate)) * rn).astype(BF16)


def _att_bias_vectors(rel_bias):
    hist = ATT_PAST_CHUNKS * CHUNK
    kw = QB_ATT + hist
    nr = 2 * MAX_REL + 1
    period = 2 * hist
    assert period >= kw + QB_ATT - 1 and hist > MAX_REL
    n_far = hist - MAX_REL + 1
    n_near = kw - (hist + MAX_REL)
    n_wrap = period - kw
    first, last = rel_bias[:, 0:1], rel_bias[:, nr - 1:nr]
    return jnp.concatenate([
        jnp.broadcast_to(last, (ATT_HEADS, n_far)),
        rel_bias[:, nr - 2:0:-1],
        jnp.broadcast_to(first, (ATT_HEADS, n_near)),
        jnp.broadcast_to(last, (ATT_HEADS, n_wrap))], axis=1).astype(F32)


def _layer0_kernel(tile_dec, x_ref, ng_ref, w_ref, base_ref, off_ref, g_ref, d_ref, qd_ref,
                   kd_ref, o_ref, ro_ref, zbuf0, zbuf1, kbuf, vbuf, bias_ref, state_ref):
    t = pl.program_id(1)
    ta = TA_ATT
    hist = ATT_PAST_CHUNKS * CHUNK
    kw = QB_ATT + hist
    period = g_ref.shape[1]
    n_qb = ta // QB_ATT
    tile = jnp.maximum(t - 1, 0)

    @pl.when(t == 0)
    def _():
        zbuf1[...] = jnp.zeros(zbuf1.shape, BF16)

    @pl.when(t <= 1)
    def _():
        kbuf[0:hist, :] = jnp.zeros((hist, ATT_W), BF16)
        vbuf[0:hist, :] = jnp.zeros((hist, ATT_W), BF16)
        state_ref[...] = jnp.zeros_like(state_ref)

    @pl.when(t <= 2)
    def _():
        ci = lax.broadcasted_iota(jnp.int32, (QB_ATT, kw), 0) // CHUNK
        col = lax.broadcasted_iota(jnp.int32, (QB_ATT, kw), 1)
        cj = col // CHUNK
        in_band = jnp.logical_and(cj >= ci, cj <= ci + ATT_PAST_CHUNKS)
        for qb in range(n_qb):
            valid = jnp.logical_and(in_band, col >= hist - qb * QB_ATT - tile * ta)
            for h in range(ATT_HEADS):
                rows = jnp.broadcast_to(g_ref[h:h + 1, :], (QB_ATT, period))
                skew = pltpu.roll(rows, 0, axis=1, stride=1, stride_axis=0)
                bias_ref[qb, h] = jnp.where(valid, skew[:, :kw] * LOG2E, NEG_INF)

    def step(wbuf, rbuf):
        def zput(c0, block):
            wbuf[:, c0:c0 + block.shape[1]] = block

        def zget(rows, c0, width):
            return rbuf[rows, c0:c0 + width]

        pending = _inproj_blocks(x_ref, ng_ref, w_ref, base_ref, off_ref, zput)

        def issue_projection_block():
            if pending:
                pending.pop(0)()

        kbuf[hist:hist + ta, :] = zget(slice(0, ta), Z_ATT_K, ATT_W)
        vbuf[hist:hist + ta, :] = zget(slice(0, ta), Z_ATT_V, ATT_W)

        lane = lax.broadcasted_iota(jnp.int32, (1, 2 * ATT_HEAD_DIM), 1)
        low = lane < ATT_HEAD_DIM
        for qb in range(n_qb):
            rows = slice(qb * QB_ATT, (qb + 1) * QB_ATT)
            krows = slice(qb * QB_ATT, qb * QB_ATT + kw)
            for hp in range(ATT_HEADS // 2):
                lanes = slice(hp * 2 * ATT_HEAD_DIM, (hp + 1) * 2 * ATT_HEAD_DIM)
                qp = zget(rows, Z_ATT_Q + lanes.start, 2 * ATT_HEAD_DIM)
                kp = kbuf[krows, lanes]
                vp = vbuf[krows, lanes]
                outs = []
                for e in range(2):
                    keep = low if e == 0 else jnp.logical_not(low)
                    qm = jnp.where(keep, qp, jnp.zeros_like(qp))
                    s = _dot_nt(qm, kp)
                    issue_projection_block()
                    s = s + bias_ref[qb, 2 * hp + e]
                    m = jnp.max(s, axis=-1, keepdims=True)
                    p = jnp.exp2(s - m).astype(BF16)
                    ov = _dot(p, jnp.where(keep, vp, jnp.ones_like(vp)))
                    outs.append(ov / pltpu.roll(ov, ATT_HEAD_DIM, axis=1))
                o_ref[0, rows, lanes] = jnp.where(low, outs[0], outs[1]).astype(BF16)
            for r0 in range(qb * QB_ATT, (qb + 1) * QB_ATT, TR_RET):
                _ret_tile(tile_dec, slice(r0, r0 + TR_RET), zget, d_ref, qd_ref, kd_ref,
                          ro_ref, state_ref)
        while pending:
            issue_projection_block()

        kbuf[0:hist, :] = kbuf[ta:ta + hist, :]
        vbuf[0:hist, :] = vbuf[ta:ta + hist, :]

    @pl.when(t % 2 == 0)
    def _():
        step(zbuf0, zbuf1)

    @pl.when(t % 2 == 1)
    def _():
        step(zbuf1, zbuf0)


def _layer0_mixers(x, norm_g, w_in, rel_bias):
    B, T, D = x.shape
    ta = TA_ATT
    n_t = T // ta
    hist = ATT_PAST_CHUNKS * CHUNK
    assert ta == hist and n_t >= 2 and QB_ATT % TR_RET == 0 and AB_IN_W % IN_CB == 0
    g = _att_bias_vectors(rel_bias)
    dmat, qdec, kdec, tile_dec = _ret_tables(TR_RET)
    bases, offs = _rotary_tables(T, ta)
    proj_tile = lambda b, t: (b, jnp.minimum(t, n_t - 1), 0)
    mix_tile = lambda b, t: (b, jnp.maximum(t - 1, 0), 0)
    return pl.pallas_call(
        functools.partial(_layer0_kernel, tile_dec),
        grid=(B, n_t + 1),
        in_specs=[
            pl.BlockSpec((1, ta, D), proj_tile),
            _const_spec((1, D)),
            _const_spec((D, AB_IN_W)),
            pl.BlockSpec((1, 2, RET_QK_DIM), lambda b, t: (jnp.minimum(t, n_t - 1), 0, 0)),
            _const_spec(offs.shape),
            _const_spec(g.shape),
            _const_spec(dmat.shape), _const_spec(qdec.shape), _const_spec(kdec.shape),
        ],
        out_specs=[pl.BlockSpec((1, ta, ATT_W), mix_tile),
                   pl.BlockSpec((1, ta, RET_V_W), mix_tile)],
        out_shape=[jax.ShapeDtypeStruct((B, T, ATT_W), BF16),
                   jax.ShapeDtypeStruct((B, T, RET_V_W), BF16)],
        scratch_shapes=[pltpu.VMEM((ta, AB_IN_W), BF16),
                        pltpu.VMEM((ta, AB_IN_W), BF16),
                        pltpu.VMEM((hist + ta, ATT_W), BF16),
                        pltpu.VMEM((hist + ta, ATT_W), BF16),
                        pltpu.VMEM((ta // QB_ATT, ATT_HEADS, QB_ATT, QB_ATT + hist), F32),
                        pltpu.VMEM((RET_HEADS, RET_QK_DIM, RET_V_DIM), F32)],
        compiler_params=_params(2),
        name="layer0_mixers",
    )(x, norm_g, w_in, bases, offs, g, dmat, qdec, kdec)


def _ffn_kernel(final_norm, mixer_proj, *refs):
    if mixer_proj:
        x_ref, ya_ref, yb_ref, wmix_ref = refs[:4]
        refs = refs[4:]
    else:
        x_ref = refs[0]
        refs = refs[1:]
    (g_ref, wup_ref, cw_ref, cb_ref, wdn_ref, fg_ref, o_ref,
     hn_ref, zs_ref, zcarry_ref, act_ref, ys_ref) = refs
    tm = x_ref.shape[1]
    pad = SUBLANES
    n_slab = FFN_CB // LANES
    half_rows = tm // 2
    n_lane_slabs = D_MODEL // LANES

    @pl.when(pl.program_id(1) == 0)
    def _():
        zcarry_ref[...] = jnp.zeros(zcarry_ref.shape, F32)

    if mixer_proj:
        na = ya_ref.shape[2]
        mix = _dot(ya_ref[0], wmix_ref[0:na, :]) + _dot(yb_ref[0], wmix_ref[na:, :])
        for l in range(n_lane_slabs):
            ys_ref[l] = mix[:, l * LANES:(l + 1) * LANES]
    for r0 in range(0, tm, NORM_ROWS):
        rows = slice(r0, r0 + NORM_ROWS)
        xs = x_ref[0, rows, :]
        if mixer_proj:
            xs = xs + jnp.concatenate([ys_ref[l, rows, :] for l in range(n_lane_slabs)], axis=1)
            o_ref[0, rows, :] = xs
        hn_ref[rows, :] = _rms_norm(xs, g_ref[...]).astype(BF16)
    lhs = hn_ref[...]
    n_blocks = FFN_HIDDEN // FFN_CB

    def up_block(j):
        for half in range(2):
            c0 = half * FFN_HIDDEN + j * FFN_CB
            z = _dot(lhs, wup_ref[:, c0:c0 + FFN_CB])
            for l in range(n_slab):
                slab = c0 // LANES + l
                zslab = z[:, l * LANES:(l + 1) * LANES]
                zs_ref[j % 2, half * n_slab + l, pad - SUBLANES:pad, :] = zcarry_ref[slab]
                zs_ref[j % 2, half * n_slab + l, pad:pad + tm, :] = zslab
                zcarry_ref[slab] = zslab[tm - SUBLANES:, :]

    up_block(0)
    for j in range(n_blocks):
        buf = j % 2
        if j + 1 < n_blocks:
            up_block(j + 1)
        for par in range(2):
            for l in range(n_slab):
                for u0 in range(0, half_rows, CONV_ROWS):
                    conv = []
                    for half in range(2):
                        c0 = half * FFN_HIDDEN + j * FFN_CB + l * LANES
                        c = cb_ref[:, c0:c0 + LANES]
                        for tap in range(CONV_WIDTH):
                            off = pad - (CONV_WIDTH - 1) + tap + par + 2 * u0
                            c = c + (zs_ref[buf, half * n_slab + l,
                                            pl.ds(off, CONV_ROWS, stride=2), :]
                                     * cw_ref[tap:tap + 1, c0:c0 + LANES])
                        conv.append(c)
                    a0 = j * FFN_CB + l * LANES
                    act_ref[par * half_rows + u0:par * half_rows + u0 + CONV_ROWS,
                            a0:a0 + LANES] = (_gelu_tanh(conv[0]) * conv[1]).astype(BF16)
    yp = _dot(act_ref[...], wdn_ref[...])
    for l in range(n_lane_slabs):
        for par in range(2):
            ys_ref[l, pl.ds(par, half_rows, stride=2), :] = (
                yp[par * half_rows:(par + 1) * half_rows, l * LANES:(l + 1) * LANES])
    for r0 in range(0, tm, NORM_ROWS):
        rows = slice(r0, r0 + NORM_ROWS)
        y = (o_ref if mixer_proj else x_ref)[0, rows, :] + jnp.concatenate(
            [ys_ref[l, rows, :] for l in range(n_lane_slabs)], axis=1)
        if final_norm:
            y = _rms_norm(y, fg_ref[...])
        o_ref[0, rows, :] = y


def _layer_spec(layer, shape):
    zeros = (0,) * len(shape)
    return pl.BlockSpec((None,) + tuple(shape), lambda *_: (layer,) + zeros,
                        pipeline_mode=pl.Buffered(1))


def _ffn(x, layer, g, w_up, conv_w, conv_b, w_down, final_g=None, mixer=None):
    B, T, D = x.shape
    tm = TM_FFN
    final_norm = final_g is not None
    if final_g is None:
        final_g = jnp.ones((D,), F32)
    tile = lambda width: pl.BlockSpec((1, tm, width), lambda b, t: (b, t, 0))
    mixer_args, mixer_specs = (), []
    if mixer is not None:
        ya, yb, w_mix = mixer
        mixer_args = (ya, yb, w_mix)
        mixer_specs = [tile(ya.shape[2]), tile(yb.shape[2]), _const_spec(w_mix.shape)]
    return pl.pallas_call(
        functools.partial(_ffn_kernel, final_norm, mixer is not None),
        grid=(B, T // tm),
        in_specs=[tile(D)] + mixer_specs + [
            _layer_spec(layer, (1, D)),
            _layer_spec(layer, (D, 2 * FFN_HIDDEN)),
            _layer_spec(layer, (CONV_WIDTH, 2 * FFN_HIDDEN)),
            _layer_spec(layer, (1, 2 * FFN_HIDDEN)),
            _layer_spec(layer, (FFN_HIDDEN, D)),
            _const_spec((1, D)),
        ],
        out_specs=pl.BlockSpec((1, tm, D), lambda b, t: (b, t, 0)),
        out_shape=jax.ShapeDtypeStruct((B, T, D), F32),
        scratch_shapes=[
            pltpu.VMEM((tm, D), BF16),
            pltpu.VMEM((2, 2 * FFN_CB // LANES, tm + SUBLANES, LANES), F32),
            pltpu.VMEM((2 * FFN_HIDDEN // LANES, SUBLANES, LANES), F32),
            pltpu.VMEM((tm, FFN_HIDDEN), BF16),
            pltpu.VMEM((D // LANES, tm, LANES), F32),
        ],
        compiler_params=_params(2),
        name="conv_ffn_final" if final_norm else "conv_ffn",
    )(x, *mixer_args, g, w_up, conv_w, conv_b, w_down, final_g[None, :])


def _sgu_kernel(x_ref, g_ref, win_ref, lng_ref, lnb_ref, ws_ref, bs_ref, wout_ref, o_ref,
                hn_ref, v_ref, u_ref, y_ref):
    tm = x_ref.shape[1]
    gw = SGU_WIDTH // SGU_GROUPS
    pair = 2 * SGU_BLOCK
    for r0 in range(0, tm, NORM_ROWS):
        rows = slice(r0, r0 + NORM_ROWS)
        hn_ref[rows, :] = _rms_norm(x_ref[0, rows, :], g_ref[...]).astype(BF16)
    hn = hn_ref[...]

    vsum = jnp.zeros((tm, 1), F32)
    for j in range(SGU_WIDTH // SGU_CB):
        cols = slice(j * SGU_CB, (j + 1) * SGU_CB)
        zv = _gelu_tanh(_dot(hn, win_ref[:, SGU_WIDTH + j * SGU_CB:SGU_WIDTH + (j + 1) * SGU_CB]))
        v_ref[:, cols] = zv
        vsum = vsum + jnp.sum(zv, axis=-1, keepdims=True)
    def u_block(j):
        cols = slice(j * gw, (j + 1) * gw)
        u_ref[:, cols] = _gelu_tanh(_dot(hn, win_ref[:, cols]))

    for j in range(U_LEAD):
        u_block(j)
    mu = vsum * (1.0 / SGU_WIDTH)
    vsq = jnp.zeros((tm, 1), F32)
    for j in range(SGU_WIDTH // SGU_CB):
        cols = slice(j * SGU_CB, (j + 1) * SGU_CB)
        vc = v_ref[:, cols] - mu
        vsq = vsq + jnp.sum(vc * vc, axis=-1, keepdims=True)
    rstd = lax.rsqrt(vsq * (1.0 / SGU_WIDTH) + EPS)

    ri = lax.broadcasted_iota(jnp.int32, (pair, pair), 0)
    ci = lax.broadcasted_iota(jnp.int32, (pair, pair), 1)
    allowed = (ci // CHUNK) <= (ri // CHUNK)
    same = (ci // SGU_BLOCK) == (ri // SGU_BLOCK)
    keep = jnp.logical_and(allowed, same)
    for gidx in range(SGU_GROUPS):
        cols = slice(gidx * gw, (gidx + 1) * gw)
        w = jnp.where(keep, ws_ref[gidx], jnp.zeros((pair, pair), BF16))
        vn = ((v_ref[:, cols] - mu) * rstd * lng_ref[:, cols] + lnb_ref[:, cols]).astype(BF16)
        for rp in range(tm // pair):
            rows = slice(rp * pair, (rp + 1) * pair)
            mixed = _dot(w, vn[rows]) + bs_ref[:, cols]
            y_ref[rows, cols] = (u_ref[rows, cols] * mixed).astype(BF16)
        if gidx + U_LEAD < SGU_GROUPS:
            u_block(gidx + U_LEAD)
    o_ref[0] = x_ref[0] + _dot(y_ref[...], wout_ref[...])


def _sgu(x, g, w_in, ln_g, ln_b, w_s, b_s, w_out):
    B, T, D = x.shape
    tm = TM_PROJ
    gw = SGU_WIDTH // SGU_GROUPS
    pair = 2 * SGU_BLOCK
    zero = jnp.zeros_like(w_s)
    w_bd = jnp.concatenate([jnp.concatenate([w_s, zero], axis=2),
                            jnp.concatenate([zero, w_s], axis=2)], axis=1).astype(BF16)
    b_tab = jnp.repeat(b_s.T, gw, axis=1)
    b_tab = jnp.concatenate([b_tab, b_tab], axis=0)
    return pl.pallas_call(
        _sgu_kernel,
        grid=(B, T // tm),
        in_specs=[
            pl.BlockSpec((1, tm, D), lambda b, t: (b, t, 0)),
            _const_spec((1, D)),
            _const_spec((D, 2 * SGU_WIDTH)),
            _const_spec((1, SGU_WIDTH)),
            _const_spec((1, SGU_WIDTH)),
            _const_spec((SGU_GROUPS, pair, pair)),
            _const_spec((pair, SGU_WIDTH)),
            _const_spec((SGU_WIDTH, D)),
        ],
        out_specs=pl.BlockSpec((1, tm, D), lambda b, t: (b, t, 0)),
        out_shape=jax.ShapeDtypeStruct((B, T, D), F32),
        scratch_shapes=[
            pltpu.VMEM((tm, D), BF16),
            pltpu.VMEM((tm, SGU_WIDTH), F32),
            pltpu.VMEM((tm, SGU_WIDTH), F32),
            pltpu.VMEM((tm, SGU_WIDTH), BF16),
        ],
        compiler_params=_params(2),
        name="sgu_mixer",
    )(x, g[None, :], w_in.astype(BF16), ln_g[None, :], ln_b[None, :], w_bd, b_tab,
      w_out.astype(BF16))


def kernel(x, attn_norm_g, ffn_norm_g, ab_w_in, ab_w_out, ab_rel_bias, c_w_in, c_ln_g, c_ln_b,
           c_w_s, c_b_s, c_w_out, ffn_w_up, ffn_conv_w, ffn_conv_b, ffn_w_down, final_norm_g):
    ffn_params = (ffn_norm_g[:, None, :], ffn_w_up.astype(BF16), ffn_conv_w,
                  ffn_conv_b[:, None, :], ffn_w_down.astype(BF16))
    yb, ya = _layer0_mixers(x, attn_norm_g[0][None, :], ab_w_in[0].astype(BF16), ab_rel_bias[0])
    h = _ffn(x, 0, *ffn_params, mixer=(ya, yb, ab_w_out[0].astype(BF16)))
    h = _sgu(h, attn_norm_g[1], c_w_in[0], c_ln_g[0], c_ln_b[0], c_w_s[0], c_b_s[0], c_w_out[0])
    h = _ffn(h, 1, *ffn_params, final_g=final_norm_g)
    return h
```
